```python
import math
import jax, jax.numpy as jnp
from jax import lax
import numpy as np

D_MODEL = 1024
BATCH = 2
SEQ = 8192
DEPTH = 2
DEC_BATCH = 8
DEC_SEQ = 32
PAST_LEN = 2048

CHUNK = 64
N_A = DEPTH // 2
N_B = DEPTH - N_A
POOL_WINDOWS = (2, 4, 8, 16)
POOL_GROUP = D_MODEL // 8
POOL_WIDTH = 4 * POOL_GROUP
POOL_HIST = max(POOL_WINDOWS) - 1
N_DIFF_HEADS = 4
DIFF_HEAD_DIM = D_MODEL // 16
DIFF_V_DIM = 2 * DIFF_HEAD_DIM
DIFF_Q_WIDTH = N_DIFF_HEADS * 2 * DIFF_HEAD_DIM
DIFF_OUT_WIDTH = N_DIFF_HEADS * DIFF_V_DIM
N_MEM = 256
N_MEM_HEADS = 4
MEM_HEAD_DIM = D_MODEL // 8
MEM_WIDTH = N_MEM_HEADS * MEM_HEAD_DIM
MIX_IN = POOL_WIDTH + MEM_WIDTH
MIX_OUT = POOL_WIDTH + MEM_WIDTH
D_FF = 4 * D_MODEL
Q_BLOCK = 128
EPS = 1e-6

kernel_name = "yoco_pool_diffattn_streaming_step"


def rmsnorm(x, g):
    xf = x.astype(jnp.float32)
    y = xf * lax.rsqrt(jnp.mean(xf * xf, axis=-1, keepdims=True) + EPS)
    return (y * g.astype(jnp.float32)).astype(x.dtype)


def sq_relu_mlp(x, w1, w2):
    h = jax.nn.relu(x @ w1)
    return (h * h) @ w2


def lambda_init(layer_idx):
    return 0.8 - 0.6 * math.exp(-0.3 * layer_idx)


def alibi_slopes():
    return 2.0 ** (-8.0 * jnp.arange(1, N_DIFF_HEADS + 1, dtype=jnp.float32) / N_DIFF_HEADS)


def pool_mix(u, hist, pos, w_grp, scale):
    B, T, P = u.shape
    full = jnp.concatenate([hist, u], axis=1).astype(jnp.float32)
    cs = jnp.concatenate([jnp.zeros((B, 1, P), jnp.float32), jnp.cumsum(full, axis=1)], axis=1)
    end = cs[:, POOL_HIST + 1:]
    uf = u.astype(jnp.float32)
    outs = []
    for g, w in enumerate(POOL_WINDOWS):
        sl = slice(g * POOL_GROUP, (g + 1) * POOL_GROUP)
        start = cs[:, POOL_HIST + 1 - w: POOL_HIST + 1 - w + T, sl]
        cnt = jnp.minimum(pos + 1, w).astype(jnp.float32)[None, :, None]
        d = ((end[..., sl] - start) / cnt - uf[..., sl]).astype(u.dtype)
        outs.append(d @ w_grp[g])
    return jnp.concatenate(outs, axis=-1) * scale


def mem_kv(mem, g, w):
    B = mem.shape[0]
    kv = rmsnorm(mem, g) @ w
    k, v = jnp.split(kv, 2, axis=-1)
    return (k.reshape(B, N_MEM, N_MEM_HEADS, MEM_HEAD_DIM),
            v.reshape(B, N_MEM, N_MEM_HEADS, MEM_HEAD_DIM))


def mem_attend(q, k, v):
    B, T = q.shape[:2]
    s = jnp.einsum('bthd,bmhd->bhtm', q, k).astype(jnp.float32) * (MEM_HEAD_DIM ** -0.5)
    p = jax.nn.softmax(s, axis=-1).astype(v.dtype)
    return jnp.einsum('bhtm,bmhd->bthd', p, v).reshape(B, T, MEM_WIDTH)


def diff_core(q, k, v, q_pos, k_pos, lam, slopes):
    s = jnp.einsum('bqhcd,bkhcd->cbhqk', q, k).astype(jnp.float32) * (DIFF_HEAD_DIM ** -0.5)
    dist = jnp.abs(q_pos[:, None] - k_pos[None, :]).astype(jnp.float32)
    bias = -slopes[:, None, None] * dist[None]
    vis = (k_pos[None, :] // CHUNK) <= (q_pos[:, None] // CHUNK)
    s = jnp.where(vis, s + bias, -jnp.inf)
    p = jax.nn.softmax(s, axis=-1)
    a = p[0] - lam * p[1]
    return jnp.einsum('bhqk,bkhe->bqhe', a.astype(v.dtype), v)


def diff_attn_blocks(q, k, v, lam, slopes):
    B, T = q.shape[:2]
    nb = T // Q_BLOCK
    qb = q.reshape(B, nb, Q_BLOCK, N_DIFF_HEADS, 2, DIFF_HEAD_DIM).transpose(1, 0, 2, 3, 4, 5)
    k_pos = jnp.arange(T)

    def one(args):
        qi, i = args
        q_pos = i * Q_BLOCK + jnp.arange(Q_BLOCK)
        return diff_core(qi, k, v, q_pos, k_pos, lam, slopes)

    o = lax.map(one, (qb, jnp.arange(nb)))
    return o.transpose(1, 0, 2, 3, 4).reshape(B, T, N_DIFF_HEADS, DIFF_V_DIM)


def trunk(x, pos, pool_hist, mem_k, mem_v, past_k, past_v, blocked,
          g_attn, w_in, w_out, g_ffn, w_ff1, w_ff2, w_pool, pool_scale,
          lambda_qk, g_subln, g_kv, w_kv, g_final):
    B, T, _ = x.shape
    P0 = past_k.shape[1]
    slopes = alibi_slopes()
    new_pool = []
    new_k = new_v = k_all = v_all = None
    for l in range(DEPTH):
        h = rmsnorm(x, g_attn[l])
        z = h @ w_in[l]
        mix, qm = z[..., :MIX_IN - MEM_WIDTH], z[..., MIX_IN - MEM_WIDTH:]
        m_out = mem_attend(qm.reshape(B, T, N_MEM_HEADS, MEM_HEAD_DIM), mem_k[l], mem_v[l])
        if l < N_A:
            hist = pool_hist[l]
            new_pool.append(jnp.concatenate([hist, mix], axis=1)[:, -POOL_HIST:])
            t_out = pool_mix(mix, hist, pos, w_pool[l], pool_scale[l])
        else:
            b = l - N_A
            if new_k is None:
                kv = rmsnorm(x, g_kv) @ w_kv
                kk, vv = jnp.split(kv, 2, axis=-1)
                new_k = kk.reshape(B, T, N_DIFF_HEADS, DIFF_V_DIM)
                new_v = vv.reshape(B, T, N_DIFF_HEADS, DIFF_V_DIM)
                k_all = jnp.concatenate([past_k, new_k], axis=1).reshape(
                    B, P0 + T, N_DIFF_HEADS, 2, DIFF_HEAD_DIM)
                v_all = jnp.concatenate([past_v, new_v], axis=1)
            lq = lambda_qk[b].astype(jnp.float32)
            lam_i = lambda_init(l)
            lam = jnp.exp(jnp.sum(lq[0] * lq[1])) - jnp.exp(jnp.sum(lq[2] * lq[3])) + lam_i
            q = mix.reshape(B, T, N_DIFF_HEADS, 2, DIFF_HEAD_DIM)
            if blocked:
                o = diff_attn_blocks(q, k_all, v_all, lam, slopes)
            else:
                o = diff_core(q, k_all, v_all, pos, jnp.arange(P0 + T), lam, slopes)
            o = rmsnorm(o, g_subln[b]) * (1.0 - lam_i)
            t_out = o.reshape(B, T, DIFF_OUT_WIDTH)
        x = x + jnp.concatenate([t_out, m_out], axis=-1) @ w_out[l]
        x = x + sq_relu_mlp(rmsnorm(x, g_ffn[l]), w_ff1[l], w_ff2[l])
    return rmsnorm(x, g_final), jnp.stack(new_pool), new_k, new_v


def setup_inputs(seed: int = 0) -> dict:
    key = jax.random.key(seed)
    ks = iter(jax.random.split(key, 32))
    f32 = jnp.float32
    nrm = lambda shape, s=1.0: jax.random.normal(next(ks), shape, f32) * s
    gain = lambda shape: 1.0 + 0.05 * jax.random.normal(next(ks), shape, f32)
    return {
        "x_prompt": nrm((BATCH, SEQ, D_MODEL)),
        "x_sample": nrm((DEC_BATCH, DEC_SEQ, D_MODEL)),
        "mem_prompt": nrm((BATCH, N_MEM, D_MODEL)),
        "cache_k": nrm((DEC_BATCH, PAST_LEN, N_DIFF_HEADS, DIFF_V_DIM)),
        "cache_v": nrm((DEC_BATCH, PAST_LEN, N_DIFF_HEADS, DIFF_V_DIM)),
        "cache_mem_k": nrm((DEPTH, DEC_BATCH, N_MEM, N_MEM_HEADS, MEM_HEAD_DIM)),
        "cache_mem_v": nrm((DEPTH, DEC_BATCH, N_MEM, N_MEM_HEADS, MEM_HEAD_DIM)),
        "state_pool": nrm((N_A, DEC_BATCH, POOL_HIST, POOL_WIDTH)),
        "g_attn": gain((DEPTH, D_MODEL)),
        "w_in": nrm((DEPTH, D_MODEL, MIX_IN), D_MODEL ** -0.5),
        "w_out": nrm((DEPTH, MIX_OUT, D_MODEL), MIX_OUT ** -0.5),
        "g_mem": gain((DEPTH, D_MODEL)),
        "w_mem_kv": nrm((DEPTH, D_MODEL, 2 * MEM_WIDTH), D_MODEL ** -0.5),
        "g_ffn": gain((DEPTH, D_MODEL)),
        "w_ff1": nrm((DEPTH, D_MODEL, D_FF), D_MODEL ** -0.5),
        "w_ff2": nrm((DEPTH, D_FF, D_MODEL), D_FF ** -0.5),
        "w_pool": nrm((N_A, 4, POOL_GROUP, POOL_GROUP), POOL_GROUP ** -0.5),
        "pool_scale": gain((N_A, POOL_WIDTH)),
        "lambda_qk": nrm((N_B, 4, DIFF_HEAD_DIM), 0.1),
        "g_subln": gain((N_B, DIFF_V_DIM)),
        "g_kv": gain((D_MODEL,)),
        "w_kv": nrm((D_MODEL, 2 * N_DIFF_HEADS * DIFF_V_DIM), D_MODEL ** -0.5),
        "g_final": gain((D_MODEL,)),
    }


def reference(x_prompt, x_sample, mem_prompt, cache_k, cache_v, cache_mem_k, cache_mem_v, state_pool,
              g_attn, w_in, w_out, g_mem, w_mem_kv, g_ffn, w_ff1, w_ff2, w_pool, pool_scale,
              lambda_qk, g_subln, g_kv, w_kv, g_final):
    weights = (g_attn, w_in, w_out, g_ffn, w_ff1, w_ff2, w_pool, pool_scale,
               lambda_qk, g_subln, g_kv, w_kv, g_final)
    B, T, _ = x_prompt.shape
    mks, mvs = [], []
    for l in range(DEPTH):
        mk, mv = mem_kv(mem_prompt, g_mem[l], w_mem_kv[l])
        mks.append(mk)
        mvs.append(mv)
    mem_k_prompt = jnp.stack(mks)
    mem_v_prompt = jnp.stack(mvs)
    hist0 = jnp.zeros((N_A, B, POOL_HIST, POOL_WIDTH), x_prompt.dtype)
    past0 = jnp.zeros((B, 0, N_DIFF_HEADS, DIFF_V_DIM), x_prompt.dtype)
    y_prompt, pool_prompt, k_prompt, v_prompt = trunk(
        x_prompt, jnp.arange(T), hist0, mem_k_prompt, mem_v_prompt, past0, past0, True, *weights)
    P0 = cache_k.shape[1]
    Ts = x_sample.shape[1]
    y_sample, pool_sample, k_sample, v_sample = trunk(
        x_sample, P0 + jnp.arange(Ts), state_pool, cache_mem_k, cache_mem_v, cache_k, cache_v, False, *weights)
    return (y_prompt, y_sample, mem_k_prompt, mem_v_prompt, pool_prompt, k_prompt, v_prompt,
            pool_sample, k_sample, v_sample)
```

```python
import functools
import math

import jax
import jax.numpy as jnp
from jax import lax
from jax.experimental import pallas as pl
from jax.experimental.pallas import tpu as pltpu

F32 = jnp.float32
BF16 = jnp.bfloat16

CHUNK = 64
POOL_WINDOWS = (2, 4, 8, 16)
POOL_HIST = 15
HALO = 16
N_HEADS = 4
HEAD_W = 128
DIFF_D = 64
MIX_W = N_HEADS * HEAD_W
EPS = 1e-6
NEG_BIG = -1e30
VMEM_LIMIT = 56 * 1024 * 1024


def _cparams(*sem):
    return pltpu.CompilerParams(dimension_semantics=sem, vmem_limit_bytes=VMEM_LIMIT)


def _resident(shape, index_map):
    return pl.BlockSpec(shape, index_map, pipeline_mode=pl.Buffered(1))


def _inv_rms(x):
    return lax.rsqrt(jnp.mean(x * x, axis=-1, keepdims=True) + EPS)


def _dot(a, b):
    return jnp.dot(a, b, preferred_element_type=F32)


def _dot_nt(a, b):
    return lax.dot_general(a, b, (((1,), (1,)), ((), ())), preferred_element_type=F32)


def _memkv_kernel(x_ref, g_ref, w_ref, k_ref, v_ref):
    x = x_ref[0]
    h = (x * _inv_rms(x) * g_ref[0]).astype(BF16)
    kv = _dot(h, w_ref[0])
    k_ref[0, 0] = kv[:, :MIX_W]
    v_ref[0, 0] = kv[:, MIX_W:]


def _mem_kv(mem, g_mem, w_mem_kv):
    B, M, D = mem.shape
    L = g_mem.shape[0]
    out = jax.ShapeDtypeStruct((L, B, M, MIX_W), F32)
    return pl.pallas_call(
        _memkv_kernel,
        grid=(L, B),
        in_specs=[pl.BlockSpec((1, M, D), lambda l, b: (b, 0, 0)),
                  pl.BlockSpec((1, 1, D), lambda l, b: (l, 0, 0)),
                  pl.BlockSpec((1, D, 2 * MIX_W), lambda l, b: (l, 0, 0))],
        out_specs=[pl.BlockSpec((1, 1, M, MIX_W), lambda l, b: (l, b, 0, 0)),
                   pl.BlockSpec((1, 1, M, MIX_W), lambda l, b: (l, b, 0, 0))],
        out_shape=[out, out],
        compiler_params=_cparams("arbitrary", "arbitrary"),
        name="mem_kv",
    )(mem, g_mem.reshape(L, 1, D), w_mem_kv)


def _proj0_kernel(x_ref, g_ref, w_ref, mix_ref, qm_ref):
    x = x_ref[...]
    h = (x * _inv_rms(x) * g_ref[...]).astype(BF16)
    z = _dot(h, w_ref[...])
    mix_ref[...] = z[:, :MIX_W]
    qm_ref[...] = z[:, MIX_W:].astype(BF16)


def _proj0(x, g, w, tm):
    R, D = x.shape
    return pl.pallas_call(
        _proj0_kernel,
        grid=(R // tm,),
        in_specs=[pl.BlockSpec((tm, D), lambda i: (i, 0)),
                  _resident((1, D), lambda i: (0, 0)),
                  _resident((D, 2 * MIX_W), lambda i: (0, 0))],
        out_specs=[pl.BlockSpec((tm, MIX_W), lambda i: (i, 0)),
                   pl.BlockSpec((tm, MIX_W), lambda i: (i, 0))],
        out_shape=[jax.ShapeDtypeStruct((R, MIX_W), F32),
                   jax.ShapeDtypeStruct((R, MIX_W), BF16)],
        compiler_params=_cparams("arbitrary"),
        name="proj0",
    )(x, g.reshape(1, D), w)


def _mem_attend(qm, mk_ref, mv_ref, out_ref):
    scale = HEAD_W ** -0.5
    for h in range(N_HEADS):
        hs = slice(h * HEAD_W, (h + 1) * HEAD_W)
        kh = mk_ref[0, :, hs].astype(BF16)
        vh = mv_ref[0, :, hs].astype(BF16)
        s = _dot_nt(qm[:, hs], kh) * scale
        m = jnp.max(s, axis=-1, keepdims=True)
        e = jnp.exp(s - m)
        l = jnp.sum(e, axis=-1, keepdims=True)
        o = _dot(e.astype(BF16), vh) / l
        out_ref[0, :, MIX_W + h * HEAD_W:MIX_W + (h + 1) * HEAD_W] = o.astype(out_ref.dtype)


def _mix0_kernel(mix_ref, halo_ref, hist_ref, qm_ref, mk_ref, mv_ref, wp_ref, ps_ref, out_ref, *, tm, pos0):
    i = pl.program_id(1)
    u = mix_ref[0]
    halo = jnp.where(i == 0, hist_ref[0], halo_ref[0])
    ext = jnp.concatenate([halo, u], axis=0)
    pos = pos0 + i * tm + lax.broadcasted_iota(jnp.int32, (tm, HEAD_W), 0)
    for g, w in enumerate(POOL_WINDOWS):
        gs = slice(g * HEAD_W, (g + 1) * HEAD_W)
        acc = ext[:, gs]
        span = 1
        while span < w:
            acc = acc + pltpu.roll(acc, span, 0)
            span *= 2
        cnt = jnp.minimum(pos + 1, w).astype(F32)
        d = acc[HALO:] / cnt - u[:, gs]
        t = _dot(d.astype(BF16), wp_ref[g]) * ps_ref[:, gs]
        out_ref[0, :, gs] = t.astype(out_ref.dtype)
    _mem_attend(qm_ref[0], mk_ref, mv_ref, out_ref)


def _mix0(mix, hist16, qm, mem_k, mem_v, w_pool, pool_scale, tm, pos0):
    B, T, _ = mix.shape
    hb = tm // HALO
    return pl.pallas_call(
        functools.partial(_mix0_kernel, tm=tm, pos0=pos0),
        grid=(B, T // tm),
        in_specs=[pl.BlockSpec((1, tm, MIX_W), lambda b, i: (b, i, 0)),
                  pl.BlockSpec((1, HALO, MIX_W), lambda b, i: (b, jnp.maximum(i * hb - 1, 0), 0)),
                  pl.BlockSpec((1, HALO, MIX_W), lambda b, i: (b, 0, 0)),
                  pl.BlockSpec((1, tm, MIX_W), lambda b, i: (b, i, 0)),
                  pl.BlockSpec((1,) + mem_k.shape[1:], lambda b, i: (b, 0, 0)),
                  pl.BlockSpec((1,) + mem_v.shape[1:], lambda b, i: (b, 0, 0)),
                  _resident(w_pool.shape, lambda b, i: (0, 0, 0)),
                  _resident((1, MIX_W), lambda b, i: (0, 0))],
        out_specs=pl.BlockSpec((1, tm, 2 * MIX_W), lambda b, i: (b, i, 0)),
        out_shape=jax.ShapeDtypeStruct((B, T, 2 * MIX_W), BF16),
        compiler_params=_cparams("arbitrary", "arbitrary"),
        name="mix0",
    )(mix, mix, hist16, qm, mem_k, mem_v, w_pool, pool_scale.reshape(1, MIX_W))


def _outmlp_kernel(x_ref, cat_ref, wo_ref, g_ref, w1_ref, w2_ref, gf_ref, out_ref, a_ref, *, final_norm, ff_chunk):
    x1 = x_ref[...] + _dot(cat_ref[...], wo_ref[...])
    h = (x1 * _inv_rms(x1) * g_ref[...]).astype(BF16)
    for c in range(w1_ref.shape[1] // ff_chunk):
        cs = slice(c * ff_chunk, (c + 1) * ff_chunk)
        a = jnp.maximum(_dot(h, w1_ref[:, cs]), 0.0)
        a_ref[:, cs] = (a * a).astype(BF16)
    x2 = x1 + _dot(a_ref[...], w2_ref[...])
    if final_norm:
        x2 = x2 * _inv_rms(x2) * gf_ref[...]
    out_ref[...] = x2


def _outmlp(x, cat, w_out, g_ffn, w1, w2, g_final, tm, final_norm):
    R, D = x.shape
    FF = w1.shape[1]
    return pl.pallas_call(
        functools.partial(_outmlp_kernel, final_norm=final_norm, ff_chunk=1024),
        grid=(R // tm,),
        in_specs=[pl.BlockSpec((tm, D), lambda i: (i, 0)),
                  pl.BlockSpec((tm, D), lambda i: (i, 0)),
                  _resident((D, D), lambda i: (0, 0)),
                  _resident((1, D), lambda i: (0, 0)),
                  _resident((D, FF), lambda i: (0, 0)),
                  _resident((FF, D), lambda i: (0, 0)),
                  _resident((1, D), lambda i: (0, 0))],
        out_specs=pl.BlockSpec((tm, D), lambda i: (i, 0)),
        out_shape=jax.ShapeDtypeStruct((R, D), F32),
        scratch_shapes=[pltpu.VMEM((tm, FF), BF16)],
        compiler_params=_cparams("arbitrary"),
        name="outmlp",
    )(x, cat, w_out, g_ffn.reshape(1, D), w1, w2, g_final.reshape(1, D))


def _proj1_kernel(x_ref, ga_ref, wi_ref, gk_ref, wk_ref, z_ref, k_ref, v_ref, kb_ref, vb_ref):
    x = x_ref[...]
    xn = x * _inv_rms(x)
    z_ref[...] = _dot((xn * ga_ref[...]).astype(BF16), wi_ref[...]).astype(BF16)
    kv = _dot((xn * gk_ref[...]).astype(BF16), wk_ref[...])
    k = kv[:, :MIX_W]
    v = kv[:, MIX_W:]
    k_ref[...] = k
    v_ref[...] = v
    kb_ref[...] = k.astype(BF16)
    vb_ref[...] = v.astype(BF16)


def _proj1(x, g_attn, w_in, g_kv, w_kv, tm):
    R, D = x.shape
    row = lambda i: (i, 0)
    fix = lambda i: (0, 0)
    return pl.pallas_call(
        _proj1_kernel,
        grid=(R // tm,),
        in_specs=[pl.BlockSpec((tm, D), row),
                  _resident((1, D), fix), _resident((D, 2 * MIX_W), fix),
                  _resident((1, D), fix), _resident((D, 2 * MIX_W), fix)],
        out_specs=[pl.BlockSpec((tm, 2 * MIX_W), row),
                   pl.BlockSpec((tm, MIX_W), row), pl.BlockSpec((tm, MIX_W), row),
                   pl.BlockSpec((tm, MIX_W), row), pl.BlockSpec((tm, MIX_W), row)],
        out_shape=[jax.ShapeDtypeStruct((R, 2 * MIX_W), BF16),
                   jax.ShapeDtypeStruct((R, MIX_W), F32), jax.ShapeDtypeStruct((R, MIX_W), F32),
                   jax.ShapeDtypeStruct((R, MIX_W), BF16), jax.ShapeDtypeStruct((R, MIX_W), BF16)],
        compiler_params=_cparams("arbitrary"),
        name="proj1",
    )(x, g_attn.reshape(1, D), w_in, g_kv.reshape(1, D), w_kv)


def _lambda(lq, lam_init):
    a = jnp.sum(lq[0:1] * lq[1:2], axis=-1, keepdims=True)
    b = jnp.sum(lq[2:3] * lq[3:4], axis=-1, keepdims=True)
    return jnp.exp(a) - jnp.exp(b) + lam_init


def _slope(h):
    return 2.0 ** (-8.0 * (h + 1) / N_HEADS)


def _split_components(qh):
    lane = lax.broadcasted_iota(jnp.int32, qh.shape, 1)
    qs = qh * jnp.asarray(DIFF_D ** -0.5, qh.dtype)
    zero = jnp.zeros_like(qs)
    return jnp.where(lane < DIFF_D, qs, zero), jnp.where(lane >= DIFF_D, qs, zero)


def _finish_head(o1, o2, lam, gs, lam_init):
    o = o1 - lam * o2
    return o * _inv_rms(o) * gs * (1.0 - lam_init)


def _diff_kernel(lq_ref, gs_ref, q_ref, qm_ref, k_ref, v_ref, mk_ref, mv_ref, out_ref, acc_ref, m_ref,
                 *, tq, lam_init):
    i = pl.program_id(1)
    lam = _lambda(lq_ref[...], lam_init)
    row = lax.broadcasted_iota(jnp.int32, (tq, tq), 0)
    col = lax.broadcasted_iota(jnp.int32, (tq, tq), 1)
    rel = (row - col).astype(F32)
    vis = (col // CHUNK) <= (row // CHUNK)
    ones = jnp.ones((tq, HEAD_W), BF16)
    q0 = pl.multiple_of(i * tq, tq)
    for h in range(N_HEADS):
        hs = slice(h * HEAD_W, (h + 1) * HEAD_W)
        slope = _slope(h)
        qc = _split_components(q_ref[0, :, hs])
        kd = k_ref[0, pl.ds(q0, tq), hs]
        vd = jnp.concatenate([v_ref[0, pl.ds(q0, tq), hs], ones], axis=1)
        bias_d = jnp.where(vis, -slope * jnp.abs(rel), NEG_BIG)
        for c in range(2):
            s = _dot_nt(qc[c], kd) + bias_d
            m = jnp.max(s, axis=-1, keepdims=True)
            p = jnp.exp(s - m)
            acc_ref[c] = _dot(p.astype(BF16), vd)
            m_ref[c] = m
        bias_o = -slope * rel

        def body(j, carry):
            k0 = pl.multiple_of(j * tq, tq)
            kt = k_ref[0, pl.ds(k0, tq), hs]
            vt = jnp.concatenate([v_ref[0, pl.ds(k0, tq), hs], ones], axis=1)
            cj = -slope * ((i - j) * tq).astype(F32)
            for c in range(2):
                s = _dot_nt(qc[c], kt) + bias_o
                m_old = m_ref[c]
                m_new = jnp.maximum(m_old, jnp.max(s, axis=-1, keepdims=True) + cj)
                p = jnp.exp(s - (m_new - cj))
                alpha = jnp.exp(m_old - m_new)
                acc_ref[c] = alpha * acc_ref[c] + _dot(p.astype(BF16), vt)
                m_ref[c] = m_new
            return carry

        lax.fori_loop(0, i, body, 0)
        o1 = acc_ref[0, :, :HEAD_W] / acc_ref[0, :, HEAD_W:]
        o2 = acc_ref[1, :, :HEAD_W] / acc_ref[1, :, HEAD_W:]
        out_ref[0, :, hs] = _finish_head(o1, o2, lam, gs_ref[...], lam_init).astype(out_ref.dtype)
    _mem_attend(qm_ref[0], mk_ref, mv_ref, out_ref)


def _diff_prompt(z, kb, vb, mem_k, mem_v, lq, g_sub, lam_init, tq):
    B, T, _ = z.shape
    return pl.pallas_call(
        functools.partial(_diff_kernel, tq=tq, lam_init=lam_init),
        grid=(B, T // tq),
        in_specs=[_resident(lq.shape, lambda b, i: (0, 0)),
                  _resident((1, HEAD_W), lambda b, i: (0, 0)),
                  pl.BlockSpec((1, tq, MIX_W), lambda b, i: (b, i, 0)),
                  pl.BlockSpec((1, tq, MIX_W), lambda b, i: (b, i, 1)),
                  _resident((1, T, MIX_W), lambda b, i: (b, 0, 0)),
                  _resident((1, T, MIX_W), lambda b, i: (b, 0, 0)),
                  pl.BlockSpec((1,) + mem_k.shape[1:], lambda b, i: (b, 0, 0)),
                  pl.BlockSpec((1,) + mem_v.shape[1:], lambda b, i: (b, 0, 0))],
        out_specs=pl.BlockSpec((1, tq, 2 * MIX_W), lambda b, i: (b, i, 0)),
        out_shape=jax.ShapeDtypeStruct((B, T, 2 * MIX_W), BF16),
        scratch_shapes=[pltpu.VMEM((2, tq, 2 * HEAD_W), F32),
                        pltpu.VMEM((2, tq, 1), F32)],
        compiler_params=_cparams("arbitrary", "arbitrary"),
        name="diff_prompt",
    )(lq, g_sub.reshape(1, HEAD_W), z, z, kb, vb, mem_k, mem_v)


def _diff_sample_kernel(lq_ref, gs_ref, q_ref, qm_ref, ck_ref, cv_ref, nk_ref, nv_ref, mk_ref, mv_ref, out_ref,
                        *, p0, lam_init):
    ts = q_ref.shape[1]
    lam = _lambda(lq_ref[...], lam_init)
    qpos_p = p0 + lax.broadcasted_iota(jnp.int32, (ts, p0), 0)
    dist_p = (qpos_p - lax.broadcasted_iota(jnp.int32, (ts, p0), 1)).astype(F32)
    row = p0 + lax.broadcasted_iota(jnp.int32, (ts, ts), 0)
    col = p0 + lax.broadcasted_iota(jnp.int32, (ts, ts), 1)
    dist_n = jnp.abs(row - col).astype(F32)
    vis_n = (col // CHUNK) <= (row // CHUNK)
    for h in range(N_HEADS):
        hs = slice(h * HEAD_W, (h + 1) * HEAD_W)
        slope = _slope(h)
        qc = _split_components(q_ref[0, :, hs])
        kp = ck_ref[0, :, hs].astype(BF16)
        vp = cv_ref[0, :, hs].astype(BF16)
        kn = nk_ref[0, :, hs].astype(BF16)
        vn = nv_ref[0, :, hs].astype(BF16)
        bias_p = -slope * dist_p
        bias_n = jnp.where(vis_n, -slope * dist_n, NEG_BIG)
        o = []
        for c in range(2):
            sp = _dot_nt(qc[c], kp) + bias_p
            sn = _dot_nt(qc[c], kn) + bias_n
            m = jnp.maximum(jnp.max(sp, axis=-1, keepdims=True), jnp.max(sn, axis=-1, keepdims=True))
            ep = jnp.exp(sp - m)
            en = jnp.exp(sn - m)
            l = jnp.sum(ep, axis=-1, keepdims=True) + jnp.sum(en, axis=-1, keepdims=True)
            o.append((_dot(ep.astype(BF16), vp) + _dot(en.astype(BF16), vn)) / l)
        out_ref[0, :, hs] = _finish_head(o[0], o[1], lam, gs_ref[...], lam_init).astype(out_ref.dtype)
    _mem_attend(qm_ref[0], mk_ref, mv_ref, out_ref)


def _diff_sample(z, cache_k, cache_v, k_new, v_new, mem_k, mem_v, lq, g_sub, lam_init):
    B, Ts, _ = z.shape
    P0 = cache_k.shape[1]
    per_b = lambda b: (b, 0, 0)
    return pl.pallas_call(
        functools.partial(_diff_sample_kernel, p0=P0, lam_init=lam_init),
        grid=(B,),
        in_specs=[_resident(lq.shape, lambda b: (0, 0)),
                  _resident((1, HEAD_W), lambda b: (0, 0)),
                  pl.BlockSpec((1, Ts, MIX_W), lambda b: (b, 0, 0)),
                  pl.BlockSpec((1, Ts, MIX_W), lambda b: (b, 0, 1)),
                  pl.BlockSpec((1, P0, MIX_W), per_b),
                  pl.BlockSpec((1, P0, MIX_W), per_b),
                  pl.BlockSpec((1, Ts, MIX_W), per_b),
                  pl.BlockSpec((1, Ts, MIX_W), per_b),
                  pl.BlockSpec((1,) + mem_k.shape[1:], per_b),
                  pl.BlockSpec((1,) + mem_v.shape[1:], per_b)],
        out_specs=pl.BlockSpec((1, Ts, 2 * MIX_W), per_b),
        out_shape=jax.ShapeDtypeStruct((B, Ts, 2 * MIX_W), BF16),
        compiler_params=_cparams("arbitrary"),
        name="diff_sample",
    )(lq, g_sub.reshape(1, HEAD_W), z, z, cache_k, cache_v, k_new, v_new, mem_k, mem_v)


def _lambda_init(layer_idx):
    return 0.8 - 0.6 * math.exp(-0.3 * layer_idx)


def _trunk(x, pos0, pool_hist, mem_k, mem_v, past_k, past_v, w, tm_rows, tm_seq, tq):
    B, T, D = x.shape
    R = B * T
    xr = x.reshape(R, D)
    hist16 = jnp.pad(pool_hist, ((0, 0), (HALO - POOL_HIST, 0), (0, 0)))

    mix, qm = _proj0(xr, w["g_attn"][0], w["w_in"][0], tm_rows)
    mix3 = mix.reshape(B, T, MIX_W)
    cat0 = _mix0(mix3, hist16, qm.reshape(B, T, MIX_W), mem_k[0], mem_v[0],
                 w["w_pool"], w["pool_scale"], tm_seq, pos0)
    new_pool = jnp.concatenate([pool_hist, mix3], axis=1)[:, -POOL_HIST:]
    x1 = _outmlp(xr, cat0.reshape(R, D), w["w_out"][0], w["g_ffn"][0], w["w_ff1"][0], w["w_ff2"][0],
                 w["g_final"], tm_rows, final_norm=False)

    z, k, v, kb, vb = _proj1(x1, w["g_attn"][1], w["w_in"][1], w["g_kv"], w["w_kv"], tm_rows)
    z3 = z.reshape(B, T, D)
    lam_init = _lambda_init(1)
    if past_k is None:
        cat1 = _diff_prompt(z3, kb.reshape(B, T, MIX_W), vb.reshape(B, T, MIX_W), mem_k[1], mem_v[1],
                            w["lambda_qk"], w["g_subln"], lam_init, tq)
    else:
        cat1 = _diff_sample(z3, past_k, past_v, k.reshape(B, T, MIX_W), v.reshape(B, T, MIX_W),
                            mem_k[1], mem_v[1], w["lambda_qk"], w["g_subln"], lam_init)
    y = _outmlp(x1, cat1.reshape(R, D), w["w_out"][1], w["g_ffn"][1], w["w_ff1"][1], w["w_ff2"][1],
                w["g_final"], tm_rows, final_norm=True)
    shp = (B, T, N_HEADS, HEAD_W)
    return y.reshape(B, T, D), new_pool[None], k.reshape(shp), v.reshape(shp)


def kernel(x_prompt, x_sample, mem_prompt, cache_k, cache_v, cache_mem_k, cache_mem_v, state_pool, g_attn, w_in, w_out, g_mem, w_mem_kv, g_ffn, w_ff1, w_ff2, w_pool, pool_scale, lambda_qk, g_subln, g_kv, w_kv, g_final):
    B, T, D = x_prompt.shape
    Bs, Ts, _ = x_sample.shape
    P0 = cache_k.shape[1]
    L = g_attn.shape[0]
    n_mem = mem_prompt.shape[1]
    w = dict(g_attn=g_attn, w_in=w_in.astype(BF16), w_out=w_out.astype(BF16), g_ffn=g_ffn,
             w_ff1=w_ff1.astype(BF16), w_ff2=w_ff2.astype(BF16), w_pool=w_pool[0].astype(BF16),
             pool_scale=pool_scale[0], lambda_qk=lambda_qk[0], g_subln=g_subln[0], g_kv=g_kv,
             w_kv=w_kv.astype(BF16), g_final=g_final)

    mem_k_p, mem_v_p = _mem_kv(mem_prompt, g_mem, w_mem_kv.astype(BF16))
    hist0 = jnp.zeros((B, POOL_HIST, MIX_W), F32)
    y_p, pool_p, k_p, v_p = _trunk(x_prompt, 0, hist0, mem_k_p, mem_v_p, None, None, w,
                                   tm_rows=512, tm_seq=512, tq=512)
    y_s, pool_s, k_s, v_s = _trunk(x_sample, P0, state_pool[0],
                                   cache_mem_k.reshape(L, Bs, n_mem, MIX_W),
                                   cache_mem_v.reshape(L, Bs, n_mem, MIX_W),
                                   cache_k.reshape(Bs, P0, MIX_W), cache_v.reshape(Bs, P0, MIX_W), w,
                                   tm_rows=Bs * Ts, tm_seq=Ts, tq=None)
    mshape = (L, B, n_mem, N_HEADS, HEAD_W)
    return (y_p, y_s, mem_k_p.reshape(mshape), mem_v_p.reshape(mshape), pool_p, k_p, v_p, pool_s, k_s, v_s)
```

```python
import functools
import math

import jax
import jax.numpy as jnp
import ml_dtypes
import numpy as np
from jax import lax
from jax.experimental import pallas as pl
from jax.experimental.pallas import tpu as pltpu

F32 = jnp.float32
BF16 = jnp.bfloat16

CHUNK = 64
POOL_WINDOWS = (2, 4, 8, 16)
POOL_HIST = 15
HALO = 16
N_HEADS = 4
HEAD_W = 128
DIFF_D = 64
MIX_W = N_HEADS * HEAD_W
EPS = 1e-6
NEG_BIG = -1e30
LOG2E = math.log2(math.e)
Q_SCALE = DIFF_D ** -0.5 * LOG2E
N_SPLIT = 4
VMEM_LIMIT = 56 * 1024 * 1024


def _cparams(*sem):
    return pltpu.CompilerParams(dimension_semantics=sem, vmem_limit_bytes=VMEM_LIMIT)


def _resident(shape, index_map):
    return pl.BlockSpec(shape, index_map, pipeline_mode=pl.Buffered(1))


def _inv_rms(x):
    return lax.rsqrt(jnp.mean(x * x, axis=-1, keepdims=True) + EPS)


def _dot(a, b):
    return jnp.dot(a, b, preferred_element_type=F32)


def _dot_nt(a, b):
    return lax.dot_general(a, b, (((1,), (1,)), ((), ())), preferred_element_type=F32)


def _memkv_kernel(x_ref, g_ref, w_ref, k_ref, v_ref):
    x = x_ref[0]
    h = (x * _inv_rms(x) * g_ref[0]).astype(BF16)
    kv = _dot(h, w_ref[0])
    k_ref[0, 0] = kv[:, :MIX_W]
    v_ref[0, 0] = kv[:, MIX_W:]


def _mem_kv(mem, g_mem, w_mem_kv):
    B, M, D = mem.shape
    L = g_mem.shape[0]
    out = jax.ShapeDtypeStruct((L, B, M, MIX_W), F32)
    return pl.pallas_call(
        _memkv_kernel,
        grid=(L, B),
        in_specs=[pl.BlockSpec((1, M, D), lambda l, b: (b, 0, 0)),
                  pl.BlockSpec((1, 1, D), lambda l, b: (l, 0, 0)),
                  pl.BlockSpec((1, D, 2 * MIX_W), lambda l, b: (l, 0, 0))],
        out_specs=[pl.BlockSpec((1, 1, M, MIX_W), lambda l, b: (l, b, 0, 0)),
                   pl.BlockSpec((1, 1, M, MIX_W), lambda l, b: (l, b, 0, 0))],
        out_shape=[out, out],
        compiler_params=_cparams("arbitrary", "arbitrary"),
        name="mem_kv",
    )(mem, g_mem.reshape(L, 1, D), w_mem_kv)


def _proj0_kernel(x_ref, g_ref, w_ref, mix_ref, qm_ref):
    x = x_ref[...]
    h = (x * _inv_rms(x) * g_ref[...]).astype(BF16)
    z = _dot(h, w_ref[...])
    mix_ref[...] = z[:, :MIX_W]
    qm_ref[...] = z[:, MIX_W:].astype(BF16)


def _proj0(x, g, w, tm):
    R, D = x.shape
    return pl.pallas_call(
        _proj0_kernel,
        grid=(R // tm,),
        in_specs=[pl.BlockSpec((tm, D), lambda i: (i, 0)),
                  _resident((1, D), lambda i: (0, 0)),
                  _resident((D, 2 * MIX_W), lambda i: (0, 0))],
        out_specs=[pl.BlockSpec((tm, MIX_W), lambda i: (i, 0)),
                   pl.BlockSpec((tm, MIX_W), lambda i: (i, 0))],
        out_shape=[jax.ShapeDtypeStruct((R, MIX_W), F32),
                   jax.ShapeDtypeStruct((R, MIX_W), BF16)],
        compiler_params=_cparams("arbitrary"),
        name="proj0",
    )(x, g.reshape(1, D), w)


def _mem_attend(qm, mk_ref, mv_ref, out_ref):
    scale = HEAD_W ** -0.5
    for h in range(N_HEADS):
        hs = slice(h * HEAD_W, (h + 1) * HEAD_W)
        kh = mk_ref[0, :, hs].astype(BF16)
        vh = mv_ref[0, :, hs].astype(BF16)
        s = _dot_nt(qm[:, hs], kh) * scale
        m = jnp.max(s, axis=-1, keepdims=True)
        e = jnp.exp(s - m)
        l = jnp.sum(e, axis=-1, keepdims=True)
        o = _dot(e.astype(BF16), vh) / l
        out_ref[0, :, MIX_W + h * HEAD_W:MIX_W + (h + 1) * HEAD_W] = o.astype(out_ref.dtype)


def _mix0_kernel(mix_ref, halo_ref, hist_ref, qm_ref, mk_ref, mv_ref, wp_ref, ps_ref, out_ref, *, tm, pos0):
    i = pl.program_id(1)
    u = mix_ref[0]
    halo = jnp.where(i == 0, hist_ref[0], halo_ref[0])
    ext = jnp.concatenate([halo, u], axis=0)
    pos = pos0 + i * tm + lax.broadcasted_iota(jnp.int32, (tm, HEAD_W), 0)
    for g, w in enumerate(POOL_WINDOWS):
        gs = slice(g * HEAD_W, (g + 1) * HEAD_W)
        acc = ext[:, gs]
        span = 1
        while span < w:
            acc = acc + pltpu.roll(acc, span, 0)
            span *= 2
        cnt = jnp.minimum(pos + 1, w).astype(F32)
        d = acc[HALO:] / cnt - u[:, gs]
        t = _dot(d.astype(BF16), wp_ref[g]) * ps_ref[:, gs]
        out_ref[0, :, gs] = t.astype(out_ref.dtype)
    _mem_attend(qm_ref[0], mk_ref, mv_ref, out_ref)


def _mix0(mix, hist16, qm, mem_k, mem_v, w_pool, pool_scale, tm, pos0):
    B, T, _ = mix.shape
    hb = tm // HALO
    return pl.pallas_call(
        functools.partial(_mix0_kernel, tm=tm, pos0=pos0),
        grid=(B, T // tm),
        in_specs=[pl.BlockSpec((1, tm, MIX_W), lambda b, i: (b, i, 0)),
                  pl.BlockSpec((1, HALO, MIX_W), lambda b, i: (b, jnp.maximum(i * hb - 1, 0), 0)),
                  pl.BlockSpec((1, HALO, MIX_W), lambda b, i: (b, 0, 0)),
                  pl.BlockSpec((1, tm, MIX_W), lambda b, i: (b, i, 0)),
                  pl.BlockSpec((1,) + mem_k.shape[1:], lambda b, i: (b, 0, 0)),
                  pl.BlockSpec((1,) + mem_v.shape[1:], lambda b, i: (b, 0, 0)),
                  _resident(w_pool.shape, lambda b, i: (0, 0, 0)),
                  _resident((1, MIX_W), lambda b, i: (0, 0))],
        out_specs=pl.BlockSpec((1, tm, 2 * MIX_W), lambda b, i: (b, i, 0)),
        out_shape=jax.ShapeDtypeStruct((B, T, 2 * MIX_W), BF16),
        compiler_params=_cparams("arbitrary", "arbitrary"),
        name="mix0",
    )(mix, mix, hist16, qm, mem_k, mem_v, w_pool, pool_scale.reshape(1, MIX_W))


def _outmlp_kernel(x_ref, cat_ref, wo_ref, g_ref, w1_ref, w2_ref, gf_ref, out_ref, a_ref, *, final_norm, ff_chunk):
    x1 = x_ref[...] + _dot(cat_ref[...], wo_ref[...])
    h = (x1 * _inv_rms(x1) * g_ref[...]).astype(BF16)
    for c in range(w1_ref.shape[1] // ff_chunk):
        cs = slice(c * ff_chunk, (c + 1) * ff_chunk)
        a = jnp.maximum(_dot(h, w1_ref[:, cs]), 0.0)
        a_ref[:, cs] = (a * a).astype(BF16)
    x2 = x1 + _dot(a_ref[...], w2_ref[...])
    if final_norm:
        x2 = x2 * _inv_rms(x2) * gf_ref[...]
    out_ref[...] = x2


def _outmlp(x, cat, w_out, g_ffn, w1, w2, g_final, tm, final_norm):
    R, D = x.shape
    FF = w1.shape[1]
    return pl.pallas_call(
        functools.partial(_outmlp_kernel, final_norm=final_norm, ff_chunk=1024),
        grid=(R // tm,),
        in_specs=[pl.BlockSpec((tm, D), lambda i: (i, 0)),
                  pl.BlockSpec((tm, D), lambda i: (i, 0)),
                  _resident((D, D), lambda i: (0, 0)),
                  _resident((1, D), lambda i: (0, 0)),
                  _resident((D, FF), lambda i: (0, 0)),
                  _resident((FF, D), lambda i: (0, 0)),
                  _resident((1, D), lambda i: (0, 0))],
        out_specs=pl.BlockSpec((tm, D), lambda i: (i, 0)),
        out_shape=jax.ShapeDtypeStruct((R, D), F32),
        scratch_shapes=[pltpu.VMEM((tm, FF), BF16)],
        compiler_params=_cparams("arbitrary"),
        name="outmlp",
    )(x, cat, w_out, g_ffn.reshape(1, D), w1, w2, g_final.reshape(1, D))


def _lane_window(lane, lo):
    return (lane >= lo) & (lane < lo + N_SPLIT)


def _proj1_kernel(x_ref, ga_ref, wi_ref, gk_ref, wk_ref, z_ref, k_ref, v_ref, *bf_refs, seq_len):
    tm = x_ref.shape[0]
    x = x_ref[...]
    xn = x * _inv_rms(x)
    z = _dot((xn * ga_ref[...]).astype(BF16), wi_ref[...])
    z_ref[:, :MIX_W] = (z[:, :MIX_W] * Q_SCALE).astype(BF16)
    z_ref[:, MIX_W:] = z[:, MIX_W:].astype(BF16)
    kv = _dot((xn * gk_ref[...]).astype(BF16), wk_ref[...])
    k = kv[:, :MIX_W]
    v = kv[:, MIX_W:]
    k_ref[...] = k
    v_ref[...] = v
    if bf_refs:
        k1a_ref, k2a_ref, vb_ref = bf_refs
        lane = lax.broadcasted_iota(jnp.int32, (tm, HEAD_W), 1)
        pos = (pl.program_id(0) * tm) % seq_len + lax.broadcasted_iota(jnp.int32, (tm, HEAD_W), 0)
        off = pos % CHUNK
        base = (pos - off).astype(F32)
        off = off.astype(F32)

        def pos_lanes(lo):
            return jnp.where(_lane_window(lane, lo), base, jnp.where(_lane_window(lane, lo + N_SPLIT), off, 0.0))

        pos_hi = pos_lanes(DIFF_D)
        pos_lo = pos_lanes(0)
        for h in range(N_HEADS):
            hs = slice(h * HEAD_W, (h + 1) * HEAD_W)
            k1a_ref[:, hs] = jnp.where(lane < DIFF_D, k[:, hs], pos_hi).astype(BF16)
            k2a_ref[:, hs] = jnp.where(lane >= DIFF_D, k[:, hs], pos_lo).astype(BF16)
        vb_ref[...] = v.astype(BF16)


def _proj1(x, g_attn, w_in, g_kv, w_kv, tm, seq_len, attn_operands):
    R, D = x.shape
    row = lambda i: (i, 0)
    fix = lambda i: (0, 0)
    n_bf = 3 if attn_operands else 0
    return pl.pallas_call(
        functools.partial(_proj1_kernel, seq_len=seq_len),
        grid=(R // tm,),
        in_specs=[pl.BlockSpec((tm, D), row),
                  _resident((1, D), fix), _resident((D, 2 * MIX_W), fix),
                  _resident((1, D), fix), _resident((D, 2 * MIX_W), fix)],
        out_specs=[pl.BlockSpec((tm, 2 * MIX_W), row),
                   pl.BlockSpec((tm, MIX_W), row), pl.BlockSpec((tm, MIX_W), row)]
                  + [pl.BlockSpec((tm, MIX_W), row)] * n_bf,
        out_shape=[jax.ShapeDtypeStruct((R, 2 * MIX_W), BF16),
                   jax.ShapeDtypeStruct((R, MIX_W), F32), jax.ShapeDtypeStruct((R, MIX_W), F32)]
                  + [jax.ShapeDtypeStruct((R, MIX_W), BF16)] * n_bf,
        compiler_params=_cparams("arbitrary"),
        name="proj1",
    )(x, g_attn.reshape(1, D), w_in, g_kv.reshape(1, D), w_kv)


def _lambda(lq, lam_init):
    a = jnp.sum(lq[0:1] * lq[1:2], axis=-1, keepdims=True)
    b = jnp.sum(lq[2:3] * lq[3:4], axis=-1, keepdims=True)
    return jnp.exp(a) - jnp.exp(b) + lam_init


def _slope2(h):
    return 2.0 ** (-8.0 * (h + 1) / N_HEADS) * LOG2E


def _bf16_pieces(c):
    out = []
    for _ in range(N_SPLIT):
        p = float(np.asarray(c, dtype=ml_dtypes.bfloat16))
        out.append(p)
        c -= p
    return out


def _slope_lanes(h, lo, shape):
    lane = lax.broadcasted_iota(jnp.int32, shape, 1)
    out = jnp.zeros(shape, F32)
    for p, piece in enumerate(_bf16_pieces(_slope2(h))):
        out = jnp.where((lane == lo + p) | (lane == lo + N_SPLIT + p), piece, out)
    return out.astype(BF16)


def _finish_head(o1, o2, lam, gs, lam_init):
    o = o1 - lam * o2
    return o * _inv_rms(o) * gs * (1.0 - lam_init)


def _diff_kernel(lq_ref, gs_ref, q_ref, qm_ref, k1_ref, k2_ref, v_ref, mk_ref, mv_ref, out_ref,
                 qa_ref, acc_ref, m_ref, *, tq, lam_init):
    i = pl.program_id(1)
    nblk = tq // HEAD_W
    k_refs = (k1_ref, k2_ref)
    row = lax.broadcasted_iota(jnp.int32, (tq, tq), 0)
    col = lax.broadcasted_iota(jnp.int32, (tq, tq), 1)
    ahead = jnp.maximum(col - row, 0).astype(F32)
    vis = (col // CHUNK) <= (row // CHUNK)
    ones = jnp.ones((tq, HEAD_W), BF16)
    lane = lax.broadcasted_iota(jnp.int32, (tq, HEAD_W), 1)
    q0 = pl.multiple_of(i * tq, tq)

    def hslice(n):
        return slice((n // 2) * HEAD_W, (n // 2 + 1) * HEAD_W)

    def scores(n, k0):
        return _dot_nt(qa_ref[n], k_refs[n % 2][0, pl.ds(k0, tq), hslice(n)])

    def v_aug(n, k0):
        return jnp.concatenate([v_ref[0, pl.ds(k0, tq), hslice(n)], ones], axis=1)

    def update(n, t, vt, first):
        blocks = [t[:, b * HEAD_W:(b + 1) * HEAD_W] for b in range(nblk)]
        m_row = jnp.max(functools.reduce(jnp.maximum, blocks), axis=-1, keepdims=True)
        if first:
            m_new = jnp.broadcast_to(m_row, (tq, HEAD_W))
        else:
            m_old = m_ref[n]
            m_new = jnp.maximum(m_old, m_row)
        p = jnp.concatenate([jnp.exp2(b - m_new) for b in blocks], axis=1).astype(BF16)
        pv = _dot(p, vt)
        if first:
            acc_ref[n] = pv
        else:
            alpha = jnp.exp2(m_old - m_new)
            acc_ref[n] = jnp.concatenate([alpha, alpha], axis=1) * acc_ref[n] + pv
        m_ref[n] = m_new

    for h in range(N_HEADS):
        qh = q_ref[0, :, h * HEAD_W:(h + 1) * HEAD_W]
        qa_ref[2 * h] = jnp.where(lane < DIFF_D, qh, _slope_lanes(h, DIFF_D, qh.shape))
        qa_ref[2 * h + 1] = jnp.where(lane >= DIFF_D, qh, _slope_lanes(h, 0, qh.shape))
        fix = jnp.where(vis, (-2.0 * _slope2(h)) * ahead, NEG_BIG)
        for n in (2 * h, 2 * h + 1):
            update(n, scores(n, q0) + fix, v_aug(n, q0), True)

    def body(j, carry):
        k0 = pl.multiple_of(j * tq, tq)
        for n in range(2 * N_HEADS):
            update(n, scores(n, k0), v_aug(n, k0), False)
        return carry

    lax.fori_loop(0, i, body, 0)

    lam = _lambda(lq_ref[...], lam_init)
    for h in range(N_HEADS):
        o = [acc_ref[n, :, :HEAD_W] / acc_ref[n, :, HEAD_W:] for n in (2 * h, 2 * h + 1)]
        out_ref[0, :, h * HEAD_W:(h + 1) * HEAD_W] = _finish_head(
            o[0], o[1], lam, gs_ref[...], lam_init).astype(out_ref.dtype)
    _mem_attend(qm_ref[0], mk_ref, mv_ref, out_ref)


def _diff_prompt(z, k1a, k2a, vb, mem_k, mem_v, lq, g_sub, lam_init, tq):
    B, T, _ = z.shape
    n_chain = 2 * N_HEADS
    return pl.pallas_call(
        functools.partial(_diff_kernel, tq=tq, lam_init=lam_init),
        grid=(B, T // tq),
        in_specs=[_resident(lq.shape, lambda b, i: (0, 0)),
                  _resident((1, HEAD_W), lambda b, i: (0, 0)),
                  pl.BlockSpec((1, tq, MIX_W), lambda b, i: (b, i, 0)),
                  pl.BlockSpec((1, tq, MIX_W), lambda b, i: (b, i, 1)),
                  _resident((1, T, MIX_W), lambda b, i: (b, 0, 0)),
                  _resident((1, T, MIX_W), lambda b, i: (b, 0, 0)),
                  _resident((1, T, MIX_W), lambda b, i: (b, 0, 0)),
                  pl.BlockSpec((1,) + mem_k.shape[1:], lambda b, i: (b, 0, 0)),
                  pl.BlockSpec((1,) + mem_v.shape[1:], lambda b, i: (b, 0, 0))],
        out_specs=pl.BlockSpec((1, tq, 2 * MIX_W), lambda b, i: (b, i, 0)),
        out_shape=jax.ShapeDtypeStruct((B, T, 2 * MIX_W), BF16),
        scratch_shapes=[pltpu.VMEM((n_chain, tq, HEAD_W), BF16),
                        pltpu.VMEM((n_chain, tq, 2 * HEAD_W), F32),
                        pltpu.VMEM((n_chain, tq, HEAD_W), F32)],
        compiler_params=_cparams("arbitrary", "arbitrary"),
        name="diff_prompt",
    )(lq, g_sub.reshape(1, HEAD_W), z, z, k1a, k2a, vb, mem_k, mem_v)


def _diff_sample_kernel(lq_ref, gs_ref, q_ref, qm_ref, ck_ref, cv_ref, nk_ref, nv_ref, mk_ref, mv_ref, out_ref,
                        *, p0, lam_init):
    ts = q_ref.shape[1]
    lam = _lambda(lq_ref[...], lam_init)
    qpos_p = p0 + lax.broadcasted_iota(jnp.int32, (ts, p0), 0)
    dist_p = (qpos_p - lax.broadcasted_iota(jnp.int32, (ts, p0), 1)).astype(F32)
    row = p0 + lax.broadcasted_iota(jnp.int32, (ts, ts), 0)
    col = p0 + lax.broadcasted_iota(jnp.int32, (ts, ts), 1)
    dist_n = jnp.abs(row - col).astype(F32)
    vis_n = (col // CHUNK) <= (row // CHUNK)
    lane = lax.broadcasted_iota(jnp.int32, (ts, HEAD_W), 1)
    for h in range(N_HEADS):
        hs = slice(h * HEAD_W, (h + 1) * HEAD_W)
        qh = q_ref[0, :, hs]
        zero = jnp.zeros_like(qh)
        qc = (jnp.where(lane < DIFF_D, qh, zero), jnp.where(lane >= DIFF_D, qh, zero))
        kp = ck_ref[0, :, hs].astype(BF16)
        vp = cv_ref[0, :, hs].astype(BF16)
        kn = nk_ref[0, :, hs].astype(BF16)
        vn = nv_ref[0, :, hs].astype(BF16)
        bias_p = -_slope2(h) * dist_p
        bias_n = jnp.where(vis_n, -_slope2(h) * dist_n, NEG_BIG)
        o = []
        for c in range(2):
            sp = _dot_nt(qc[c], kp) + bias_p
            sn = _dot_nt(qc[c], kn) + bias_n
            m = jnp.maximum(jnp.max(sp, axis=-1, keepdims=True), jnp.max(sn, axis=-1, keepdims=True))
            ep = jnp.exp2(sp - m)
            en = jnp.exp2(sn - m)
            l = jnp.sum(ep, axis=-1, keepdims=True) + jnp.sum(en, axis=-1, keepdims=True)
            o.append((_dot(ep.astype(BF16), vp) + _dot(en.astype(BF16), vn)) / l)
        out_ref[0, :, hs] = _finish_head(o[0], o[1], lam, gs_ref[...], lam_init).astype(out_ref.dtype)
    _mem_attend(qm_ref[0], mk_ref, mv_ref, out_ref)


def _diff_sample(z, cache_k, cache_v, k_new, v_new, mem_k, mem_v, lq, g_sub, lam_init):
    B, Ts, _ = z.shape
    P0 = cache_k.shape[1]
    per_b = lambda b: (b, 0, 0)
    return pl.pallas_call(
        functools.partial(_diff_sample_kernel, p0=P0, lam_init=lam_init),
        grid=(B,),
        in_specs=[_resident(lq.shape, lambda b: (0, 0)),
                  _resident((1, HEAD_W), lambda b: (0, 0)),
                  pl.BlockSpec((1, Ts, MIX_W), lambda b: (b, 0, 0)),
                  pl.BlockSpec((1, Ts, MIX_W), lambda b: (b, 0, 1)),
                  pl.BlockSpec((1, P0, MIX_W), per_b),
                  pl.BlockSpec((1, P0, MIX_W), per_b),
                  pl.BlockSpec((1, Ts, MIX_W), per_b),
                  pl.BlockSpec((1, Ts, MIX_W), per_b),
                  pl.BlockSpec((1,) + mem_k.shape[1:], per_b),
                  pl.BlockSpec((1,) + mem_v.shape[1:], per_b)],
        out_specs=pl.BlockSpec((1, Ts, 2 * MIX_W), per_b),
        out_shape=jax.ShapeDtypeStruct((B, Ts, 2 * MIX_W), BF16),
        compiler_params=_cparams("arbitrary"),
        name="diff_sample",
    )(lq, g_sub.reshape(1, HEAD_W), z, z, cache_k, cache_v, k_new, v_new, mem_k, mem_v)


def _lambda_init(layer_idx):
    return 0.8 - 0.6 * math.exp(-0.3 * layer_idx)


def _trunk(x, pos0, pool_hist, mem_k, mem_v, past_k, past_v, w, tm_rows, tm_seq, tq):
    B, T, D = x.shape
    R = B * T
    xr = x.reshape(R, D)
    hist16 = jnp.pad(pool_hist, ((0, 0), (HALO - POOL_HIST, 0), (0, 0)))

    mix, qm = _proj0(xr, w["g_attn"][0], w["w_in"][0], tm_rows)
    mix3 = mix.reshape(B, T, MIX_W)
    cat0 = _mix0(mix3, hist16, qm.reshape(B, T, MIX_W), mem_k[0], mem_v[0],
                 w["w_pool"], w["pool_scale"], tm_seq, pos0)
    new_pool = jnp.concatenate([pool_hist, mix3], axis=1)[:, -POOL_HIST:]
    x1 = _outmlp(xr, cat0.reshape(R, D), w["w_out"][0], w["g_ffn"][0], w["w_ff1"][0], w["w_ff2"][0],
                 w["g_final"], tm_rows, final_norm=False)

    prompt = past_k is None
    z, k, v, *bf = _proj1(x1, w["g_attn"][1], w["w_in"][1], w["g_kv"], w["w_kv"], tm_rows, T, prompt)
    z3 = z.reshape(B, T, D)
    lam_init = _lambda_init(1)
    if prompt:
        k1a, k2a, vb = (a.reshape(B, T, MIX_W) for a in bf)
        cat1 = _diff_prompt(z3, k1a, k2a, vb, mem_k[1], mem_v[1], w["lambda_qk"], w["g_subln"], lam_init, tq)
    else:
        cat1 = _diff_sample(z3, past_k, past_v, k.reshape(B, T, MIX_W), v.reshape(B, T, MIX_W),
                            mem_k[1], mem_v[1], w["lambda_qk"], w["g_subln"], lam_init)
    y = _outmlp(x1, cat1.reshape(R, D), w["w_out"][1], w["g_ffn"][1], w["w_ff1"][1], w["w_ff2"][1],
                w["g_final"], tm_rows, final_norm=True)
    shp = (B, T, N_HEADS, HEAD_W)
    return y.reshape(B, T, D), new_pool[None], k.reshape(shp), v.reshape(shp)


def kernel(x_prompt, x_sample, mem_prompt, cache_k, cache_v, cache_mem_k, cache_mem_v, state_pool, g_attn, w_in, w_out, g_mem, w_mem_kv, g_ffn, w_ff1, w_ff2, w_pool, pool_scale, lambda_qk, g_subln, g_kv, w_kv, g_final):
    B, T, D = x_prompt.shape
    Bs, Ts, _ = x_sample.shape
    P0 = cache_k.shape[1]
    L = g_attn.shape[0]
    n_mem = mem_prompt.shape[1]
    w = dict(g_attn=g_attn, w_in=w_in.astype(BF16), w_out=w_out.astype(BF16), g_ffn=g_ffn,
             w_ff1=w_ff1.astype(BF16), w_ff2=w_ff2.astype(BF16), w_pool=w_pool[0].astype(BF16),
             pool_scale=pool_scale[0], lambda_qk=lambda_qk[0], g_subln=g_subln[0], g_kv=g_kv,
             w_kv=w_kv.astype(BF16), g_final=g_final)

    mem_k_p, mem_v_p = _mem_kv(mem_prompt, g_mem, w_mem_kv.astype(BF16))
    hist0 = jnp.zeros((B, POOL_HIST, MIX_W), F32)
    y_p, pool_p, k_p, v_p = _trunk(x_prompt, 0, hist0, mem_k_p, mem_v_p, None, None, w,
                                   tm_rows=512, tm_seq=512, tq=512)
    y_s, pool_s, k_s, v_s = _trunk(x_sample, P0, state_pool[0],
                                   cache_mem_k.reshape(L, Bs, n_mem, MIX_W),
                                   cache_mem_v.reshape(L, Bs, n_mem, MIX_W),
                                   cache_k.reshape(Bs, P0, MIX_W), cache_v.reshape(Bs, P0, MIX_W), w,
                                   tm_rows=Bs * Ts, tm_seq=Ts, tq=None)
    mshape = (L, B, n_mem, N_HEADS, HEAD_W)
    return (y_p, y_s, mem_k_p.reshape(mshape), mem_v_p.reshape(mshape), pool_p, k_p, v_p, pool_s, k_s, v_s)
```

```python
import functools
import math

import jax
import jax.numpy as jnp
import ml_dtypes
import numpy as np
from jax import lax
from jax.experimental import pallas as pl
from jax.experimental.pallas import tpu as pltpu

F32 = jnp.float32
BF16 = jnp.bfloat16

CHUNK = 64
POOL_WINDOWS = (2, 4, 8, 16)
POOL_HIST = 15
HALO = 16
N_HEADS = 4
HEAD_W = 128
DIFF_D = 64
MIX_W = N_HEADS * HEAD_W
EPS = 1e-6
NEG_BIG = -1e30
LOG2E = math.log2(math.e)
Q_SCALE = DIFF_D ** -0.5 * LOG2E
N_SPLIT = 4
VMEM_LIMIT = 56 * 1024 * 1024


def _cparams(*sem):
    return pltpu.CompilerParams(dimension_semantics=sem, vmem_limit_bytes=VMEM_LIMIT)


def _resident(shape, index_map):
    return pl.BlockSpec(shape, index_map, pipeline_mode=pl.Buffered(1))


def _inv_rms(x):
    return lax.rsqrt(jnp.mean(x * x, axis=-1, keepdims=True) + EPS)


def _dot(a, b):
    return jnp.dot(a, b, preferred_element_type=F32)


def _dot_nt(a, b):
    return lax.dot_general(a, b, (((1,), (1,)), ((), ())), preferred_element_type=F32)


def _memkv_kernel(x_ref, g_ref, w_ref, k_ref, v_ref):
    x = x_ref[0]
    h = (x * _inv_rms(x) * g_ref[0]).astype(BF16)
    kv = _dot(h, w_ref[0])
    for hd in range(N_HEADS):
        k_ref[0, 0, :, hd, :] = kv[:, hd * HEAD_W:(hd + 1) * HEAD_W]
        v_ref[0, 0, :, hd, :] = kv[:, MIX_W + hd * HEAD_W:MIX_W + (hd + 1) * HEAD_W]


def _mem_kv(mem, g_mem, w_mem_kv):
    B, M, D = mem.shape
    L = g_mem.shape[0]
    out = jax.ShapeDtypeStruct((L, B, M, N_HEADS, HEAD_W), F32)
    return pl.pallas_call(
        _memkv_kernel,
        grid=(L, B),
        in_specs=[pl.BlockSpec((1, M, D), lambda l, b: (b, 0, 0)),
                  pl.BlockSpec((1, 1, D), lambda l, b: (l, 0, 0)),
                  pl.BlockSpec((1, D, 2 * MIX_W), lambda l, b: (l, 0, 0))],
        out_specs=[pl.BlockSpec((1, 1, M, N_HEADS, HEAD_W), lambda l, b: (l, b, 0, 0, 0)),
                   pl.BlockSpec((1, 1, M, N_HEADS, HEAD_W), lambda l, b: (l, b, 0, 0, 0))],
        out_shape=[out, out],
        compiler_params=_cparams("arbitrary", "arbitrary"),
        name="mem_kv",
    )(mem, g_mem.reshape(L, 1, D), w_mem_kv)


def _proj0_kernel(x_ref, g_ref, w_ref, mix_ref, qm_ref):
    x = x_ref[...]
    h = (x * _inv_rms(x) * g_ref[...]).astype(BF16)
    z = _dot(h, w_ref[...])
    mix_ref[...] = z[:, :MIX_W]
    qm_ref[...] = z[:, MIX_W:].astype(BF16)


def _proj0(x, g, w, tm):
    R, D = x.shape
    return pl.pallas_call(
        _proj0_kernel,
        grid=(R // tm,),
        in_specs=[pl.BlockSpec((tm, D), lambda i: (i, 0)),
                  _resident((1, D), lambda i: (0, 0)),
                  _resident((D, 2 * MIX_W), lambda i: (0, 0))],
        out_specs=[pl.BlockSpec((tm, MIX_W), lambda i: (i, 0)),
                   pl.BlockSpec((tm, MIX_W), lambda i: (i, 0))],
        out_shape=[jax.ShapeDtypeStruct((R, MIX_W), F32),
                   jax.ShapeDtypeStruct((R, MIX_W), BF16)],
        compiler_params=_cparams("arbitrary"),
        name="proj0",
    )(x, g.reshape(1, D), w)


def _mem_attend(qm, mk_ref, mv_ref, out_ref):
    scale = HEAD_W ** -0.5
    for h in range(N_HEADS):
        hs = slice(h * HEAD_W, (h + 1) * HEAD_W)
        kh = mk_ref[0, :, h, :].astype(BF16)
        vh = mv_ref[0, :, h, :].astype(BF16)
        s = _dot_nt(qm[:, hs], kh) * scale
        m = jnp.max(s, axis=-1, keepdims=True)
        e = jnp.exp(s - m)
        l = jnp.sum(e, axis=-1, keepdims=True)
        o = _dot(e.astype(BF16), vh) / l
        out_ref[0, :, MIX_W + h * HEAD_W:MIX_W + (h + 1) * HEAD_W] = o.astype(out_ref.dtype)


def _mix0_kernel(mix_ref, halo_ref, hist_ref, qm_ref, mk_ref, mv_ref, wp_ref, ps_ref, out_ref, *, tm, pos0):
    i = pl.program_id(1)
    u = mix_ref[0]
    halo = jnp.where(i == 0, hist_ref[0], halo_ref[0])
    ext = jnp.concatenate([halo, u], axis=0)
    pos = pos0 + i * tm + lax.broadcasted_iota(jnp.int32, (tm, HEAD_W), 0)
    for g, w in enumerate(POOL_WINDOWS):
        gs = slice(g * HEAD_W, (g + 1) * HEAD_W)
        acc = ext[:, gs]
        span = 1
        while span < w:
            acc = acc + pltpu.roll(acc, span, 0)
            span *= 2
        cnt = jnp.minimum(pos + 1, w).astype(F32)
        d = acc[HALO:] / cnt - u[:, gs]
        t = _dot(d.astype(BF16), wp_ref[g]) * ps_ref[:, gs]
        out_ref[0, :, gs] = t.astype(out_ref.dtype)
    _mem_attend(qm_ref[0], mk_ref, mv_ref, out_ref)


def _mix0(mix, hist16, qm, mem_k, mem_v, w_pool, pool_scale, tm, pos0):
    B, T, _ = mix.shape
    hb = tm // HALO
    return pl.pallas_call(
        functools.partial(_mix0_kernel, tm=tm, pos0=pos0),
        grid=(B, T // tm),
        in_specs=[pl.BlockSpec((1, tm, MIX_W), lambda b, i: (b, i, 0)),
                  pl.BlockSpec((1, HALO, MIX_W), lambda b, i: (b, jnp.maximum(i * hb - 1, 0), 0)),
                  pl.BlockSpec((1, HALO, MIX_W), lambda b, i: (b, 0, 0)),
                  pl.BlockSpec((1, tm, MIX_W), lambda b, i: (b, i, 0)),
                  pl.BlockSpec((1,) + mem_k.shape[1:], lambda b, i: (b, 0, 0, 0)),
                  pl.BlockSpec((1,) + mem_v.shape[1:], lambda b, i: (b, 0, 0, 0)),
                  _resident(w_pool.shape, lambda b, i: (0, 0, 0)),
                  _resident((1, MIX_W), lambda b, i: (0, 0))],
        out_specs=pl.BlockSpec((1, tm, 2 * MIX_W), lambda b, i: (b, i, 0)),
        out_shape=jax.ShapeDtypeStruct((B, T, 2 * MIX_W), BF16),
        compiler_params=_cparams("arbitrary", "arbitrary"),
        name="mix0",
    )(mix, mix, hist16, qm, mem_k, mem_v, w_pool, pool_scale.reshape(1, MIX_W))


def _outmlp_kernel(x_ref, cat_ref, wo_ref, g_ref, w1_ref, w2_ref, gf_ref, out_ref, a_ref, *, final_norm, ff_chunk):
    x1 = x_ref[...] + _dot(cat_ref[...], wo_ref[...])
    h = (x1 * _inv_rms(x1) * g_ref[...]).astype(BF16)
    for c in range(w1_ref.shape[1] // ff_chunk):
        cs = slice(c * ff_chunk, (c + 1) * ff_chunk)
        a = jnp.maximum(_dot(h, w1_ref[:, cs]), 0.0)
        a_ref[:, cs] = (a * a).astype(BF16)
    x2 = x1 + _dot(a_ref[...], w2_ref[...])
    if final_norm:
        x2 = x2 * _inv_rms(x2) * gf_ref[...]
    out_ref[...] = x2


def _outmlp(x, cat, w_out, g_ffn, w1, w2, g_final, tm, final_norm):
    R, D = x.shape
    FF = w1.shape[1]
    return pl.pallas_call(
        functools.partial(_outmlp_kernel, final_norm=final_norm, ff_chunk=1024),
        grid=(R // tm,),
        in_specs=[pl.BlockSpec((tm, D), lambda i: (i, 0)),
                  pl.BlockSpec((tm, D), lambda i: (i, 0)),
                  _resident((D, D), lambda i: (0, 0)),
                  _resident((1, D), lambda i: (0, 0)),
                  _resident((D, FF), lambda i: (0, 0)),
                  _resident((FF, D), lambda i: (0, 0)),
                  _resident((1, D), lambda i: (0, 0))],
        out_specs=pl.BlockSpec((tm, D), lambda i: (i, 0)),
        out_shape=jax.ShapeDtypeStruct((R, D), F32),
        scratch_shapes=[pltpu.VMEM((tm, FF), BF16)],
        compiler_params=_cparams("arbitrary"),
        name="outmlp",
    )(x, cat, w_out, g_ffn.reshape(1, D), w1, w2, g_final.reshape(1, D))


def _lane_window(lane, lo):
    return (lane >= lo) & (lane < lo + N_SPLIT)


def _proj1_kernel(x_ref, ga_ref, wi_ref, gk_ref, wk_ref, z_ref, k_ref, v_ref, *bf_refs, seq_len):
    tm = x_ref.shape[0]
    x = x_ref[...]
    xn = x * _inv_rms(x)
    z = _dot((xn * ga_ref[...]).astype(BF16), wi_ref[...])
    z_ref[:, :MIX_W] = (z[:, :MIX_W] * Q_SCALE).astype(BF16)
    z_ref[:, MIX_W:] = z[:, MIX_W:].astype(BF16)
    kv = _dot((xn * gk_ref[...]).astype(BF16), wk_ref[...])
    k = kv[:, :MIX_W]
    v = kv[:, MIX_W:]
    for h in range(N_HEADS):
        k_ref[:, h, :] = k[:, h * HEAD_W:(h + 1) * HEAD_W]
        v_ref[:, h, :] = v[:, h * HEAD_W:(h + 1) * HEAD_W]
    if bf_refs:
        k1a_ref, k2a_ref, vb_ref = bf_refs
        lane = lax.broadcasted_iota(jnp.int32, (tm, HEAD_W), 1)
        pos = (pl.program_id(0) * tm) % seq_len + lax.broadcasted_iota(jnp.int32, (tm, HEAD_W), 0)
        off = pos % CHUNK
        base = (pos - off).astype(F32)
        off = off.astype(F32)

        def pos_lanes(lo):
            return jnp.where(_lane_window(lane, lo), base, jnp.where(_lane_window(lane, lo + N_SPLIT), off, 0.0))

        pos_hi = pos_lanes(DIFF_D)
        pos_lo = pos_lanes(0)
        for h in range(N_HEADS):
            hs = slice(h * HEAD_W, (h + 1) * HEAD_W)
            k1a_ref[:, hs] = jnp.where(lane < DIFF_D, k[:, hs], pos_hi).astype(BF16)
            k2a_ref[:, hs] = jnp.where(lane >= DIFF_D, k[:, hs], pos_lo).astype(BF16)
        vb_ref[...] = v.astype(BF16)


def _proj1(x, g_attn, w_in, g_kv, w_kv, tm, seq_len, attn_operands):
    R, D = x.shape
    row = lambda i: (i, 0)
    fix = lambda i: (0, 0)
    n_bf = 3 if attn_operands else 0
    return pl.pallas_call(
        functools.partial(_proj1_kernel, seq_len=seq_len),
        grid=(R // tm,),
        in_specs=[pl.BlockSpec((tm, D), row),
                  _resident((1, D), fix), _resident((D, 2 * MIX_W), fix),
                  _resident((1, D), fix), _resident((D, 2 * MIX_W), fix)],
        out_specs=[pl.BlockSpec((tm, 2 * MIX_W), row)]
                  + [pl.BlockSpec((tm, N_HEADS, HEAD_W), lambda i: (i, 0, 0))] * 2
                  + [pl.BlockSpec((tm, MIX_W), row)] * n_bf,
        out_shape=[jax.ShapeDtypeStruct((R, 2 * MIX_W), BF16)]
                  + [jax.ShapeDtypeStruct((R, N_HEADS, HEAD_W), F32)] * 2
                  + [jax.ShapeDtypeStruct((R, MIX_W), BF16)] * n_bf,
        compiler_params=_cparams("arbitrary"),
        name="proj1",
    )(x, g_attn.reshape(1, D), w_in, g_kv.reshape(1, D), w_kv)


def _lambda(lq, lam_init):
    a = jnp.sum(lq[0:1] * lq[1:2], axis=-1, keepdims=True)
    b = jnp.sum(lq[2:3] * lq[3:4], axis=-1, keepdims=True)
    return jnp.exp(a) - jnp.exp(b) + lam_init


def _slope2(h):
    return 2.0 ** (-8.0 * (h + 1) / N_HEADS) * LOG2E


def _bf16_pieces(c):
    out = []
    for _ in range(N_SPLIT):
        p = float(np.asarray(c, dtype=ml_dtypes.bfloat16))
        out.append(p)
        c -= p
    return out


def _slope_lanes(h, lo, shape):
    lane = lax.broadcasted_iota(jnp.int32, shape, 1)
    out = jnp.zeros(shape, F32)
    for p, piece in enumerate(_bf16_pieces(_slope2(h))):
        out = jnp.where((lane == lo + p) | (lane == lo + N_SPLIT + p), piece, out)
    return out.astype(BF16)


def _finish_head(o1, o2, lam, gs, lam_init):
    o = o1 - lam * o2
    return o * _inv_rms(o) * gs * (1.0 - lam_init)


def _diff_kernel(lq_ref, gs_ref, q_ref, qm_ref, k1_ref, k2_ref, v_ref, mk_ref, mv_ref, out_ref,
                 qa_ref, acc_ref, m_ref, *, tq, lam_init):
    i = pl.program_id(1)
    nblk = tq // HEAD_W
    k_refs = (k1_ref, k2_ref)
    row = lax.broadcasted_iota(jnp.int32, (tq, tq), 0)
    col = lax.broadcasted_iota(jnp.int32, (tq, tq), 1)
    ahead = jnp.maximum(col - row, 0).astype(F32)
    vis = (col // CHUNK) <= (row // CHUNK)
    ones = jnp.ones((tq, HEAD_W), BF16)
    lane = lax.broadcasted_iota(jnp.int32, (tq, HEAD_W), 1)
    q0 = pl.multiple_of(i * tq, tq)

    def hslice(n):
        return slice((n // 2) * HEAD_W, (n // 2 + 1) * HEAD_W)

    def scores(n, k0):
        return _dot_nt(qa_ref[n], k_refs[n % 2][0, pl.ds(k0, tq), hslice(n)])

    def v_aug(n, k0):
        return jnp.concatenate([v_ref[0, pl.ds(k0, tq), hslice(n)], ones], axis=1)

    def update(n, t, vt, first):
        blocks = [t[:, b * HEAD_W:(b + 1) * HEAD_W] for b in range(nblk)]
        m_row = jnp.max(functools.reduce(jnp.maximum, blocks), axis=-1, keepdims=True)
        if first:
            m_new = jnp.broadcast_to(m_row, (tq, HEAD_W))
        else:
            m_old = m_ref[n]
            m_new = jnp.maximum(m_old, m_row)
        p = jnp.concatenate([jnp.exp2(b - m_new) for b in blocks], axis=1).astype(BF16)
        pv = _dot(p, vt)
        if first:
            acc_ref[n] = pv
        else:
            alpha = jnp.exp2(m_old - m_new)
            acc_ref[n] = jnp.concatenate([alpha, alpha], axis=1) * acc_ref[n] + pv
        m_ref[n] = m_new

    for h in range(N_HEADS):
        qh = q_ref[0, :, h * HEAD_W:(h + 1) * HEAD_W]
        qa_ref[2 * h] = jnp.where(lane < DIFF_D, qh, _slope_lanes(h, DIFF_D, qh.shape))
        qa_ref[2 * h + 1] = jnp.where(lane >= DIFF_D, qh, _slope_lanes(h, 0, qh.shape))
        fix = jnp.where(vis, (-2.0 * _slope2(h)) * ahead, NEG_BIG)
        for n in (2 * h, 2 * h + 1):
            update(n, scores(n, q0) + fix, v_aug(n, q0), True)

    def body(j, carry):
        k0 = pl.multiple_of(j * tq, tq)
        for n in range(2 * N_HEADS):
            update(n, scores(n, k0), v_aug(n, k0), False)
        return carry

    lax.fori_loop(0, i, body, 0)

    lam = _lambda(lq_ref[...], lam_init)
    for h in range(N_HEADS):
        o = [acc_ref[n, :, :HEAD_W] / acc_ref[n, :, HEAD_W:] for n in (2 * h, 2 * h + 1)]
        out_ref[0, :, h * HEAD_W:(h + 1) * HEAD_W] = _finish_head(
            o[0], o[1], lam, gs_ref[...], lam_init).astype(out_ref.dtype)
    _mem_attend(qm_ref[0], mk_ref, mv_ref, out_ref)


def _diff_prompt(z, k1a, k2a, vb, mem_k, mem_v, lq, g_sub, lam_init, tq):
    B, T, _ = z.shape
    n_chain = 2 * N_HEADS
    return pl.pallas_call(
        functools.partial(_diff_kernel, tq=tq, lam_init=lam_init),
        grid=(B, T // tq),
        in_specs=[_resident(lq.shape, lambda b, i: (0, 0)),
                  _resident((1, HEAD_W), lambda b, i: (0, 0)),
                  pl.BlockSpec((1, tq, MIX_W), lambda b, i: (b, i, 0)),
                  pl.BlockSpec((1, tq, MIX_W), lambda b, i: (b, i, 1)),
                  _resident((1, T, MIX_W), lambda b, i: (b, 0, 0)),
                  _resident((1, T, MIX_W), lambda b, i: (b, 0, 0)),
                  _resident((1, T, MIX_W), lambda b, i: (b, 0, 0)),
                  pl.BlockSpec((1,) + mem_k.shape[1:], lambda b, i: (b, 0, 0, 0)),
                  pl.BlockSpec((1,) + mem_v.shape[1:], lambda b, i: (b, 0, 0, 0))],
        out_specs=pl.BlockSpec((1, tq, 2 * MIX_W), lambda b, i: (b, i, 0)),
        out_shape=jax.ShapeDtypeStruct((B, T, 2 * MIX_W), BF16),
        scratch_shapes=[pltpu.VMEM((n_chain, tq, HEAD_W), BF16),
                        pltpu.VMEM((n_chain, tq, 2 * HEAD_W), F32),
                        pltpu.VMEM((n_chain, tq, HEAD_W), F32)],
        compiler_params=_cparams("arbitrary", "arbitrary"),
        name="diff_prompt",
    )(lq, g_sub.reshape(1, HEAD_W), z, z, k1a, k2a, vb, mem_k, mem_v)


def _diff_sample_kernel(lq_ref, gs_ref, q_ref, qm_ref, ck_ref, cv_ref, nk_ref, nv_ref, mk_ref, mv_ref, out_ref,
                        *, p0, lam_init):
    ts = q_ref.shape[1]
    lam = _lambda(lq_ref[...], lam_init)
    qpos_p = p0 + lax.broadcasted_iota(jnp.int32, (ts, p0), 0)
    dist_p = (qpos_p - lax.broadcasted_iota(jnp.int32, (ts, p0), 1)).astype(F32)
    row = p0 + lax.broadcasted_iota(jnp.int32, (ts, ts), 0)
    col = p0 + lax.broadcasted_iota(jnp.int32, (ts, ts), 1)
    dist_n = jnp.abs(row - col).astype(F32)
    vis_n = (col // CHUNK) <= (row // CHUNK)
    lane = lax.broadcasted_iota(jnp.int32, (ts, HEAD_W), 1)
    for h in range(N_HEADS):
        hs = slice(h * HEAD_W, (h + 1) * HEAD_W)
        qh = q_ref[0, :, hs]
        zero = jnp.zeros_like(qh)
        qc = (jnp.where(lane < DIFF_D, qh, zero), jnp.where(lane >= DIFF_D, qh, zero))
        kp = ck_ref[0, :, h, :].astype(BF16)
        vp = cv_ref[0, :, h, :].astype(BF16)
        kn = nk_ref[0, :, h, :].astype(BF16)
        vn = nv_ref[0, :, h, :].astype(BF16)
        bias_p = -_slope2(h) * dist_p
        bias_n = jnp.where(vis_n, -_slope2(h) * dist_n, NEG_BIG)
        o = []
        for c in range(2):
            sp = _dot_nt(qc[c], kp) + bias_p
            sn = _dot_nt(qc[c], kn) + bias_n
            m = jnp.maximum(jnp.max(sp, axis=-1, keepdims=True), jnp.max(sn, axis=-1, keepdims=True))
            ep = jnp.exp2(sp - m)
            en = jnp.exp2(sn - m)
            l = jnp.sum(ep, axis=-1, keepdims=True) + jnp.sum(en, axis=-1, keepdims=True)
            o.append((_dot(ep.astype(BF16), vp) + _dot(en.astype(BF16), vn)) / l)
        out_ref[0, :, hs] = _finish_head(o[0], o[1], lam, gs_ref[...], lam_init).astype(out_ref.dtype)
    _mem_attend(qm_ref[0], mk_ref, mv_ref, out_ref)


def _diff_sample(z, cache_k, cache_v, k_new, v_new, mem_k, mem_v, lq, g_sub, lam_init):
    B, Ts, _ = z.shape
    P0 = cache_k.shape[1]
    per_b = lambda b: (b, 0, 0, 0)
    heads = (N_HEADS, HEAD_W)
    return pl.pallas_call(
        functools.partial(_diff_sample_kernel, p0=P0, lam_init=lam_init),
        grid=(B,),
        in_specs=[_resident(lq.shape, lambda b: (0, 0)),
                  _resident((1, HEAD_W), lambda b: (0, 0)),
                  pl.BlockSpec((1, Ts, MIX_W), lambda b: (b, 0, 0)),
                  pl.BlockSpec((1, Ts, MIX_W), lambda b: (b, 0, 1)),
                  pl.BlockSpec((1, P0) + heads, per_b),
                  pl.BlockSpec((1, P0) + heads, per_b),
                  pl.BlockSpec((1, Ts) + heads, per_b),
                  pl.BlockSpec((1, Ts) + heads, per_b),
                  pl.BlockSpec((1,) + mem_k.shape[1:], per_b),
                  pl.BlockSpec((1,) + mem_v.shape[1:], per_b)],
        out_specs=pl.BlockSpec((1, Ts, 2 * MIX_W), lambda b: (b, 0, 0)),
        out_shape=jax.ShapeDtypeStruct((B, Ts, 2 * MIX_W), BF16),
        compiler_params=_cparams("arbitrary"),
        name="diff_sample",
    )(lq, g_sub.reshape(1, HEAD_W), z, z, cache_k, cache_v, k_new, v_new, mem_k, mem_v)


def _lambda_init(layer_idx):
    return 0.8 - 0.6 * math.exp(-0.3 * layer_idx)


def _trunk(x, pos0, pool_hist, mem_k, mem_v, past_k, past_v, w, tm_rows, tm_seq, tq):
    B, T, D = x.shape
    R = B * T
    xr = x.reshape(R, D)
    hist16 = jnp.pad(pool_hist, ((0, 0), (HALO - POOL_HIST, 0), (0, 0)))

    mix, qm = _proj0(xr, w["g_attn"][0], w["w_in"][0], tm_rows)
    mix3 = mix.reshape(B, T, MIX_W)
    cat0 = _mix0(mix3, hist16, qm.reshape(B, T, MIX_W), mem_k[0], mem_v[0],
                 w["w_pool"], w["pool_scale"], tm_seq, pos0)
    new_pool = jnp.concatenate([pool_hist, mix3], axis=1)[:, -POOL_HIST:]
    x1 = _outmlp(xr, cat0.reshape(R, D), w["w_out"][0], w["g_ffn"][0], w["w_ff1"][0], w["w_ff2"][0],
                 w["g_final"], tm_rows, final_norm=False)

    prompt = past_k is None
    z, k, v, *bf = _proj1(x1, w["g_attn"][1], w["w_in"][1], w["g_kv"], w["w_kv"], tm_rows, T, prompt)
    z3 = z.reshape(B, T, D)
    lam_init = _lambda_init(1)
    if prompt:
        k1a, k2a, vb = (a.reshape(B, T, MIX_W) for a in bf)
        cat1 = _diff_prompt(z3, k1a, k2a, vb, mem_k[1], mem_v[1], w["lambda_qk"], w["g_subln"], lam_init, tq)
    else:
        shp = (B, T, N_HEADS, HEAD_W)
        cat1 = _diff_sample(z3, past_k, past_v, k.reshape(shp), v.reshape(shp),
                            mem_k[1], mem_v[1], w["lambda_qk"], w["g_subln"], lam_init)
    y = _outmlp(x1, cat1.reshape(R, D), w["w_out"][1], w["g_ffn"][1], w["w_ff1"][1], w["w_ff2"][1],
                w["g_final"], tm_rows, final_norm=True)
    shp = (B, T, N_HEADS, HEAD_W)
    return y.reshape(B, T, D), new_pool[None], k.reshape(shp), v.reshape(shp)


def kernel(x_prompt, x_sample, mem_prompt, cache_k, cache_v, cache_mem_k, cache_mem_v, state_pool, g_attn, w_in, w_out, g_mem, w_mem_kv, g_ffn, w_ff1, w_ff2, w_pool, pool_scale, lambda_qk, g_subln, g_kv, w_kv, g_final):
    B, T, D = x_prompt.shape
    Bs, Ts, _ = x_sample.shape
    P0 = cache_k.shape[1]
    w = dict(g_attn=g_attn, w_in=w_in.astype(BF16), w_out=w_out.astype(BF16), g_ffn=g_ffn,
             w_ff1=w_ff1.astype(BF16), w_ff2=w_ff2.astype(BF16), w_pool=w_pool[0].astype(BF16),
             pool_scale=pool_scale[0], lambda_qk=lambda_qk[0], g_subln=g_subln[0], g_kv=g_kv,
             w_kv=w_kv.astype(BF16), g_final=g_final)

    mem_k_p, mem_v_p = _mem_kv(mem_prompt, g_mem, w_mem_kv.astype(BF16))
    hist0 = jnp.zeros((B, POOL_HIST, MIX_W), F32)
    y_p, pool_p, k_p, v_p = _trunk(x_prompt, 0, hist0, mem_k_p, mem_v_p, None, None, w,
                                   tm_rows=512, tm_seq=512, tq=512)
    y_s, pool_s, k_s, v_s = _trunk(x_sample, P0, state_pool[0], cache_mem_k, cache_mem_v, cache_k, cache_v, w,
                                   tm_rows=Bs * Ts, tm_seq=Ts, tq=None)
    return (y_p, y_s, mem_k_p, mem_v_p, pool_p, k_p, v_p, pool_s, k_s, v_s)
```

```python
import functools
import math

import jax
import jax.numpy as jnp
import ml_dtypes
import numpy as np
from jax import lax
from jax.experimental import pallas as pl
from jax.experimental.pallas import tpu as pltpu

F32 = jnp.float32
BF16 = jnp.bfloat16

CHUNK = 64
POOL_WINDOWS = (2, 4, 8, 16)
POOL_HIST = 15
HALO = 16
N_HEADS = 4
HEAD_W = 128
DIFF_D = 64
MIX_W = N_HEADS * HEAD_W
EPS = 1e-6
NEG_BIG = -1e30
LOG2E = math.log2(math.e)
Q_SCALE = DIFF_D ** -0.5 * LOG2E
N_SPLIT = 4
VMEM_LIMIT = 56 * 1024 * 1024


def _cparams(*sem):
    return pltpu.CompilerParams(dimension_semantics=sem, vmem_limit_bytes=VMEM_LIMIT)


def _resident(shape, index_map):
    return pl.BlockSpec(shape, index_map, pipeline_mode=pl.Buffered(1))


def _inv_rms(x):
    return lax.rsqrt(jnp.mean(x * x, axis=-1, keepdims=True) + EPS)


def _dot(a, b):
    return jnp.dot(a, b, preferred_element_type=F32)


def _dot_nt(a, b):
    return lax.dot_general(a, b, (((1,), (1,)), ((), ())), preferred_element_type=F32)


def _memkv_kernel(x_ref, g_ref, w_ref, k_ref, v_ref, kd_ref, vd_ref):
    x = x_ref[0]
    h = (x * _inv_rms(x) * g_ref[0]).astype(BF16)
    kv = _dot(h, w_ref[0])
    kd_ref[0, 0] = kv[:, :MIX_W]
    vd_ref[0, 0] = kv[:, MIX_W:]
    for hd in range(N_HEADS):
        k_ref[0, 0, :, hd, :] = kv[:, hd * HEAD_W:(hd + 1) * HEAD_W]
        v_ref[0, 0, :, hd, :] = kv[:, MIX_W + hd * HEAD_W:MIX_W + (hd + 1) * HEAD_W]


def _mem_kv(mem, g_mem, w_mem_kv):
    B, M, D = mem.shape
    L = g_mem.shape[0]
    out = jax.ShapeDtypeStruct((L, B, M, N_HEADS, HEAD_W), F32)
    dense = jax.ShapeDtypeStruct((L, B, M, MIX_W), F32)
    return pl.pallas_call(
        _memkv_kernel,
        grid=(L, B),
        in_specs=[pl.BlockSpec((1, M, D), lambda l, b: (b, 0, 0)),
                  pl.BlockSpec((1, 1, D), lambda l, b: (l, 0, 0)),
                  pl.BlockSpec((1, D, 2 * MIX_W), lambda l, b: (l, 0, 0))],
        out_specs=[pl.BlockSpec((1, 1, M, N_HEADS, HEAD_W), lambda l, b: (l, b, 0, 0, 0))] * 2
                  + [pl.BlockSpec((1, 1, M, MIX_W), lambda l, b: (l, b, 0, 0))] * 2,
        out_shape=[out, out, dense, dense],
        compiler_params=_cparams("arbitrary", "arbitrary"),
        name="mem_kv",
    )(mem, g_mem.reshape(L, 1, D), w_mem_kv)


def _proj0_kernel(x_ref, g_ref, w_ref, mix_ref, qm_ref):
    x = x_ref[...]
    h = (x * _inv_rms(x) * g_ref[...]).astype(BF16)
    z = _dot(h, w_ref[...])
    mix_ref[...] = z[:, :MIX_W]
    qm_ref[...] = z[:, MIX_W:].astype(BF16)


def _proj0(x, g, w, tm):
    R, D = x.shape
    return pl.pallas_call(
        _proj0_kernel,
        grid=(R // tm,),
        in_specs=[pl.BlockSpec((tm, D), lambda i: (i, 0)),
                  _resident((1, D), lambda i: (0, 0)),
                  _resident((D, 2 * MIX_W), lambda i: (0, 0))],
        out_specs=[pl.BlockSpec((tm, MIX_W), lambda i: (i, 0)),
                   pl.BlockSpec((tm, MIX_W), lambda i: (i, 0))],
        out_shape=[jax.ShapeDtypeStruct((R, MIX_W), F32),
                   jax.ShapeDtypeStruct((R, MIX_W), BF16)],
        compiler_params=_cparams("arbitrary"),
        name="proj0",
    )(x, g.reshape(1, D), w)


def _batch_spec(arr):
    zeros = (0,) * (arr.ndim - 1)
    return pl.BlockSpec((1,) + arr.shape[1:], lambda b, *_: (b,) + zeros)


def _head(ref, h):
    if len(ref.shape) == 3:
        return ref[0, :, h * HEAD_W:(h + 1) * HEAD_W]
    return ref[0, :, h, :]


def _mem_attend(qm, mk_ref, mv_ref, out_ref):
    scale = HEAD_W ** -0.5
    for h in range(N_HEADS):
        hs = slice(h * HEAD_W, (h + 1) * HEAD_W)
        kh = _head(mk_ref, h).astype(BF16)
        vh = _head(mv_ref, h).astype(BF16)
        s = _dot_nt(qm[:, hs], kh) * scale
        m = jnp.max(s, axis=-1, keepdims=True)
        e = jnp.exp(s - m)
        l = jnp.sum(e, axis=-1, keepdims=True)
        o = _dot(e.astype(BF16), vh) / l
        out_ref[0, :, MIX_W + h * HEAD_W:MIX_W + (h + 1) * HEAD_W] = o.astype(out_ref.dtype)


def _mix0_kernel(mix_ref, halo_ref, hist_ref, qm_ref, wp_ref, ps_ref, mk_ref, mv_ref, out_ref, *, tm, pos0):
    i = pl.program_id(1)
    u = mix_ref[0]
    halo = jnp.where(i == 0, hist_ref[0], halo_ref[0])
    ext = jnp.concatenate([halo, u], axis=0)
    pos = pos0 + i * tm + lax.broadcasted_iota(jnp.int32, (tm, HEAD_W), 0)
    for g, w in enumerate(POOL_WINDOWS):
        gs = slice(g * HEAD_W, (g + 1) * HEAD_W)
        acc = ext[:, gs]
        span = 1
        while span < w:
            acc = acc + pltpu.roll(acc, span, 0)
            span *= 2
        cnt = jnp.minimum(pos + 1, w).astype(F32)
        d = acc[HALO:] / cnt - u[:, gs]
        t = _dot(d.astype(BF16), wp_ref[g]) * ps_ref[:, gs]
        out_ref[0, :, gs] = t.astype(out_ref.dtype)
    _mem_attend(qm_ref[0], mk_ref, mv_ref, out_ref)


def _mix0(mix, hist16, qm, mem_k, mem_v, w_pool, pool_scale, tm, pos0):
    B, T, _ = mix.shape
    hb = tm // HALO
    return pl.pallas_call(
        functools.partial(_mix0_kernel, tm=tm, pos0=pos0),
        grid=(B, T // tm),
        in_specs=[pl.BlockSpec((1, tm, MIX_W), lambda b, i: (b, i, 0)),
                  pl.BlockSpec((1, HALO, MIX_W), lambda b, i: (b, jnp.maximum(i * hb - 1, 0), 0)),
                  pl.BlockSpec((1, HALO, MIX_W), lambda b, i: (b, 0, 0)),
                  pl.BlockSpec((1, tm, MIX_W), lambda b, i: (b, i, 0)),
                  _resident(w_pool.shape, lambda b, i: (0, 0, 0)),
                  _resident((1, MIX_W), lambda b, i: (0, 0)),
                  _batch_spec(mem_k), _batch_spec(mem_v)],
        out_specs=pl.BlockSpec((1, tm, 2 * MIX_W), lambda b, i: (b, i, 0)),
        out_shape=jax.ShapeDtypeStruct((B, T, 2 * MIX_W), BF16),
        compiler_params=_cparams("arbitrary", "arbitrary"),
        name="mix0",
    )(mix, mix, hist16, qm, w_pool, pool_scale.reshape(1, MIX_W), mem_k, mem_v)


def _outmlp_kernel(x_ref, cat_ref, wo_ref, g_ref, w1_ref, w2_ref, gf_ref, out_ref, a_ref, *, final_norm, ff_chunk):
    x1 = x_ref[...] + _dot(cat_ref[...], wo_ref[...])
    h = (x1 * _inv_rms(x1) * g_ref[...]).astype(BF16)
    for c in range(w1_ref.shape[1] // ff_chunk):
        cs = slice(c * ff_chunk, (c + 1) * ff_chunk)
        a = jnp.maximum(_dot(h, w1_ref[:, cs]), 0.0)
        a_ref[:, cs] = (a * a).astype(BF16)
    x2 = x1 + _dot(a_ref[...], w2_ref[...])
    if final_norm:
        x2 = x2 * _inv_rms(x2) * gf_ref[...]
    out_ref[...] = x2


def _outmlp(x, cat, w_out, g_ffn, w1, w2, g_final, tm, final_norm):
    R, D = x.shape
    FF = w1.shape[1]
    return pl.pallas_call(
        functools.partial(_outmlp_kernel, final_norm=final_norm, ff_chunk=1024),
        grid=(R // tm,),
        in_specs=[pl.BlockSpec((tm, D), lambda i: (i, 0)),
                  pl.BlockSpec((tm, D), lambda i: (i, 0)),
                  _resident((D, D), lambda i: (0, 0)),
                  _resident((1, D), lambda i: (0, 0)),
                  _resident((D, FF), lambda i: (0, 0)),
                  _resident((FF, D), lambda i: (0, 0)),
                  _resident((1, D), lambda i: (0, 0))],
        out_specs=pl.BlockSpec((tm, D), lambda i: (i, 0)),
        out_shape=jax.ShapeDtypeStruct((R, D), F32),
        scratch_shapes=[pltpu.VMEM((tm, FF), BF16)],
        compiler_params=_cparams("arbitrary"),
        name="outmlp",
    )(x, cat, w_out, g_ffn.reshape(1, D), w1, w2, g_final.reshape(1, D))


def _lane_window(lane, lo):
    return (lane >= lo) & (lane < lo + N_SPLIT)


def _head_copies(buf, slot, hbm_refs, row0, tm, sem):
    out = []
    for t, hbm in enumerate(hbm_refs):
        for h in range(N_HEADS):
            src = buf.at[slot, :, pl.ds(t * MIX_W + h * HEAD_W, HEAD_W)]
            out.append(pltpu.make_async_copy(src, hbm.at[pl.ds(row0, tm), h, :], sem.at[slot]))
    return out


def _proj1_kernel(x_ref, ga_ref, wi_ref, gk_ref, wk_ref, z_ref, *rest, seq_len, n_bf):
    bf_refs = rest[:n_bf]
    k_hbm, v_hbm, kv_buf, sem = rest[n_bf:]
    tm = x_ref.shape[0]
    i = pl.program_id(0)
    n_steps = pl.num_programs(0)
    slot = i % 2
    copies = functools.partial(_head_copies, kv_buf, hbm_refs=(k_hbm, v_hbm), tm=tm, sem=sem)

    x = x_ref[...]
    xn = x * _inv_rms(x)
    z = _dot((xn * ga_ref[...]).astype(BF16), wi_ref[...])
    z_ref[:, :MIX_W] = (z[:, :MIX_W] * Q_SCALE).astype(BF16)
    z_ref[:, MIX_W:] = z[:, MIX_W:].astype(BF16)
    kv = _dot((xn * gk_ref[...]).astype(BF16), wk_ref[...])

    @pl.when(i >= 2)
    def _():
        for c in copies(slot, row0=(i - 2) * tm):
            c.wait()

    kv_buf[slot] = kv
    for c in copies(slot, row0=i * tm):
        c.start()

    @pl.when(i == n_steps - 1)
    def _():
        @pl.when(i >= 1)
        def _():
            for c in copies(1 - slot, row0=(i - 1) * tm):
                c.wait()
        for c in copies(slot, row0=i * tm):
            c.wait()

    if n_bf:
        k = kv[:, :MIX_W]
        v = kv[:, MIX_W:]
        k1a_ref, k2a_ref, vb_ref = bf_refs
        lane = lax.broadcasted_iota(jnp.int32, (tm, HEAD_W), 1)
        pos = (i * tm) % seq_len + lax.broadcasted_iota(jnp.int32, (tm, HEAD_W), 0)
        off = pos % CHUNK
        base = (pos - off).astype(F32)
        off = off.astype(F32)

        def pos_lanes(lo):
            return jnp.where(_lane_window(lane, lo), base, jnp.where(_lane_window(lane, lo + N_SPLIT), off, 0.0))

        pos_hi = pos_lanes(DIFF_D)
        pos_lo = pos_lanes(0)
        for h in range(N_HEADS):
            hs = slice(h * HEAD_W, (h + 1) * HEAD_W)
            k1a_ref[:, hs] = jnp.where(lane < DIFF_D, k[:, hs], pos_hi).astype(BF16)
            k2a_ref[:, hs] = jnp.where(lane >= DIFF_D, k[:, hs], pos_lo).astype(BF16)
        vb_ref[...] = v.astype(BF16)


def _proj1(x, g_attn, w_in, g_kv, w_kv, tm, seq_len, attn_operands):
    R, D = x.shape
    row = lambda i: (i, 0)
    fix = lambda i: (0, 0)
    n_bf = 3 if attn_operands else 0
    return pl.pallas_call(
        functools.partial(_proj1_kernel, seq_len=seq_len, n_bf=n_bf),
        grid=(R // tm,),
        in_specs=[pl.BlockSpec((tm, D), row),
                  _resident((1, D), fix), _resident((D, 2 * MIX_W), fix),
                  _resident((1, D), fix), _resident((D, 2 * MIX_W), fix)],
        out_specs=[pl.BlockSpec((tm, 2 * MIX_W), row)]
                  + [pl.BlockSpec((tm, MIX_W), row)] * n_bf
                  + [pl.BlockSpec(memory_space=pl.ANY)] * 2,
        out_shape=[jax.ShapeDtypeStruct((R, 2 * MIX_W), BF16)]
                  + [jax.ShapeDtypeStruct((R, MIX_W), BF16)] * n_bf
                  + [jax.ShapeDtypeStruct((R, N_HEADS, HEAD_W), F32)] * 2,
        scratch_shapes=[pltpu.VMEM((2, tm, 2 * MIX_W), F32), pltpu.SemaphoreType.DMA((2,))],
        compiler_params=_cparams("arbitrary"),
        name="proj1",
    )(x, g_attn.reshape(1, D), w_in, g_kv.reshape(1, D), w_kv)


def _lambda(lq, lam_init):
    a = jnp.sum(lq[0:1] * lq[1:2], axis=-1, keepdims=True)
    b = jnp.sum(lq[2:3] * lq[3:4], axis=-1, keepdims=True)
    return jnp.exp(a) - jnp.exp(b) + lam_init


def _slope2(h):
    return 2.0 ** (-8.0 * (h + 1) / N_HEADS) * LOG2E


def _bf16_pieces(c):
    out = []
    for _ in range(N_SPLIT):
        p = float(np.asarray(c, dtype=ml_dtypes.bfloat16))
        out.append(p)
        c -= p
    return out


def _slope_lanes(h, lo, shape):
    lane = lax.broadcasted_iota(jnp.int32, shape, 1)
    out = jnp.zeros(shape, F32)
    for p, piece in enumerate(_bf16_pieces(_slope2(h))):
        out = jnp.where((lane == lo + p) | (lane == lo + N_SPLIT + p), piece, out)
    return out.astype(BF16)


def _finish_head(o1, o2, lam, gs, lam_init):
    o = o1 - lam * o2
    return o * _inv_rms(o) * gs * (1.0 - lam_init)


def _diff_kernel(lq_ref, gs_ref, q_ref, qm_ref, k1_ref, k2_ref, v_ref, mk_ref, mv_ref, out_ref,
                 qa_ref, acc_ref, m_ref, *, tq, lam_init):
    i = pl.program_id(1)
    nblk = tq // HEAD_W
    k_refs = (k1_ref, k2_ref)
    row = lax.broadcasted_iota(jnp.int32, (tq, tq), 0)
    col = lax.broadcasted_iota(jnp.int32, (tq, tq), 1)
    ahead = jnp.maximum(col - row, 0).astype(F32)
    vis = (col // CHUNK) <= (row // CHUNK)
    ones = jnp.ones((tq, HEAD_W), BF16)
    lane = lax.broadcasted_iota(jnp.int32, (tq, HEAD_W), 1)
    q0 = pl.multiple_of(i * tq, tq)

    def hslice(n):
        return slice((n // 2) * HEAD_W, (n // 2 + 1) * HEAD_W)

    def scores(n, k0):
        return _dot_nt(qa_ref[n], k_refs[n % 2][0, pl.ds(k0, tq), hslice(n)])

    def v_aug(n, k0):
        return jnp.concatenate([v_ref[0, pl.ds(k0, tq), hslice(n)], ones], axis=1)

    def update(n, t, vt, first):
        blocks = [t[:, b * HEAD_W:(b + 1) * HEAD_W] for b in range(nblk)]
        m_row = jnp.max(functools.reduce(jnp.maximum, blocks), axis=-1, keepdims=True)
        if first:
            m_new = jnp.broadcast_to(m_row, (tq, HEAD_W))
        else:
            m_old = m_ref[n]
            m_new = jnp.maximum(m_old, m_row)
        p = jnp.concatenate([jnp.exp2(b - m_new) for b in blocks], axis=1).astype(BF16)
        pv = _dot(p, vt)
        if first:
            acc_ref[n] = pv
        else:
            alpha = jnp.exp2(m_old - m_new)
            acc_ref[n] = jnp.concatenate([alpha, alpha], axis=1) * acc_ref[n] + pv
        m_ref[n] = m_new

    for h in range(N_HEADS):
        qh = q_ref[0, :, h * HEAD_W:(h + 1) * HEAD_W]
        qa_ref[2 * h] = jnp.where(lane < DIFF_D, qh, _slope_lanes(h, DIFF_D, qh.shape))
        qa_ref[2 * h + 1] = jnp.where(lane >= DIFF_D, qh, _slope_lanes(h, 0, qh.shape))
        fix = jnp.where(vis, (-2.0 * _slope2(h)) * ahead, NEG_BIG)
        for n in (2 * h, 2 * h + 1):
            update(n, scores(n, q0) + fix, v_aug(n, q0), True)

    def body(j, carry):
        k0 = pl.multiple_of(j * tq, tq)
        for n in range(2 * N_HEADS):
            update(n, scores(n, k0), v_aug(n, k0), False)
        return carry

    lax.fori_loop(0, i, body, 0)

    lam = _lambda(lq_ref[...], lam_init)
    for h in range(N_HEADS):
        o = [acc_ref[n, :, :HEAD_W] / acc_ref[n, :, HEAD_W:] for n in (2 * h, 2 * h + 1)]
        out_ref[0, :, h * HEAD_W:(h + 1) * HEAD_W] = _finish_head(
            o[0], o[1], lam, gs_ref[...], lam_init).astype(out_ref.dtype)
    _mem_attend(qm_ref[0], mk_ref, mv_ref, out_ref)


def _diff_prompt(z, k1a, k2a, vb, mem_k, mem_v, lq, g_sub, lam_init, tq):
    B, T, _ = z.shape
    n_chain = 2 * N_HEADS
    return pl.pallas_call(
        functools.partial(_diff_kernel, tq=tq, lam_init=lam_init),
        grid=(B, T // tq),
        in_specs=[_resident(lq.shape, lambda b, i: (0, 0)),
                  _resident((1, HEAD_W), lambda b, i: (0, 0)),
                  pl.BlockSpec((1, tq, MIX_W), lambda b, i: (b, i, 0)),
                  pl.BlockSpec((1, tq, MIX_W), lambda b, i: (b, i, 1)),
                  _resident((1, T, MIX_W), lambda b, i: (b, 0, 0)),
                  _resident((1, T, MIX_W), lambda b, i: (b, 0, 0)),
                  _resident((1, T, MIX_W), lambda b, i: (b, 0, 0)),
                  _batch_spec(mem_k), _batch_spec(mem_v)],
        out_specs=pl.BlockSpec((1, tq, 2 * MIX_W), lambda b, i: (b, i, 0)),
        out_shape=jax.ShapeDtypeStruct((B, T, 2 * MIX_W), BF16),
        scratch_shapes=[pltpu.VMEM((n_chain, tq, HEAD_W), BF16),
                        pltpu.VMEM((n_chain, tq, 2 * HEAD_W), F32),
                        pltpu.VMEM((n_chain, tq, HEAD_W), F32)],
        compiler_params=_cparams("arbitrary", "arbitrary"),
        name="diff_prompt",
    )(lq, g_sub.reshape(1, HEAD_W), z, z, k1a, k2a, vb, mem_k, mem_v)


def _cache_copies(hbm_refs, b, buf, slot, sem):
    return [pltpu.make_async_copy(hbm.at[b, :, h, :], buf.at[slot, t, h], sem.at[slot])
            for t, hbm in enumerate(hbm_refs) for h in range(N_HEADS)]


def _diff_sample_kernel(lq_ref, gs_ref, q_ref, qm_ref, ck_hbm, cv_hbm, nk_ref, nv_ref, mk_ref, mv_ref, out_ref,
                        cache_buf, sem, *, p0, lam_init):
    b = pl.program_id(0)
    slot = b % 2
    copies = functools.partial(_cache_copies, (ck_hbm, cv_hbm), buf=cache_buf, sem=sem)

    @pl.when(b == 0)
    def _():
        for c in copies(b, slot=slot):
            c.start()

    @pl.when(b + 1 < pl.num_programs(0))
    def _():
        for c in copies(b + 1, slot=1 - slot):
            c.start()

    for c in copies(b, slot=slot):
        c.wait()

    ts = q_ref.shape[1]
    lam = _lambda(lq_ref[...], lam_init)
    qpos_p = p0 + lax.broadcasted_iota(jnp.int32, (ts, p0), 0)
    dist_p = (qpos_p - lax.broadcasted_iota(jnp.int32, (ts, p0), 1)).astype(F32)
    row = p0 + lax.broadcasted_iota(jnp.int32, (ts, ts), 0)
    col = p0 + lax.broadcasted_iota(jnp.int32, (ts, ts), 1)
    dist_n = jnp.abs(row - col).astype(F32)
    vis_n = (col // CHUNK) <= (row // CHUNK)
    lane = lax.broadcasted_iota(jnp.int32, (ts, HEAD_W), 1)
    for h in range(N_HEADS):
        hs = slice(h * HEAD_W, (h + 1) * HEAD_W)
        qh = q_ref[0, :, hs]
        zero = jnp.zeros_like(qh)
        qc = (jnp.where(lane < DIFF_D, qh, zero), jnp.where(lane >= DIFF_D, qh, zero))
        kp = cache_buf[slot, 0, h].astype(BF16)
        vp = cache_buf[slot, 1, h].astype(BF16)
        kn = _head(nk_ref, h).astype(BF16)
        vn = _head(nv_ref, h).astype(BF16)
        bias_p = -_slope2(h) * dist_p
        bias_n = jnp.where(vis_n, -_slope2(h) * dist_n, NEG_BIG)
        o = []
        for c in range(2):
            sp = _dot_nt(qc[c], kp) + bias_p
            sn = _dot_nt(qc[c], kn) + bias_n
            m = jnp.maximum(jnp.max(sp, axis=-1, keepdims=True), jnp.max(sn, axis=-1, keepdims=True))
            ep = jnp.exp2(sp - m)
            en = jnp.exp2(sn - m)
            l = jnp.sum(ep, axis=-1, keepdims=True) + jnp.sum(en, axis=-1, keepdims=True)
            o.append((_dot(ep.astype(BF16), vp) + _dot(en.astype(BF16), vn)) / l)
        out_ref[0, :, hs] = _finish_head(o[0], o[1], lam, gs_ref[...], lam_init).astype(out_ref.dtype)
    _mem_attend(qm_ref[0], mk_ref, mv_ref, out_ref)


def _diff_sample(z, cache_k, cache_v, k_new, v_new, mem_k, mem_v, lq, g_sub, lam_init):
    B, Ts, _ = z.shape
    P0 = cache_k.shape[1]
    return pl.pallas_call(
        functools.partial(_diff_sample_kernel, p0=P0, lam_init=lam_init),
        grid=(B,),
        in_specs=[_resident(lq.shape, lambda b: (0, 0)),
                  _resident((1, HEAD_W), lambda b: (0, 0)),
                  pl.BlockSpec((1, Ts, MIX_W), lambda b: (b, 0, 0)),
                  pl.BlockSpec((1, Ts, MIX_W), lambda b: (b, 0, 1)),
                  pl.BlockSpec(memory_space=pl.ANY), pl.BlockSpec(memory_space=pl.ANY),
                  _batch_spec(k_new), _batch_spec(v_new), _batch_spec(mem_k), _batch_spec(mem_v)],
        out_specs=pl.BlockSpec((1, Ts, 2 * MIX_W), lambda b: (b, 0, 0)),
        out_shape=jax.ShapeDtypeStruct((B, Ts, 2 * MIX_W), BF16),
        scratch_shapes=[pltpu.VMEM((2, 2, N_HEADS, P0, HEAD_W), F32), pltpu.SemaphoreType.DMA((2,))],
        compiler_params=_cparams("arbitrary"),
        name="diff_sample",
    )(lq, g_sub.reshape(1, HEAD_W), z, z, cache_k, cache_v, k_new, v_new, mem_k, mem_v)


def _lambda_init(layer_idx):
    return 0.8 - 0.6 * math.exp(-0.3 * layer_idx)


def _trunk(x, pos0, pool_hist, mem_k, mem_v, past_k, past_v, w, tm_rows, tm_seq, tq):
    B, T, D = x.shape
    R = B * T
    xr = x.reshape(R, D)
    hist16 = jnp.pad(pool_hist, ((0, 0), (HALO - POOL_HIST, 0), (0, 0)))

    mix, qm = _proj0(xr, w["g_attn"][0], w["w_in"][0], tm_rows)
    mix3 = mix.reshape(B, T, MIX_W)
    cat0 = _mix0(mix3, hist16, qm.reshape(B, T, MIX_W), mem_k[0], mem_v[0],
                 w["w_pool"], w["pool_scale"], tm_seq, pos0)
    new_pool = jnp.concatenate([pool_hist, mix3], axis=1)[:, -POOL_HIST:]
    x1 = _outmlp(xr, cat0.reshape(R, D), w["w_out"][0], w["g_ffn"][0], w["w_ff1"][0], w["w_ff2"][0],
                 w["g_final"], tm_rows, final_norm=False)

    prompt = past_k is None
    z, *bf, k, v = _proj1(x1, w["g_attn"][1], w["w_in"][1], w["g_kv"], w["w_kv"], tm_rows, T, prompt)
    z3 = z.reshape(B, T, D)
    lam_init = _lambda_init(1)
    if prompt:
        k1a, k2a, vb = (a.reshape(B, T, MIX_W) for a in bf)
        cat1 = _diff_prompt(z3, k1a, k2a, vb, mem_k[1], mem_v[1], w["lambda_qk"], w["g_subln"], lam_init, tq)
    else:
        shp = (B, T, N_HEADS, HEAD_W)
        cat1 = _diff_sample(z3, past_k, past_v, k.reshape(shp), v.reshape(shp),
                            mem_k[1], mem_v[1], w["lambda_qk"], w["g_subln"], lam_init)
    y = _outmlp(x1, cat1.reshape(R, D), w["w_out"][1], w["g_ffn"][1], w["w_ff1"][1], w["w_ff2"][1],
                w["g_final"], tm_rows, final_norm=True)
    shp = (B, T, N_HEADS, HEAD_W)
    return y.reshape(B, T, D), new_pool[None], k.reshape(shp), v.reshape(shp)


def kernel(x_prompt, x_sample, mem_prompt, cache_k, cache_v, cache_mem_k, cache_mem_v, state_pool, g_attn, w_in, w_out, g_mem, w_mem_kv, g_ffn, w_ff1, w_ff2, w_pool, pool_scale, lambda_qk, g_subln, g_kv, w_kv, g_final):
    B, T, D = x_prompt.shape
    Bs, Ts, _ = x_sample.shape
    P0 = cache_k.shape[1]
    w = dict(g_attn=g_attn, w_in=w_in.astype(BF16), w_out=w_out.astype(BF16), g_ffn=g_ffn,
             w_ff1=w_ff1.astype(BF16), w_ff2=w_ff2.astype(BF16), w_pool=w_pool[0].astype(BF16),
             pool_scale=pool_scale[0], lambda_qk=lambda_qk[0], g_subln=g_subln[0], g_kv=g_kv,
             w_kv=w_kv.astype(BF16), g_final=g_final)

    mem_k_p, mem_v_p, mem_k_d, mem_v_d = _mem_kv(mem_prompt, g_mem, w_mem_kv.astype(BF16))
    hist0 = jnp.zeros((B, POOL_HIST, MIX_W), F32)
    y_p, pool_p, k_p, v_p = _trunk(x_prompt, 0, hist0, mem_k_d, mem_v_d, None, None, w,
                                   tm_rows=512, tm_seq=512, tq=512)
    y_s, pool_s, k_s, v_s = _trunk(x_sample, P0, state_pool[0], cache_mem_k, cache_mem_v, cache_k, cache_v, w,
                                   tm_rows=Bs * Ts, tm_seq=Ts, tq=None)
    return (y_p, y_s, mem_k_p, mem_v_p, pool_p, k_p, v_p, pool_s, k_s, v_s)
```

```python
import functools
import math

import jax
import jax.numpy as jnp
import ml_dtypes
import numpy as np
from jax import lax
from jax.experimental import pallas as pl
from jax.experimental.pallas import tpu as pltpu

F32 = jnp.float32
BF16 = jnp.bfloat16

CHUNK = 64
POOL_WINDOWS = (2, 4, 8, 16)
POOL_HIST = 15
HALO = 16
N_HEADS = 4
HEAD_W = 128
DIFF_D = 64
MIX_W = N_HEADS * HEAD_W
EPS = 1e-6
NEG_BIG = -1e30
LOG2E = math.log2(math.e)
Q_SCALE = DIFF_D ** -0.5 * LOG2E
N_SPLIT = 4
VMEM_LIMIT = 56 * 1024 * 1024


def _cparams(*sem):
    return pltpu.CompilerParams(dimension_semantics=sem, vmem_limit_bytes=VMEM_LIMIT)


def _resident(shape, index_map):
    return pl.BlockSpec(shape, index_map, pipeline_mode=pl.Buffered(1))


def _inv_rms(x):
    return lax.rsqrt(jnp.mean(x * x, axis=-1, keepdims=True) + EPS)


def _dot(a, b):
    return jnp.dot(a, b, preferred_element_type=F32)


def _dot_nt(a, b):
    return lax.dot_general(a, b, (((1,), (1,)), ((), ())), preferred_element_type=F32)


def _memkv_kernel(x_ref, g_ref, w_ref, k_ref, v_ref, kd_ref, vd_ref):
    x = x_ref[0]
    h = (x * _inv_rms(x) * g_ref[0]).astype(BF16)
    kv = _dot(h, w_ref[0])
    kd_ref[0, 0] = kv[:, :MIX_W]
    vd_ref[0, 0] = kv[:, MIX_W:]
    for hd in range(N_HEADS):
        k_ref[0, 0, :, hd, :] = kv[:, hd * HEAD_W:(hd + 1) * HEAD_W]
        v_ref[0, 0, :, hd, :] = kv[:, MIX_W + hd * HEAD_W:MIX_W + (hd + 1) * HEAD_W]


def _mem_kv(mem, g_mem, w_mem_kv):
    B, M, D = mem.shape
    L = g_mem.shape[0]
    out = jax.ShapeDtypeStruct((L, B, M, N_HEADS, HEAD_W), F32)
    dense = jax.ShapeDtypeStruct((L, B, M, MIX_W), F32)
    return pl.pallas_call(
        _memkv_kernel,
        grid=(L, B),
        in_specs=[pl.BlockSpec((1, M, D), lambda l, b: (b, 0, 0)),
                  pl.BlockSpec((1, 1, D), lambda l, b: (l, 0, 0)),
                  pl.BlockSpec((1, D, 2 * MIX_W), lambda l, b: (l, 0, 0))],
        out_specs=[pl.BlockSpec((1, 1, M, N_HEADS, HEAD_W), lambda l, b: (l, b, 0, 0, 0))] * 2
                  + [pl.BlockSpec((1, 1, M, MIX_W), lambda l, b: (l, b, 0, 0))] * 2,
        out_shape=[out, out, dense, dense],
        compiler_params=_cparams("arbitrary", "arbitrary"),
        name="mem_kv",
    )(mem, g_mem.reshape(L, 1, D), w_mem_kv)


def _proj0_kernel(x_ref, g_ref, w_ref, mix_ref, qm_ref):
    x = x_ref[...]
    h = (x * _inv_rms(x) * g_ref[...]).astype(BF16)
    z = _dot(h, w_ref[...])
    mix_ref[...] = z[:, :MIX_W]
    qm_ref[...] = z[:, MIX_W:].astype(BF16)


def _proj0(x, g, w, tm):
    R, D = x.shape
    return pl.pallas_call(
        _proj0_kernel,
        grid=(R // tm,),
        in_specs=[pl.BlockSpec((tm, D), lambda i: (i, 0)),
                  _resident((1, D), lambda i: (0, 0)),
                  _resident((D, 2 * MIX_W), lambda i: (0, 0))],
        out_specs=[pl.BlockSpec((tm, MIX_W), lambda i: (i, 0)),
                   pl.BlockSpec((tm, MIX_W), lambda i: (i, 0))],
        out_shape=[jax.ShapeDtypeStruct((R, MIX_W), F32),
                   jax.ShapeDtypeStruct((R, MIX_W), BF16)],
        compiler_params=_cparams("arbitrary"),
        name="proj0",
    )(x, g.reshape(1, D), w)


def _batch_spec(arr):
    zeros = (0,) * (arr.ndim - 1)
    return pl.BlockSpec((1,) + arr.shape[1:], lambda b, *_: (b,) + zeros)


def _head(ref, h):
    if len(ref.shape) == 3:
        return ref[0, :, h * HEAD_W:(h + 1) * HEAD_W]
    return ref[0, :, h, :]


def _mem_attend(qm, mk_ref, mv_ref, out_ref):
    scale = HEAD_W ** -0.5
    for h in range(N_HEADS):
        hs = slice(h * HEAD_W, (h + 1) * HEAD_W)
        kh = _head(mk_ref, h).astype(BF16)
        vh = _head(mv_ref, h).astype(BF16)
        s = _dot_nt(qm[:, hs], kh) * scale
        m = jnp.max(s, axis=-1, keepdims=True)
        e = jnp.exp(s - m)
        l = jnp.sum(e, axis=-1, keepdims=True)
        o = _dot(e.astype(BF16), vh) / l
        out_ref[0, :, MIX_W + h * HEAD_W:MIX_W + (h + 1) * HEAD_W] = o.astype(out_ref.dtype)


def _mix0_kernel(mix_ref, halo_ref, hist_ref, qm_ref, wp_ref, ps_ref, mk_ref, mv_ref, out_ref, *, tm, pos0):
    i = pl.program_id(1)
    u = mix_ref[0]
    halo = jnp.where(i == 0, hist_ref[0], halo_ref[0])
    ext = jnp.concatenate([halo, u], axis=0)
    pos = pos0 + i * tm + lax.broadcasted_iota(jnp.int32, (tm, HEAD_W), 0)
    for g, w in enumerate(POOL_WINDOWS):
        gs = slice(g * HEAD_W, (g + 1) * HEAD_W)
        acc = ext[:, gs]
        span = 1
        while span < w:
            acc = acc + pltpu.roll(acc, span, 0)
            span *= 2
        cnt = jnp.minimum(pos + 1, w).astype(F32)
        d = acc[HALO:] / cnt - u[:, gs]
        t = _dot(d.astype(BF16), wp_ref[g]) * ps_ref[:, gs]
        out_ref[0, :, gs] = t.astype(out_ref.dtype)
    _mem_attend(qm_ref[0], mk_ref, mv_ref, out_ref)


def _mix0(mix, hist16, qm, mem_k, mem_v, w_pool, pool_scale, tm, pos0):
    B, T, _ = mix.shape
    hb = tm // HALO
    return pl.pallas_call(
        functools.partial(_mix0_kernel, tm=tm, pos0=pos0),
        grid=(B, T // tm),
        in_specs=[pl.BlockSpec((1, tm, MIX_W), lambda b, i: (b, i, 0)),
                  pl.BlockSpec((1, HALO, MIX_W), lambda b, i: (b, jnp.maximum(i * hb - 1, 0), 0)),
                  pl.BlockSpec((1, HALO, MIX_W), lambda b, i: (b, 0, 0)),
                  pl.BlockSpec((1, tm, MIX_W), lambda b, i: (b, i, 0)),
                  _resident(w_pool.shape, lambda b, i: (0, 0, 0)),
                  _resident((1, MIX_W), lambda b, i: (0, 0)),
                  _batch_spec(mem_k), _batch_spec(mem_v)],
        out_specs=pl.BlockSpec((1, tm, 2 * MIX_W), lambda b, i: (b, i, 0)),
        out_shape=jax.ShapeDtypeStruct((B, T, 2 * MIX_W), BF16),
        compiler_params=_cparams("arbitrary", "arbitrary"),
        name="mix0",
    )(mix, mix, hist16, qm, w_pool, pool_scale.reshape(1, MIX_W), mem_k, mem_v)


def _outmlp_kernel(x_ref, cat_ref, wo_ref, g_ref, w1_ref, w2_ref, gf_ref, out_ref, a_ref, *, final_norm, ff_chunk):
    x1 = x_ref[...] + _dot(cat_ref[...], wo_ref[...])
    h = (x1 * _inv_rms(x1) * g_ref[...]).astype(BF16)
    for c in range(w1_ref.shape[1] // ff_chunk):
        cs = slice(c * ff_chunk, (c + 1) * ff_chunk)
        a = jnp.maximum(_dot(h, w1_ref[:, cs]), 0.0)
        a_ref[:, cs] = (a * a).astype(BF16)
    x2 = x1 + _dot(a_ref[...], w2_ref[...])
    if final_norm:
        x2 = x2 * _inv_rms(x2) * gf_ref[...]
    out_ref[...] = x2


def _outmlp(x, cat, w_out, g_ffn, w1, w2, g_final, tm, final_norm):
    R, D = x.shape
    FF = w1.shape[1]
    return pl.pallas_call(
        functools.partial(_outmlp_kernel, final_norm=final_norm, ff_chunk=1024),
        grid=(R // tm,),
        in_specs=[pl.BlockSpec((tm, D), lambda i: (i, 0)),
                  pl.BlockSpec((tm, D), lambda i: (i, 0)),
                  _resident((D, D), lambda i: (0, 0)),
                  _resident((1, D), lambda i: (0, 0)),
                  _resident((D, FF), lambda i: (0, 0)),
                  _resident((FF, D), lambda i: (0, 0)),
                  _resident((1, D), lambda i: (0, 0))],
        out_specs=pl.BlockSpec((tm, D), lambda i: (i, 0)),
        out_shape=jax.ShapeDtypeStruct((R, D), F32),
        scratch_shapes=[pltpu.VMEM((tm, FF), BF16)],
        compiler_params=_cparams("arbitrary"),
        name="outmlp",
    )(x, cat, w_out, g_ffn.reshape(1, D), w1, w2, g_final.reshape(1, D))


def _lane_window(lane, lo):
    return (lane >= lo) & (lane < lo + N_SPLIT)


def _head_copies(buf, slot, hbm_refs, row0, tm, sem):
    out = []
    for t, hbm in enumerate(hbm_refs):
        for h in range(N_HEADS):
            src = buf.at[slot, :, pl.ds(t * MIX_W + h * HEAD_W, HEAD_W)]
            out.append(pltpu.make_async_copy(src, hbm.at[pl.ds(row0, tm), h, :], sem.at[slot]))
    return out


def _proj1_kernel(x_ref, ga_ref, wi_ref, gk_ref, wk_ref, z_ref, *rest, seq_len, n_bf):
    bf_refs = rest[:n_bf]
    k_hbm, v_hbm, kv_buf, sem = rest[n_bf:]
    tm = x_ref.shape[0]
    i = pl.program_id(0)
    n_steps = pl.num_programs(0)
    slot = i % 2
    copies = functools.partial(_head_copies, kv_buf, hbm_refs=(k_hbm, v_hbm), tm=tm, sem=sem)

    x = x_ref[...]
    xn = x * _inv_rms(x)
    z = _dot((xn * ga_ref[...]).astype(BF16), wi_ref[...])
    z_ref[:, :MIX_W] = (z[:, :MIX_W] * Q_SCALE).astype(BF16)
    z_ref[:, MIX_W:] = z[:, MIX_W:].astype(BF16)
    kv = _dot((xn * gk_ref[...]).astype(BF16), wk_ref[...])

    @pl.when(i >= 2)
    def _():
        for c in copies(slot, row0=(i - 2) * tm):
            c.wait()

    kv_buf[slot] = kv
    for c in copies(slot, row0=i * tm):
        c.start()

    @pl.when(i == n_steps - 1)
    def _():
        @pl.when(i >= 1)
        def _():
            for c in copies(1 - slot, row0=(i - 1) * tm):
                c.wait()
        for c in copies(slot, row0=i * tm):
            c.wait()

    if n_bf:
        k = kv[:, :MIX_W]
        v = kv[:, MIX_W:]
        k1a_ref, k2a_ref, vb_ref = bf_refs
        lane = lax.broadcasted_iota(jnp.int32, (tm, HEAD_W), 1)
        pos = (i * tm) % seq_len + lax.broadcasted_iota(jnp.int32, (tm, HEAD_W), 0)
        off = pos % CHUNK
        base = (pos - off).astype(F32)
        off = off.astype(F32)

        def pos_lanes(lo):
            return jnp.where(_lane_window(lane, lo), base, jnp.where(_lane_window(lane, lo + N_SPLIT), off, 0.0))

        pos_hi = pos_lanes(DIFF_D)
        pos_lo = pos_lanes(0)
        for h in range(N_HEADS):
            hs = slice(h * HEAD_W, (h + 1) * HEAD_W)
            k1a_ref[:, hs] = jnp.where(lane < DIFF_D, k[:, hs], pos_hi).astype(BF16)
            k2a_ref[:, hs] = jnp.where(lane >= DIFF_D, k[:, hs], pos_lo).astype(BF16)
        vb_ref[...] = v.astype(BF16)


def _proj1(x, g_attn, w_in, g_kv, w_kv, tm, seq_len, attn_operands):
    R, D = x.shape
    row = lambda i: (i, 0)
    fix = lambda i: (0, 0)
    n_bf = 3 if attn_operands else 0
    return pl.pallas_call(
        functools.partial(_proj1_kernel, seq_len=seq_len, n_bf=n_bf),
        grid=(R // tm,),
        in_specs=[pl.BlockSpec((tm, D), row),
                  _resident((1, D), fix), _resident((D, 2 * MIX_W), fix),
                  _resident((1, D), fix), _resident((D, 2 * MIX_W), fix)],
        out_specs=[pl.BlockSpec((tm, 2 * MIX_W), row)]
                  + [pl.BlockSpec((tm, MIX_W), row)] * n_bf
                  + [pl.BlockSpec(memory_space=pl.ANY)] * 2,
        out_shape=[jax.ShapeDtypeStruct((R, 2 * MIX_W), BF16)]
                  + [jax.ShapeDtypeStruct((R, MIX_W), BF16)] * n_bf
                  + [jax.ShapeDtypeStruct((R, N_HEADS, HEAD_W), F32)] * 2,
        scratch_shapes=[pltpu.VMEM((2, tm, 2 * MIX_W), F32), pltpu.SemaphoreType.DMA((2,))],
        compiler_params=_cparams("arbitrary"),
        name="proj1",
    )(x, g_attn.reshape(1, D), w_in, g_kv.reshape(1, D), w_kv)


def _lambda(lq, lam_init):
    a = jnp.sum(lq[0:1] * lq[1:2], axis=-1, keepdims=True)
    b = jnp.sum(lq[2:3] * lq[3:4], axis=-1, keepdims=True)
    return jnp.exp(a) - jnp.exp(b) + lam_init


def _slope2(h):
    return 2.0 ** (-8.0 * (h + 1) / N_HEADS) * LOG2E


def _bf16_pieces(c):
    out = []
    for _ in range(N_SPLIT):
        p = float(np.asarray(c, dtype=ml_dtypes.bfloat16))
        out.append(p)
        c -= p
    return out


def _slope_lanes(h, lo, shape):
    lane = lax.broadcasted_iota(jnp.int32, shape, 1)
    out = jnp.zeros(shape, F32)
    for p, piece in enumerate(_bf16_pieces(_slope2(h))):
        out = jnp.where((lane == lo + p) | (lane == lo + N_SPLIT + p), piece, out)
    return out.astype(BF16)


def _finish_head(o1, o2, lam, gs, lam_init):
    o = o1 - lam * o2
    return o * _inv_rms(o) * gs * (1.0 - lam_init)


def _diff_kernel(lq_ref, gs_ref, q_ref, qm_ref, k1_ref, k2_ref, v_ref, mk_ref, mv_ref, out_ref,
                 qa_ref, acc_ref, m_ref, s_ref, *, tq, lam_init):
    i = pl.program_id(1)
    nblk = tq // HEAD_W
    n_chain = 2 * N_HEADS
    k_refs = (k1_ref, k2_ref)
    row = lax.broadcasted_iota(jnp.int32, (tq, tq), 0)
    col = lax.broadcasted_iota(jnp.int32, (tq, tq), 1)
    ahead = jnp.maximum(col - row, 0).astype(F32)
    vis = (col // CHUNK) <= (row // CHUNK)
    ones = jnp.ones((tq, HEAD_W), BF16)
    lane = lax.broadcasted_iota(jnp.int32, (tq, HEAD_W), 1)
    q0 = pl.multiple_of(i * tq, tq)

    def hslice(n):
        return slice((n // 2) * HEAD_W, (n // 2 + 1) * HEAD_W)

    def scores(n, k0):
        return _dot_nt(qa_ref[n], k_refs[n % 2][0, pl.ds(k0, tq), hslice(n)])

    def v_aug(n, k0):
        return jnp.concatenate([v_ref[0, pl.ds(k0, tq), hslice(n)], ones], axis=1)

    def consume(n, k0, first):
        vt = v_aug(n, k0)
        blocks = [s_ref[n, :, b * HEAD_W:(b + 1) * HEAD_W] for b in range(nblk)]
        m_row = jnp.max(functools.reduce(jnp.maximum, blocks), axis=-1, keepdims=True)
        if first:
            m_new = jnp.broadcast_to(m_row, (tq, HEAD_W))
        else:
            m_old = m_ref[n]
            m_new = jnp.maximum(m_old, m_row)
        p = jnp.concatenate([jnp.exp2(b - m_new) for b in blocks], axis=1).astype(BF16)
        pv = _dot(p, vt)
        if first:
            acc_ref[n] = pv
        else:
            alpha = jnp.exp2(m_old - m_new)
            acc_ref[n] = jnp.concatenate([alpha, alpha], axis=1) * acc_ref[n] + pv
        m_ref[n] = m_new

    for h in range(N_HEADS):
        qh = q_ref[0, :, h * HEAD_W:(h + 1) * HEAD_W]
        qa_ref[2 * h] = jnp.where(lane < DIFF_D, qh, _slope_lanes(h, DIFF_D, qh.shape))
        qa_ref[2 * h + 1] = jnp.where(lane >= DIFF_D, qh, _slope_lanes(h, 0, qh.shape))
        fix = jnp.where(vis, (-2.0 * _slope2(h)) * ahead, NEG_BIG)
        for n in (2 * h, 2 * h + 1):
            s_ref[n] = scores(n, q0) + fix

    @pl.when(i == 0)
    def _():
        for n in range(n_chain):
            consume(n, q0, True)

    @pl.when(i > 0)
    def _():
        for n in range(n_chain):
            consume(n, q0, True)
            s_ref[n] = scores(n, 0)

        def body(j, carry):
            k_cur = pl.multiple_of((j - 1) * tq, tq)
            k_next = pl.multiple_of(j * tq, tq)
            for n in range(n_chain):
                consume(n, k_cur, False)
                s_ref[n] = scores(n, k_next)
            return carry

        lax.fori_loop(1, i, body, 0)
        k_last = pl.multiple_of((i - 1) * tq, tq)
        for n in range(n_chain):
            consume(n, k_last, False)

    lam = _lambda(lq_ref[...], lam_init)
    for h in range(N_HEADS):
        o = [acc_ref[n, :, :HEAD_W] / acc_ref[n, :, HEAD_W:] for n in (2 * h, 2 * h + 1)]
        out_ref[0, :, h * HEAD_W:(h + 1) * HEAD_W] = _finish_head(
            o[0], o[1], lam, gs_ref[...], lam_init).astype(out_ref.dtype)
    _mem_attend(qm_ref[0], mk_ref, mv_ref, out_ref)


def _diff_prompt(z, k1a, k2a, vb, mem_k, mem_v, lq, g_sub, lam_init, tq):
    B, T, _ = z.shape
    n_chain = 2 * N_HEADS
    return pl.pallas_call(
        functools.partial(_diff_kernel, tq=tq, lam_init=lam_init),
        grid=(B, T // tq),
        in_specs=[_resident(lq.shape, lambda b, i: (0, 0)),
                  _resident((1, HEAD_W), lambda b, i: (0, 0)),
                  pl.BlockSpec((1, tq, MIX_W), lambda b, i: (b, i, 0)),
                  pl.BlockSpec((1, tq, MIX_W), lambda b, i: (b, i, 1)),
                  _resident((1, T, MIX_W), lambda b, i: (b, 0, 0)),
                  _resident((1, T, MIX_W), lambda b, i: (b, 0, 0)),
                  _resident((1, T, MIX_W), lambda b, i: (b, 0, 0)),
                  _batch_spec(mem_k), _batch_spec(mem_v)],
        out_specs=pl.BlockSpec((1, tq, 2 * MIX_W), lambda b, i: (b, i, 0)),
        out_shape=jax.ShapeDtypeStruct((B, T, 2 * MIX_W), BF16),
        scratch_shapes=[pltpu.VMEM((n_chain, tq, HEAD_W), BF16),
                        pltpu.VMEM((n_chain, tq, 2 * HEAD_W), F32),
                        pltpu.VMEM((n_chain, tq, HEAD_W), F32),
                        pltpu.VMEM((n_chain, tq, tq), F32)],
        compiler_params=_cparams("arbitrary", "arbitrary"),
        name="diff_prompt",
    )(lq, g_sub.reshape(1, HEAD_W), z, z, k1a, k2a, vb, mem_k, mem_v)


def _cache_copies(hbm_refs, b, buf, slot, sem):
    return [pltpu.make_async_copy(hbm.at[b, :, h, :], buf.at[slot, t, h], sem.at[slot])
            for t, hbm in enumerate(hbm_refs) for h in range(N_HEADS)]


def _diff_sample_kernel(lq_ref, gs_ref, q_ref, qm_ref, ck_hbm, cv_hbm, nk_ref, nv_ref, mk_ref, mv_ref, out_ref,
                        cache_buf, sem, *, p0, lam_init):
    b = pl.program_id(0)
    slot = b % 2
    copies = functools.partial(_cache_copies, (ck_hbm, cv_hbm), buf=cache_buf, sem=sem)

    @pl.when(b == 0)
    def _():
        for c in copies(b, slot=slot):
            c.start()

    @pl.when(b + 1 < pl.num_programs(0))
    def _():
        for c in copies(b + 1, slot=1 - slot):
            c.start()

    for c in copies(b, slot=slot):
        c.wait()

    ts = q_ref.shape[1]
    lam = _lambda(lq_ref[...], lam_init)
    qpos_p = p0 + lax.broadcasted_iota(jnp.int32, (ts, p0), 0)
    dist_p = (qpos_p - lax.broadcasted_iota(jnp.int32, (ts, p0), 1)).astype(F32)
    row = p0 + lax.broadcasted_iota(jnp.int32, (ts, ts), 0)
    col = p0 + lax.broadcasted_iota(jnp.int32, (ts, ts), 1)
    dist_n = jnp.abs(row - col).astype(F32)
    vis_n = (col // CHUNK) <= (row // CHUNK)
    lane = lax.broadcasted_iota(jnp.int32, (ts, HEAD_W), 1)
    for h in range(N_HEADS):
        hs = slice(h * HEAD_W, (h + 1) * HEAD_W)
        qh = q_ref[0, :, hs]
        zero = jnp.zeros_like(qh)
        qc = (jnp.where(lane < DIFF_D, qh, zero), jnp.where(lane >= DIFF_D, qh, zero))
        kp = cache_buf[slot, 0, h].astype(BF16)
        vp = cache_buf[slot, 1, h].astype(BF16)
        kn = _head(nk_ref, h).astype(BF16)
        vn = _head(nv_ref, h).astype(BF16)
        bias_p = -_slope2(h) * dist_p
        bias_n = jnp.where(vis_n, -_slope2(h) * dist_n, NEG_BIG)
        o = []
        for c in range(2):
            sp = _dot_nt(qc[c], kp) + bias_p
            sn = _dot_nt(qc[c], kn) + bias_n
            m = jnp.maximum(jnp.max(sp, axis=-1, keepdims=True), jnp.max(sn, axis=-1, keepdims=True))
            ep = jnp.exp2(sp - m)
            en = jnp.exp2(sn - m)
            l = jnp.sum(ep, axis=-1, keepdims=True) + jnp.sum(en, axis=-1, keepdims=True)
            o.append((_dot(ep.astype(BF16), vp) + _dot(en.astype(BF16), vn)) / l)
        out_ref[0, :, hs] = _finish_head(o[0], o[1], lam, gs_ref[...], lam_init).astype(out_ref.dtype)
    _mem_attend(qm_ref[0], mk_ref, mv_ref, out_ref)


def _diff_sample(z, cache_k, cache_v, k_new, v_new, mem_k, mem_v, lq, g_sub, lam_init):
    B, Ts, _ = z.shape
    P0 = cache_k.shape[1]
    return pl.pallas_call(
        functools.partial(_diff_sample_kernel, p0=P0, lam_init=lam_init),
        grid=(B,),
        in_specs=[_resident(lq.shape, lambda b: (0, 0)),
                  _resident((1, HEAD_W), lambda b: (0, 0)),
                  pl.BlockSpec((1, Ts, MIX_W), lambda b: (b, 0, 0)),
                  pl.BlockSpec((1, Ts, MIX_W), lambda b: (b, 0, 1)),
                  pl.BlockSpec(memory_space=pl.ANY), pl.BlockSpec(memory_space=pl.ANY),
                  _batch_spec(k_new), _batch_spec(v_new), _batch_spec(mem_k), _batch_spec(mem_v)],
        out_specs=pl.BlockSpec((1, Ts, 2 * MIX_W), lambda b: (b, 0, 0)),
        out_shape=jax.ShapeDtypeStruct((B, Ts, 2 * MIX_W), BF16),
        scratch_shapes=[pltpu.VMEM((2, 2, N_HEADS, P0, HEAD_W), F32), pltpu.SemaphoreType.DMA((2,))],
        compiler_params=_cparams("arbitrary"),
        name="diff_sample",
    )(lq, g_sub.reshape(1, HEAD_W), z, z, cache_k, cache_v, k_new, v_new, mem_k, mem_v)


def _lambda_init(layer_idx):
    return 0.8 - 0.6 * math.exp(-0.3 * layer_idx)


def _trunk(x, pos0, pool_hist, mem_k, mem_v, past_k, past_v, w, tm_rows, tm_seq, tq):
    B, T, D = x.shape
    R = B * T
    xr = x.reshape(R, D)
    hist16 = jnp.pad(pool_hist, ((0, 0), (HALO - POOL_HIST, 0), (0, 0)))

    mix, qm = _proj0(xr, w["g_attn"][0], w["w_in"][0], tm_rows)
    mix3 = mix.reshape(B, T, MIX_W)
    cat0 = _mix0(mix3, hist16, qm.reshape(B, T, MIX_W), mem_k[0], mem_v[0],
                 w["w_pool"], w["pool_scale"], tm_seq, pos0)
    new_pool = jnp.concatenate([pool_hist, mix3], axis=1)[:, -POOL_HIST:]
    x1 = _outmlp(xr, cat0.reshape(R, D), w["w_out"][0], w["g_ffn"][0], w["w_ff1"][0], w["w_ff2"][0],
                 w["g_final"], tm_rows, final_norm=False)

    prompt = past_k is None
    z, *bf, k, v = _proj1(x1, w["g_attn"][1], w["w_in"][1], w["g_kv"], w["w_kv"], tm_rows, T, prompt)
    z3 = z.reshape(B, T, D)
    lam_init = _lambda_init(1)
    if prompt:
        k1a, k2a, vb = (a.reshape(B, T, MIX_W) for a in bf)
        cat1 = _diff_prompt(z3, k1a, k2a, vb, mem_k[1], mem_v[1], w["lambda_qk"], w["g_subln"], lam_init, tq)
    else:
        shp = (B, T, N_HEADS, HEAD_W)
        cat1 = _diff_sample(z3, past_k, past_v, k.reshape(shp), v.reshape(shp),
                            mem_k[1], mem_v[1], w["lambda_qk"], w["g_subln"], lam_init)
    y = _outmlp(x1, cat1.reshape(R, D), w["w_out"][1], w["g_ffn"][1], w["w_ff1"][1], w["w_ff2"][1],
                w["g_final"], tm_rows, final_norm=True)
    shp = (B, T, N_HEADS, HEAD_W)
    return y.reshape(B, T, D), new_pool[None], k.reshape(shp), v.reshape(shp)


def kernel(x_prompt, x_sample, mem_prompt, cache_k, cache_v, cache_mem_k, cache_mem_v, state_pool, g_attn, w_in, w_out, g_mem, w_mem_kv, g_ffn, w_ff1, w_ff2, w_pool, pool_scale, lambda_qk, g_subln, g_kv, w_kv, g_final):
    B, T, D = x_prompt.shape
    Bs, Ts, _ = x_sample.shape
    P0 = cache_k.shape[1]
    w = dict(g_attn=g_attn, w_in=w_in.astype(BF16), w_out=w_out.astype(BF16), g_ffn=g_ffn,
             w_ff1=w_ff1.astype(BF16), w_ff2=w_ff2.astype(BF16), w_pool=w_pool[0].astype(BF16),
             pool_scale=pool_scale[0], lambda_qk=lambda_qk[0], g_subln=g_subln[0], g_kv=g_kv,
             w_kv=w_kv.astype(BF16), g_final=g_final)

    mem_k_p, mem_v_p, mem_k_d, mem_v_d = _mem_kv(mem_prompt, g_mem, w_mem_kv.astype(BF16))
    hist0 = jnp.zeros((B, POOL_HIST, MIX_W), F32)
    y_p, pool_p, k_p, v_p = _trunk(x_prompt, 0, hist0, mem_k_d, mem_v_d, None, None, w,
                                   tm_rows=512, tm_seq=512, tq=512)
    y_s, pool_s, k_s, v_s = _trunk(x_sample, P0, state_pool[0], cache_mem_k, cache_mem_v, cache_k, cache_v, w,
                                   tm_rows=Bs * Ts, tm_seq=Ts, tq=None)
    return (y_p, y_s, mem_k_p, mem_v_p, pool_p, k_p, v_p, pool_s, k_s, v_s)
```

```python
import functools
import math

import jax
import jax.numpy as jnp
import ml_dtypes
import numpy as np
from jax import lax
from jax.experimental import pallas as pl
from jax.experimental.pallas import tpu as pltpu

F32 = jnp.float32
BF16 = jnp.bfloat16

CHUNK = 64
POOL_WINDOWS = (2, 4, 8, 16)
POOL_HIST = 15
HALO = 16
N_HEADS = 4
HEAD_W = 128
DIFF_D = 64
MIX_W = N_HEADS * HEAD_W
EPS = 1e-6
NEG_BIG = -1e30
LOG2E = math.log2(math.e)
Q_SCALE = DIFF_D ** -0.5 * LOG2E
N_SPLIT = 4
VMEM_LIMIT = 56 * 1024 * 1024


def _cparams(*sem):
    return pltpu.CompilerParams(dimension_semantics=sem, vmem_limit_bytes=VMEM_LIMIT)


def _resident(shape, index_map):
    return pl.BlockSpec(shape, index_map, pipeline_mode=pl.Buffered(1))


def _layer(arr, layer):
    zeros = (0,) * (arr.ndim - 1)
    return pl.BlockSpec((None,) + arr.shape[1:], lambda *_: (layer,) + zeros, pipeline_mode=pl.Buffered(1))


def _rows(g):
    return g.reshape(g.shape[0], 1, g.shape[1])


def _inv_rms(x):
    return lax.rsqrt(jnp.mean(x * x, axis=-1, keepdims=True) + EPS)


def _dot(a, b):
    return jnp.dot(a, b, preferred_element_type=F32)


def _dot_nt(a, b):
    return lax.dot_general(a, b, (((1,), (1,)), ((), ())), preferred_element_type=F32)


def _memkv_kernel(x_ref, g_ref, w_ref, k_ref, v_ref, kd_ref, vd_ref):
    x = x_ref[0]
    h = (x * _inv_rms(x) * g_ref[0]).astype(BF16)
    kv = _dot(h, w_ref[0])
    kd_ref[0, 0] = kv[:, :MIX_W]
    vd_ref[0, 0] = kv[:, MIX_W:]
    for hd in range(N_HEADS):
        k_ref[0, 0, :, hd, :] = kv[:, hd * HEAD_W:(hd + 1) * HEAD_W]
        v_ref[0, 0, :, hd, :] = kv[:, MIX_W + hd * HEAD_W:MIX_W + (hd + 1) * HEAD_W]


def _mem_kv(mem, g_mem, w_mem_kv):
    B, M, D = mem.shape
    L = g_mem.shape[0]
    out = jax.ShapeDtypeStruct((L, B, M, N_HEADS, HEAD_W), F32)
    dense = jax.ShapeDtypeStruct((L, B, M, MIX_W), F32)
    return pl.pallas_call(
        _memkv_kernel,
        grid=(L, B),
        in_specs=[pl.BlockSpec((1, M, D), lambda l, b: (b, 0, 0)),
                  pl.BlockSpec((1, 1, D), lambda l, b: (l, 0, 0)),
                  pl.BlockSpec((1, D, 2 * MIX_W), lambda l, b: (l, 0, 0))],
        out_specs=[pl.BlockSpec((1, 1, M, N_HEADS, HEAD_W), lambda l, b: (l, b, 0, 0, 0))] * 2
                  + [pl.BlockSpec((1, 1, M, MIX_W), lambda l, b: (l, b, 0, 0))] * 2,
        out_shape=[out, out, dense, dense],
        compiler_params=_cparams("arbitrary", "arbitrary"),
        name="mem_kv",
    )(mem, g_mem.reshape(L, 1, D), w_mem_kv)


def _proj0_kernel(x_ref, g_ref, w_ref, mix_ref, qm_ref):
    x = x_ref[...]
    h = (x * _inv_rms(x) * g_ref[...]).astype(BF16)
    z = _dot(h, w_ref[...])
    mix_ref[...] = z[:, :MIX_W]
    qm_ref[...] = z[:, MIX_W:].astype(BF16)


def _proj0(x, g, w, layer, tm):
    R, D = x.shape
    return pl.pallas_call(
        _proj0_kernel,
        grid=(R // tm,),
        in_specs=[pl.BlockSpec((tm, D), lambda i: (i, 0)),
                  _layer(_rows(g), layer), _layer(w, layer)],
        out_specs=[pl.BlockSpec((tm, MIX_W), lambda i: (i, 0)),
                   pl.BlockSpec((tm, MIX_W), lambda i: (i, 0))],
        out_shape=[jax.ShapeDtypeStruct((R, MIX_W), F32),
                   jax.ShapeDtypeStruct((R, MIX_W), BF16)],
        compiler_params=_cparams("arbitrary"),
        name="proj0",
    )(x, _rows(g), w)


def _batch_spec(arr, layer=None):
    if layer is None:
        zeros = (0,) * (arr.ndim - 1)
        return pl.BlockSpec((1,) + arr.shape[1:], lambda b, *_: (b,) + zeros)
    zeros = (0,) * (arr.ndim - 2)
    return pl.BlockSpec((None, 1) + arr.shape[2:], lambda b, *_: (layer, b) + zeros)


def _head(ref, h):
    if len(ref.shape) == 3:
        return ref[0, :, h * HEAD_W:(h + 1) * HEAD_W]
    return ref[0, :, h, :]


def _mem_attend(qm, mk_ref, mv_ref, out_ref):
    scale = HEAD_W ** -0.5
    for h in range(N_HEADS):
        hs = slice(h * HEAD_W, (h + 1) * HEAD_W)
        kh = _head(mk_ref, h).astype(BF16)
        vh = _head(mv_ref, h).astype(BF16)
        s = _dot_nt(qm[:, hs], kh) * scale
        m = jnp.max(s, axis=-1, keepdims=True)
        e = jnp.exp(s - m)
        l = jnp.sum(e, axis=-1, keepdims=True)
        o = _dot(e.astype(BF16), vh) / l
        out_ref[0, :, MIX_W + h * HEAD_W:MIX_W + (h + 1) * HEAD_W] = o.astype(out_ref.dtype)


def _mix0_kernel(mix_ref, halo_ref, hist_ref, qm_ref, wp_ref, ps_ref, mk_ref, mv_ref, out_ref, *, tm, pos0):
    i = pl.program_id(1)
    u = mix_ref[0]
    halo = jnp.where(i == 0, hist_ref[0], halo_ref[0])
    ext = jnp.concatenate([halo, u], axis=0)
    pos = pos0 + i * tm + lax.broadcasted_iota(jnp.int32, (tm, HEAD_W), 0)
    for g, w in enumerate(POOL_WINDOWS):
        gs = slice(g * HEAD_W, (g + 1) * HEAD_W)
        acc = ext[:, gs]
        span = 1
        while span < w:
            acc = acc + pltpu.roll(acc, span, 0)
            span *= 2
        cnt = jnp.minimum(pos + 1, w).astype(F32)
        d = acc[HALO:] / cnt - u[:, gs]
        t = _dot(d.astype(BF16), wp_ref[g]) * ps_ref[:, gs]
        out_ref[0, :, gs] = t.astype(out_ref.dtype)
    _mem_attend(qm_ref[0], mk_ref, mv_ref, out_ref)


def _mix0(mix, hist16, qm, mem_k, mem_v, w_pool, pool_scale, layer, tm, pos0):
    B, T, _ = mix.shape
    hb = tm // HALO
    return pl.pallas_call(
        functools.partial(_mix0_kernel, tm=tm, pos0=pos0),
        grid=(B, T // tm),
        in_specs=[pl.BlockSpec((1, tm, MIX_W), lambda b, i: (b, i, 0)),
                  pl.BlockSpec((1, HALO, MIX_W), lambda b, i: (b, jnp.maximum(i * hb - 1, 0), 0)),
                  pl.BlockSpec((1, HALO, MIX_W), lambda b, i: (b, 0, 0)),
                  pl.BlockSpec((1, tm, MIX_W), lambda b, i: (b, i, 0)),
                  _layer(w_pool, layer), _layer(_rows(pool_scale), layer),
                  _batch_spec(mem_k, layer), _batch_spec(mem_v, layer)],
        out_specs=pl.BlockSpec((1, tm, 2 * MIX_W), lambda b, i: (b, i, 0)),
        out_shape=jax.ShapeDtypeStruct((B, T, 2 * MIX_W), BF16),
        compiler_params=_cparams("arbitrary", "arbitrary"),
        name="mix0",
    )(mix, mix, hist16, qm, w_pool, _rows(pool_scale), mem_k, mem_v)


def _outmlp_kernel(x_ref, cat_ref, wo_ref, g_ref, w1_ref, w2_ref, gf_ref, out_ref, a_ref, *, final_norm, ff_chunk):
    x1 = x_ref[...] + _dot(cat_ref[...], wo_ref[...])
    h = (x1 * _inv_rms(x1) * g_ref[...]).astype(BF16)
    for c in range(w1_ref.shape[1] // ff_chunk):
        cs = slice(c * ff_chunk, (c + 1) * ff_chunk)
        a = jnp.maximum(_dot(h, w1_ref[:, cs]), 0.0)
        a_ref[:, cs] = (a * a).astype(BF16)
    x2 = x1 + _dot(a_ref[...], w2_ref[...])
    if final_norm:
        x2 = x2 * _inv_rms(x2) * gf_ref[...]
    out_ref[...] = x2


def _outmlp(x, cat, w_out, g_ffn, w1, w2, g_final, layer, tm, final_norm):
    R, D = x.shape
    FF = w1.shape[-1]
    return pl.pallas_call(
        functools.partial(_outmlp_kernel, final_norm=final_norm, ff_chunk=1024),
        grid=(R // tm,),
        in_specs=[pl.BlockSpec((tm, D), lambda i: (i, 0)),
                  pl.BlockSpec((tm, D), lambda i: (i, 0)),
                  _layer(w_out, layer), _layer(_rows(g_ffn), layer), _layer(w1, layer), _layer(w2, layer),
                  _resident((1, D), lambda i: (0, 0))],
        out_specs=pl.BlockSpec((tm, D), lambda i: (i, 0)),
        out_shape=jax.ShapeDtypeStruct((R, D), F32),
        scratch_shapes=[pltpu.VMEM((tm, FF), BF16)],
        compiler_params=_cparams("arbitrary"),
        name="outmlp",
    )(x, cat, w_out, _rows(g_ffn), w1, w2, g_final.reshape(1, D))


def _lane_window(lane, lo):
    return (lane >= lo) & (lane < lo + N_SPLIT)


def _head_copies(buf, slot, hbm_refs, row0, tm, sem):
    out = []
    for t, hbm in enumerate(hbm_refs):
        for h in range(N_HEADS):
            src = buf.at[slot, :, pl.ds(t * MIX_W + h * HEAD_W, HEAD_W)]
            out.append(pltpu.make_async_copy(src, hbm.at[pl.ds(row0, tm), h, :], sem.at[slot]))
    return out


def _proj1_kernel(x_ref, ga_ref, wi_ref, gk_ref, wk_ref, z_ref, *rest, seq_len, n_bf):
    bf_refs = rest[:n_bf]
    k_hbm, v_hbm, kv_buf, sem = rest[n_bf:]
    tm = x_ref.shape[0]
    i = pl.program_id(0)
    n_steps = pl.num_programs(0)
    slot = i % 2
    copies = functools.partial(_head_copies, kv_buf, hbm_refs=(k_hbm, v_hbm), tm=tm, sem=sem)

    x = x_ref[...]
    xn = x * _inv_rms(x)
    z = _dot((xn * ga_ref[...]).astype(BF16), wi_ref[...])
    z_ref[:, :MIX_W] = (z[:, :MIX_W] * Q_SCALE).astype(BF16)
    z_ref[:, MIX_W:] = z[:, MIX_W:].astype(BF16)
    kv = _dot((xn * gk_ref[...]).astype(BF16), wk_ref[...])

    @pl.when(i >= 2)
    def _():
        for c in copies(slot, row0=(i - 2) * tm):
            c.wait()

    kv_buf[slot] = kv
    for c in copies(slot, row0=i * tm):
        c.start()

    @pl.when(i == n_steps - 1)
    def _():
        @pl.when(i >= 1)
        def _():
            for c in copies(1 - slot, row0=(i - 1) * tm):
                c.wait()
        for c in copies(slot, row0=i * tm):
            c.wait()

    if n_bf:
        k = kv[:, :MIX_W]
        v = kv[:, MIX_W:]
        k1a_ref, k2a_ref, vb_ref = bf_refs
        lane = lax.broadcasted_iota(jnp.int32, (tm, HEAD_W), 1)
        pos = (i * tm) % seq_len + lax.broadcasted_iota(jnp.int32, (tm, HEAD_W), 0)
        off = pos % CHUNK
        base = (pos - off).astype(F32)
        off = off.astype(F32)

        def pos_lanes(lo):
            return jnp.where(_lane_window(lane, lo), base, jnp.where(_lane_window(lane, lo + N_SPLIT), off, 0.0))

        pos_hi = pos_lanes(DIFF_D)
        pos_lo = pos_lanes(0)
        for h in range(N_HEADS):
            hs = slice(h * HEAD_W, (h + 1) * HEAD_W)
            k1a_ref[:, hs] = jnp.where(lane < DIFF_D, k[:, hs], pos_hi).astype(BF16)
            k2a_ref[:, hs] = jnp.where(lane >= DIFF_D, k[:, hs], pos_lo).astype(BF16)
        vb_ref[...] = v.astype(BF16)


def _proj1(x, g_attn, w_in, layer, g_kv, w_kv, tm, seq_len, attn_operands):
    R, D = x.shape
    row = lambda i: (i, 0)
    fix = lambda i: (0, 0)
    n_bf = 3 if attn_operands else 0
    return pl.pallas_call(
        functools.partial(_proj1_kernel, seq_len=seq_len, n_bf=n_bf),
        grid=(R // tm,),
        in_specs=[pl.BlockSpec((tm, D), row),
                  _layer(_rows(g_attn), layer), _layer(w_in, layer),
                  _resident((1, D), fix), _resident((D, 2 * MIX_W), fix)],
        out_specs=[pl.BlockSpec((tm, 2 * MIX_W), row)]
                  + [pl.BlockSpec((tm, MIX_W), row)] * n_bf
                  + [pl.BlockSpec(memory_space=pl.ANY)] * 2,
        out_shape=[jax.ShapeDtypeStruct((R, 2 * MIX_W), BF16)]
                  + [jax.ShapeDtypeStruct((R, MIX_W), BF16)] * n_bf
                  + [jax.ShapeDtypeStruct((R, N_HEADS, HEAD_W), F32)] * 2,
        scratch_shapes=[pltpu.VMEM((2, tm, 2 * MIX_W), F32), pltpu.SemaphoreType.DMA((2,))],
        compiler_params=_cparams("arbitrary"),
        name="proj1",
    )(x, _rows(g_attn), w_in, g_kv.reshape(1, D), w_kv)


def _lambda(lq, lam_init):
    a = jnp.sum(lq[0:1] * lq[1:2], axis=-1, keepdims=True)
    b = jnp.sum(lq[2:3] * lq[3:4], axis=-1, keepdims=True)
    return jnp.exp(a) - jnp.exp(b) + lam_init


def _slope2(h):
    return 2.0 ** (-8.0 * (h + 1) / N_HEADS) * LOG2E


def _bf16_pieces(c):
    out = []
    for _ in range(N_SPLIT):
        p = float(np.asarray(c, dtype=ml_dtypes.bfloat16))
        out.append(p)
        c -= p
    return out


def _slope_lanes(h, lo, shape):
    lane = lax.broadcasted_iota(jnp.int32, shape, 1)
    out = jnp.zeros(shape, F32)
    for p, piece in enumerate(_bf16_pieces(_slope2(h))):
        out = jnp.where((lane == lo + p) | (lane == lo + N_SPLIT + p), piece, out)
    return out.astype(BF16)


def _finish_head(o1, o2, lam, gs, lam_init):
    o = o1 - lam * o2
    return o * _inv_rms(o) * gs * (1.0 - lam_init)


def _diff_kernel(lq_ref, gs_ref, q_ref, qm_ref, k1_ref, k2_ref, v_ref, mk_ref, mv_ref, out_ref,
                 qa_ref, acc_ref, m_ref, s_ref, *, tq, lam_init):
    i = pl.program_id(1)
    nblk = tq // HEAD_W
    n_chain = 2 * N_HEADS
    k_refs = (k1_ref, k2_ref)
    row = lax.broadcasted_iota(jnp.int32, (tq, tq), 0)
    col = lax.broadcasted_iota(jnp.int32, (tq, tq), 1)
    ahead = jnp.maximum(col - row, 0).astype(F32)
    vis = (col // CHUNK) <= (row // CHUNK)
    ones = jnp.ones((tq, HEAD_W), BF16)
    lane = lax.broadcasted_iota(jnp.int32, (tq, HEAD_W), 1)
    q0 = pl.multiple_of(i * tq, tq)

    def hslice(n):
        return slice((n // 2) * HEAD_W, (n // 2 + 1) * HEAD_W)

    def scores(n, k0):
        return _dot_nt(qa_ref[n], k_refs[n % 2][0, pl.ds(k0, tq), hslice(n)])

    def v_aug(n, k0):
        return jnp.concatenate([v_ref[0, pl.ds(k0, tq), hslice(n)], ones], axis=1)

    def consume(n, k0, first):
        vt = v_aug(n, k0)
        blocks = [s_ref[n, :, b * HEAD_W:(b + 1) * HEAD_W] for b in range(nblk)]
        m_row = jnp.max(functools.reduce(jnp.maximum, blocks), axis=-1, keepdims=True)
        if first:
            m_new = jnp.broadcast_to(m_row, (tq, HEAD_W))
        else:
            m_old = m_ref[n]
            m_new = jnp.maximum(m_old, m_row)
        p = jnp.concatenate([jnp.exp2(b - m_new) for b in blocks], axis=1).astype(BF16)
        pv = _dot(p, vt)
        if first:
            acc_ref[n] = pv
        else:
            alpha = jnp.exp2(m_old - m_new)
            acc_ref[n] = jnp.concatenate([alpha, alpha], axis=1) * acc_ref[n] + pv
        m_ref[n] = m_new

    _mem_attend(qm_ref[0], mk_ref, mv_ref, out_ref)
    for h in range(N_HEADS):
        qh = q_ref[0, :, h * HEAD_W:(h + 1) * HEAD_W]
        qa_ref[2 * h] = jnp.where(lane < DIFF_D, qh, _slope_lanes(h, DIFF_D, qh.shape))
        qa_ref[2 * h + 1] = jnp.where(lane >= DIFF_D, qh, _slope_lanes(h, 0, qh.shape))
        fix = jnp.where(vis, (-2.0 * _slope2(h)) * ahead, NEG_BIG)
        for n in (2 * h, 2 * h + 1):
            s_ref[n] = scores(n, q0) + fix

    @pl.when(i == 0)
    def _():
        for n in range(n_chain):
            consume(n, q0, True)

    @pl.when(i > 0)
    def _():
        for n in range(n_chain):
            consume(n, q0, True)
            s_ref[n] = scores(n, 0)

        def body(j, carry):
            k_cur = pl.multiple_of((j - 1) * tq, tq)
            k_next = pl.multiple_of(j * tq, tq)
            for n in range(n_chain):
                consume(n, k_cur, False)
                s_ref[n] = scores(n, k_next)
            return carry

        lax.fori_loop(1, i, body, 0)
        k_last = pl.multiple_of((i - 1) * tq, tq)
        for n in range(n_chain):
            consume(n, k_last, False)

    lam = _lambda(lq_ref[...], lam_init)
    for h in range(N_HEADS):
        o = [acc_ref[n, :, :HEAD_W] / acc_ref[n, :, HEAD_W:] for n in (2 * h, 2 * h + 1)]
        out_ref[0, :, h * HEAD_W:(h + 1) * HEAD_W] = _finish_head(
            o[0], o[1], lam, gs_ref[...], lam_init).astype(out_ref.dtype)


def _diff_prompt(z, k1a, k2a, vb, mem_k, mem_v, lq, g_sub, layer, lam_init, tq):
    B, T, _ = z.shape
    n_chain = 2 * N_HEADS
    diff_layer = layer - (mem_k.shape[0] - lq.shape[0])
    return pl.pallas_call(
        functools.partial(_diff_kernel, tq=tq, lam_init=lam_init),
        grid=(B, T // tq),
        in_specs=[_layer(lq, diff_layer), _layer(_rows(g_sub), diff_layer),
                  pl.BlockSpec((1, tq, MIX_W), lambda b, i: (b, i, 0)),
                  pl.BlockSpec((1, tq, MIX_W), lambda b, i: (b, i, 1)),
                  _resident((1, T, MIX_W), lambda b, i: (b, 0, 0)),
                  _resident((1, T, MIX_W), lambda b, i: (b, 0, 0)),
                  _resident((1, T, MIX_W), lambda b, i: (b, 0, 0)),
                  _batch_spec(mem_k, layer), _batch_spec(mem_v, layer)],
        out_specs=pl.BlockSpec((1, tq, 2 * MIX_W), lambda b, i: (b, i, 0)),
        out_shape=jax.ShapeDtypeStruct((B, T, 2 * MIX_W), BF16),
        scratch_shapes=[pltpu.VMEM((n_chain, tq, HEAD_W), BF16),
                        pltpu.VMEM((n_chain, tq, 2 * HEAD_W), F32),
                        pltpu.VMEM((n_chain, tq, HEAD_W), F32),
                        pltpu.VMEM((n_chain, tq, tq), F32)],
        compiler_params=_cparams("arbitrary", "arbitrary"),
        name="diff_prompt",
    )(lq, _rows(g_sub), z, z, k1a, k2a, vb, mem_k, mem_v)


def _cache_copies(hbm_refs, b, buf, slot, sem):
    return [pltpu.make_async_copy(hbm.at[b, :, h, :], buf.at[slot, t, h], sem.at[slot])
            for t, hbm in enumerate(hbm_refs) for h in range(N_HEADS)]


def _diff_sample_kernel(lq_ref, gs_ref, q_ref, qm_ref, ck_hbm, cv_hbm, nk_ref, nv_ref, mk_ref, mv_ref, out_ref,
                        cache_buf, sem, *, p0, lam_init):
    b = pl.program_id(0)
    slot = b % 2
    copies = functools.partial(_cache_copies, (ck_hbm, cv_hbm), buf=cache_buf, sem=sem)

    @pl.when(b == 0)
    def _():
        for c in copies(b, slot=slot):
            c.start()

    @pl.when(b + 1 < pl.num_programs(0))
    def _():
        for c in copies(b + 1, slot=1 - slot):
            c.start()

    for c in copies(b, slot=slot):
        c.wait()

    ts = q_ref.shape[1]
    lam = _lambda(lq_ref[...], lam_init)
    qpos_p = p0 + lax.broadcasted_iota(jnp.int32, (ts, p0), 0)
    dist_p = (qpos_p - lax.broadcasted_iota(jnp.int32, (ts, p0), 1)).astype(F32)
    row = p0 + lax.broadcasted_iota(jnp.int32, (ts, ts), 0)
    col = p0 + lax.broadcasted_iota(jnp.int32, (ts, ts), 1)
    dist_n = jnp.abs(row - col).astype(F32)
    vis_n = (col // CHUNK) <= (row // CHUNK)
    lane = lax.broadcasted_iota(jnp.int32, (ts, HEAD_W), 1)
    for h in range(N_HEADS):
        hs = slice(h * HEAD_W, (h + 1) * HEAD_W)
        qh = q_ref[0, :, hs]
        zero = jnp.zeros_like(qh)
        qc = (jnp.where(lane < DIFF_D, qh, zero), jnp.where(lane >= DIFF_D, qh, zero))
        kp = cache_buf[slot, 0, h].astype(BF16)
        vp = cache_buf[slot, 1, h].astype(BF16)
        kn = _head(nk_ref, h).astype(BF16)
        vn = _head(nv_ref, h).astype(BF16)
        bias_p = -_slope2(h) * dist_p
        bias_n = jnp.where(vis_n, -_slope2(h) * dist_n, NEG_BIG)
        o = []
        for c in range(2):
            sp = _dot_nt(qc[c], kp) + bias_p
            sn = _dot_nt(qc[c], kn) + bias_n
            m = jnp.maximum(jnp.max(sp, axis=-1, keepdims=True), jnp.max(sn, axis=-1, keepdims=True))
            ep = jnp.exp2(sp - m)
            en = jnp.exp2(sn - m)
            l = jnp.sum(ep, axis=-1, keepdims=True) + jnp.sum(en, axis=-1, keepdims=True)
            o.append((_dot(ep.astype(BF16), vp) + _dot(en.astype(BF16), vn)) / l)
        out_ref[0, :, hs] = _finish_head(o[0], o[1], lam, gs_ref[...], lam_init).astype(out_ref.dtype)
    _mem_attend(qm_ref[0], mk_ref, mv_ref, out_ref)


def _diff_sample(z, cache_k, cache_v, k_new, v_new, mem_k, mem_v, lq, g_sub, layer, lam_init):
    B, Ts, _ = z.shape
    P0 = cache_k.shape[1]
    diff_layer = layer - (mem_k.shape[0] - lq.shape[0])
    return pl.pallas_call(
        functools.partial(_diff_sample_kernel, p0=P0, lam_init=lam_init),
        grid=(B,),
        in_specs=[_layer(lq, diff_layer), _layer(_rows(g_sub), diff_layer),
                  pl.BlockSpec((1, Ts, MIX_W), lambda b: (b, 0, 0)),
                  pl.BlockSpec((1, Ts, MIX_W), lambda b: (b, 0, 1)),
                  pl.BlockSpec(memory_space=pl.ANY), pl.BlockSpec(memory_space=pl.ANY),
                  _batch_spec(k_new), _batch_spec(v_new), _batch_spec(mem_k, layer), _batch_spec(mem_v, layer)],
        out_specs=pl.BlockSpec((1, Ts, 2 * MIX_W), lambda b: (b, 0, 0)),
        out_shape=jax.ShapeDtypeStruct((B, Ts, 2 * MIX_W), BF16),
        scratch_shapes=[pltpu.VMEM((2, 2, N_HEADS, P0, HEAD_W), F32), pltpu.SemaphoreType.DMA((2,))],
        compiler_params=_cparams("arbitrary"),
        name="diff_sample",
    )(lq, _rows(g_sub), z, z, cache_k, cache_v, k_new, v_new, mem_k, mem_v)


def _lambda_init(layer_idx):
    return 0.8 - 0.6 * math.exp(-0.3 * layer_idx)


def _trunk(x, pos0, pool_hist, mem_k, mem_v, past_k, past_v, w, tm_rows, tm_seq, tq):
    B, T, D = x.shape
    R = B * T
    xr = x.reshape(R, D)
    hist16 = jnp.pad(pool_hist, ((0, 0), (HALO - POOL_HIST, 0), (0, 0)))

    mix, qm = _proj0(xr, w["g_attn"], w["w_in"], 0, tm_rows)
    mix3 = mix.reshape(B, T, MIX_W)
    cat0 = _mix0(mix3, hist16, qm.reshape(B, T, MIX_W), mem_k, mem_v, w["w_pool"], w["pool_scale"], 0,
                 tm_seq, pos0)
    new_pool = jnp.concatenate([pool_hist, mix3], axis=1)[:, -POOL_HIST:]
    x1 = _outmlp(xr, cat0.reshape(R, D), w["w_out"], w["g_ffn"], w["w_ff1"], w["w_ff2"], w["g_final"], 0,
                 tm_rows, final_norm=False)

    prompt = past_k is None
    z, *bf, k, v = _proj1(x1, w["g_attn"], w["w_in"], 1, w["g_kv"], w["w_kv"], tm_rows, T, prompt)
    z3 = z.reshape(B, T, D)
    lam_init = _lambda_init(1)
    if prompt:
        k1a, k2a, vb = (a.reshape(B, T, MIX_W) for a in bf)
        cat1 = _diff_prompt(z3, k1a, k2a, vb, mem_k, mem_v, w["lambda_qk"], w["g_subln"], 1, lam_init, tq)
    else:
        shp = (B, T, N_HEADS, HEAD_W)
        cat1 = _diff_sample(z3, past_k, past_v, k.reshape(shp), v.reshape(shp), mem_k, mem_v,
                            w["lambda_qk"], w["g_subln"], 1, lam_init)
    y = _outmlp(x1, cat1.reshape(R, D), w["w_out"], w["g_ffn"], w["w_ff1"], w["w_ff2"], w["g_final"], 1,
                tm_rows, final_norm=True)
    shp = (B, T, N_HEADS, HEAD_W)
    return y.reshape(B, T, D), new_pool[None], k.reshape(shp), v.reshape(shp)


def kernel(x_prompt, x_sample, mem_prompt, cache_k, cache_v, cache_mem_k, cache_mem_v, state_pool, g_attn, w_in, w_out, g_mem, w_mem_kv, g_ffn, w_ff1, w_ff2, w_pool, pool_scale, lambda_qk, g_subln, g_kv, w_kv, g_final):
    B, T, D = x_prompt.shape
    Bs, Ts, _ = x_sample.shape
    P0 = cache_k.shape[1]
    w = dict(g_attn=g_attn, w_in=w_in.astype(BF16), w_out=w_out.astype(BF16), g_ffn=g_ffn,
             w_ff1=w_ff1.astype(BF16), w_ff2=w_ff2.astype(BF16), w_pool=w_pool.astype(BF16),
             pool_scale=pool_scale, lambda_qk=lambda_qk, g_subln=g_subln, g_kv=g_kv,
             w_kv=w_kv.astype(BF16), g_final=g_final)

    mem_k_p, mem_v_p, mem_k_d, mem_v_d = _mem_kv(mem_prompt, g_mem, w_mem_kv.astype(BF16))
    hist0 = jnp.zeros((B, POOL_HIST, MIX_W), F32)
    y_p, pool_p, k_p, v_p = _trunk(x_prompt, 0, hist0, mem_k_d, mem_v_d, None, None, w,
                                   tm_rows=512, tm_seq=512, tq=512)
    y_s, pool_s, k_s, v_s = _trunk(x_sample, P0, state_pool[0], cache_mem_k, cache_mem_v, cache_k, cache_v, w,
                                   tm_rows=Bs * Ts, tm_seq=Ts, tq=None)
    return (y_p, y_s, mem_k_p, mem_v_p, pool_p, k_p, v_p, pool_s, k_s, v_s)
```

```python
import functools
import math

import jax
import jax.numpy as jnp
import ml_dtypes
import numpy as np
from jax import lax
from jax.experimental import pallas as pl
from jax.experimental.pallas import tpu as pltpu

F32 = jnp.float32
BF16 = jnp.bfloat16

CHUNK = 64
POOL_WINDOWS = (2, 4, 8, 16)
POOL_HIST = 15
HALO = 16
N_HEADS = 4
HEAD_W = 128
DIFF_D = 64
MIX_W = N_HEADS * HEAD_W
EPS = 1e-6
NEG_BIG = -1e30
LOG2E = math.log2(math.e)
Q_SCALE = DIFF_D ** -0.5 * LOG2E
N_SPLIT = 4
VMEM_LIMIT = 56 * 1024 * 1024


def _cparams(*sem):
    return pltpu.CompilerParams(dimension_semantics=sem, vmem_limit_bytes=VMEM_LIMIT)


def _resident(shape, index_map):
    return pl.BlockSpec(shape, index_map, pipeline_mode=pl.Buffered(1))


def _layer(arr, layer):
    zeros = (0,) * (arr.ndim - 1)
    return pl.BlockSpec((None,) + arr.shape[1:], lambda *_: (layer,) + zeros, pipeline_mode=pl.Buffered(1))


def _rows(g):
    return g.reshape(g.shape[0], 1, g.shape[1])


def _inv_rms(x):
    return lax.rsqrt(jnp.mean(x * x, axis=-1, keepdims=True) + EPS)


def _dot(a, b):
    return jnp.dot(a, b, preferred_element_type=F32)


def _dot_nt(a, b):
    return lax.dot_general(a, b, (((1,), (1,)), ((), ())), preferred_element_type=F32)


def _memkv_kernel(x_ref, g_ref, w_ref, k_ref, v_ref, kd_ref, vd_ref):
    x = x_ref[0]
    h = (x * _inv_rms(x) * g_ref[0]).astype(BF16)
    kv = _dot(h, w_ref[0])
    kd_ref[0, 0] = kv[:, :MIX_W]
    vd_ref[0, 0] = kv[:, MIX_W:]
    for hd in range(N_HEADS):
        k_ref[0, 0, :, hd, :] = kv[:, hd * HEAD_W:(hd + 1) * HEAD_W]
        v_ref[0, 0, :, hd, :] = kv[:, MIX_W + hd * HEAD_W:MIX_W + (hd + 1) * HEAD_W]


def _mem_kv(mem, g_mem, w_mem_kv):
    B, M, D = mem.shape
    L = g_mem.shape[0]
    out = jax.ShapeDtypeStruct((L, B, M, N_HEADS, HEAD_W), F32)
    dense = jax.ShapeDtypeStruct((L, B, M, MIX_W), F32)
    return pl.pallas_call(
        _memkv_kernel,
        grid=(L, B),
        in_specs=[pl.BlockSpec((1, M, D), lambda l, b: (b, 0, 0)),
                  pl.BlockSpec((1, 1, D), lambda l, b: (l, 0, 0)),
                  pl.BlockSpec((1, D, 2 * MIX_W), lambda l, b: (l, 0, 0))],
        out_specs=[pl.BlockSpec((1, 1, M, N_HEADS, HEAD_W), lambda l, b: (l, b, 0, 0, 0))] * 2
                  + [pl.BlockSpec((1, 1, M, MIX_W), lambda l, b: (l, b, 0, 0))] * 2,
        out_shape=[out, out, dense, dense],
        compiler_params=_cparams("arbitrary", "arbitrary"),
        name="mem_kv",
    )(mem, g_mem.reshape(L, 1, D), w_mem_kv)


def _batch_spec(arr, layer=None):
    if layer is None:
        zeros = (0,) * (arr.ndim - 1)
        return pl.BlockSpec((1,) + arr.shape[1:], lambda b, *_: (b,) + zeros)
    zeros = (0,) * (arr.ndim - 2)
    return pl.BlockSpec((None, 1) + arr.shape[2:], lambda b, *_: (layer, b) + zeros)


def _head(ref, h):
    if len(ref.shape) == 3:
        return ref[0, :, h * HEAD_W:(h + 1) * HEAD_W]
    return ref[0, :, h, :]


def _mem_attend(qm, mk_ref, mv_ref, out_ref):
    scale = HEAD_W ** -0.5
    for h in range(N_HEADS):
        hs = slice(h * HEAD_W, (h + 1) * HEAD_W)
        kh = _head(mk_ref, h).astype(BF16)
        vh = _head(mv_ref, h).astype(BF16)
        s = _dot_nt(qm[:, hs], kh) * scale
        m = jnp.max(s, axis=-1, keepdims=True)
        e = jnp.exp(s - m)
        l = jnp.sum(e, axis=-1, keepdims=True)
        o = _dot(e.astype(BF16), vh) / l
        out_ref[0, :, MIX_W + h * HEAD_W:MIX_W + (h + 1) * HEAD_W] = o.astype(out_ref.dtype)


def _mix0_kernel(x_ref, g_ref, wi_ref, hist_ref, wp_ref, ps_ref, mk_ref, mv_ref, out_ref, tail_ref, carry_ref,
                 *, tm, pos0):
    i = pl.program_id(1)

    @pl.when(i == 0)
    def _():
        carry_ref[...] = hist_ref[0]

    x = x_ref[0]
    z = _dot((x * _inv_rms(x) * g_ref[...]).astype(BF16), wi_ref[...])
    u = z[:, :MIX_W]
    qm = z[:, MIX_W:].astype(BF16)
    ext = jnp.concatenate([carry_ref[...], u], axis=0)
    carry_ref[...] = u[tm - HALO:]
    tail_ref[0] = u[tm - HALO:]
    pos = pos0 + i * tm + lax.broadcasted_iota(jnp.int32, (tm, HEAD_W), 0)
    for g, w in enumerate(POOL_WINDOWS):
        gs = slice(g * HEAD_W, (g + 1) * HEAD_W)
        acc = ext[:, gs]
        span = 1
        while span < w:
            acc = acc + pltpu.roll(acc, span, 0)
            span *= 2
        cnt = jnp.minimum(pos + 1, w).astype(F32)
        d = acc[HALO:] / cnt - u[:, gs]
        t = _dot(d.astype(BF16), wp_ref[g]) * ps_ref[:, gs]
        out_ref[0, :, gs] = t.astype(out_ref.dtype)
    _mem_attend(qm, mk_ref, mv_ref, out_ref)


def _mix0(x, g_attn, w_in, hist16, mem_k, mem_v, w_pool, pool_scale, layer, tm, pos0):
    B, T, D = x.shape
    assert T % tm == 0 and tm >= HALO
    return pl.pallas_call(
        functools.partial(_mix0_kernel, tm=tm, pos0=pos0),
        grid=(B, T // tm),
        in_specs=[pl.BlockSpec((1, tm, D), lambda b, i: (b, i, 0)),
                  _layer(_rows(g_attn), layer), _layer(w_in, layer),
                  pl.BlockSpec((1, HALO, MIX_W), lambda b, i: (b, 0, 0)),
                  _layer(w_pool, layer), _layer(_rows(pool_scale), layer),
                  _batch_spec(mem_k, layer), _batch_spec(mem_v, layer)],
        out_specs=[pl.BlockSpec((1, tm, 2 * MIX_W), lambda b, i: (b, i, 0)),
                   pl.BlockSpec((1, HALO, MIX_W), lambda b, i: (b, 0, 0))],
        out_shape=[jax.ShapeDtypeStruct((B, T, 2 * MIX_W), BF16),
                   jax.ShapeDtypeStruct((B, HALO, MIX_W), F32)],
        scratch_shapes=[pltpu.VMEM((HALO, MIX_W), F32)],
        compiler_params=_cparams("arbitrary", "arbitrary"),
        name="mix0",
    )(x, _rows(g_attn), w_in, hist16, w_pool, _rows(pool_scale), mem_k, mem_v)


def _outmlp_kernel(x_ref, cat_ref, wo_ref, g_ref, w1_ref, w2_ref, gf_ref, out_ref, a_ref, *, final_norm, ff_chunk):
    x1 = x_ref[...] + _dot(cat_ref[...], wo_ref[...])
    h = (x1 * _inv_rms(x1) * g_ref[...]).astype(BF16)
    for c in range(w1_ref.shape[1] // ff_chunk):
        cs = slice(c * ff_chunk, (c + 1) * ff_chunk)
        a = jnp.maximum(_dot(h, w1_ref[:, cs]), 0.0)
        a_ref[:, cs] = (a * a).astype(BF16)
    x2 = x1 + _dot(a_ref[...], w2_ref[...])
    if final_norm:
        x2 = x2 * _inv_rms(x2) * gf_ref[...]
    out_ref[...] = x2


def _outmlp(x, cat, w_out, g_ffn, w1, w2, g_final, layer, tm, final_norm):
    R, D = x.shape
    FF = w1.shape[-1]
    return pl.pallas_call(
        functools.partial(_outmlp_kernel, final_norm=final_norm, ff_chunk=1024),
        grid=(R // tm,),
        in_specs=[pl.BlockSpec((tm, D), lambda i: (i, 0)),
                  pl.BlockSpec((tm, D), lambda i: (i, 0)),
                  _layer(w_out, layer), _layer(_rows(g_ffn), layer), _layer(w1, layer), _layer(w2, layer),
                  _resident((1, D), lambda i: (0, 0))],
        out_specs=pl.BlockSpec((tm, D), lambda i: (i, 0)),
        out_shape=jax.ShapeDtypeStruct((R, D), F32),
        scratch_shapes=[pltpu.VMEM((tm, FF), BF16)],
        compiler_params=_cparams("arbitrary"),
        name="outmlp",
    )(x, cat, w_out, _rows(g_ffn), w1, w2, g_final.reshape(1, D))


def _lane_window(lane, lo):
    return (lane >= lo) & (lane < lo + N_SPLIT)


def _head_copies(buf, slot, hbm_refs, row0, tm, sem):
    out = []
    for t, hbm in enumerate(hbm_refs):
        for h in range(N_HEADS):
            src = buf.at[slot, :, pl.ds(t * MIX_W + h * HEAD_W, HEAD_W)]
            out.append(pltpu.make_async_copy(src, hbm.at[pl.ds(row0, tm), h, :], sem.at[slot]))
    return out


def _proj1_kernel(x_ref, ga_ref, wi_ref, gk_ref, wk_ref, z_ref, *rest, seq_len, n_bf):
    bf_refs = rest[:n_bf]
    k_hbm, v_hbm, kv_buf, sem = rest[n_bf:]
    tm = x_ref.shape[0]
    i = pl.program_id(0)
    n_steps = pl.num_programs(0)
    slot = i % 2
    copies = functools.partial(_head_copies, kv_buf, hbm_refs=(k_hbm, v_hbm), tm=tm, sem=sem)

    x = x_ref[...]
    xn = x * _inv_rms(x)
    z = _dot((xn * ga_ref[...]).astype(BF16), wi_ref[...])
    z_ref[:, :MIX_W] = (z[:, :MIX_W] * Q_SCALE).astype(BF16)
    z_ref[:, MIX_W:] = z[:, MIX_W:].astype(BF16)
    kv = _dot((xn * gk_ref[...]).astype(BF16), wk_ref[...])

    @pl.when(i >= 2)
    def _():
        for c in copies(slot, row0=(i - 2) * tm):
            c.wait()

    kv_buf[slot] = kv
    for c in copies(slot, row0=i * tm):
        c.start()

    @pl.when(i == n_steps - 1)
    def _():
        @pl.when(i >= 1)
        def _():
            for c in copies(1 - slot, row0=(i - 1) * tm):
                c.wait()
        for c in copies(slot, row0=i * tm):
            c.wait()

    if n_bf:
        k = kv[:, :MIX_W]
        v = kv[:, MIX_W:]
        k1a_ref, k2a_ref, vb_ref = bf_refs
        lane = lax.broadcasted_iota(jnp.int32, (tm, HEAD_W), 1)
        pos = (i * tm) % seq_len + lax.broadcasted_iota(jnp.int32, (tm, HEAD_W), 0)
        off = pos % CHUNK
        base = (pos - off).astype(F32)
        off = off.astype(F32)

        def pos_lanes(lo):
            return jnp.where(_lane_window(lane, lo), base, jnp.where(_lane_window(lane, lo + N_SPLIT), off, 0.0))

        pos_hi = pos_lanes(DIFF_D)
        pos_lo = pos_lanes(0)
        for h in range(N_HEADS):
            hs = slice(h * HEAD_W, (h + 1) * HEAD_W)
            k1a_ref[:, hs] = jnp.where(lane < DIFF_D, k[:, hs], pos_hi).astype(BF16)
            k2a_ref[:, hs] = jnp.where(lane >= DIFF_D, k[:, hs], pos_lo).astype(BF16)
        vb_ref[...] = v.astype(BF16)


def _proj1(x, g_attn, w_in, layer, g_kv, w_kv, tm, seq_len, attn_operands):
    R, D = x.shape
    row = lambda i: (i, 0)
    fix = lambda i: (0, 0)
    n_bf = 3 if attn_operands else 0
    return pl.pallas_call(
        functools.partial(_proj1_kernel, seq_len=seq_len, n_bf=n_bf),
        grid=(R // tm,),
        in_specs=[pl.BlockSpec((tm, D), row),
                  _layer(_rows(g_attn), layer), _layer(w_in, layer),
                  _resident((1, D), fix), _resident((D, 2 * MIX_W), fix)],
        out_specs=[pl.BlockSpec((tm, 2 * MIX_W), row)]
                  + [pl.BlockSpec((tm, MIX_W), row)] * n_bf
                  + [pl.BlockSpec(memory_space=pl.ANY)] * 2,
        out_shape=[jax.ShapeDtypeStruct((R, 2 * MIX_W), BF16)]
                  + [jax.ShapeDtypeStruct((R, MIX_W), BF16)] * n_bf
                  + [jax.ShapeDtypeStruct((R, N_HEADS, HEAD_W), F32)] * 2,
        scratch_shapes=[pltpu.VMEM((2, tm, 2 * MIX_W), F32), pltpu.SemaphoreType.DMA((2,))],
        compiler_params=_cparams("arbitrary"),
        name="proj1",
    )(x, _rows(g_attn), w_in, g_kv.reshape(1, D), w_kv)


def _lambda(lq, lam_init):
    a = jnp.sum(lq[0:1] * lq[1:2], axis=-1, keepdims=True)
    b = jnp.sum(lq[2:3] * lq[3:4], axis=-1, keepdims=True)
    return jnp.exp(a) - jnp.exp(b) + lam_init


def _slope2(h):
    return 2.0 ** (-8.0 * (h + 1) / N_HEADS) * LOG2E


def _bf16_pieces(c):
    out = []
    for _ in range(N_SPLIT):
        p = float(np.asarray(c, dtype=ml_dtypes.bfloat16))
        out.append(p)
        c -= p
    return out


def _slope_lanes(h, lo, shape):
    lane = lax.broadcasted_iota(jnp.int32, shape, 1)
    out = jnp.zeros(shape, F32)
    for p, piece in enumerate(_bf16_pieces(_slope2(h))):
        out = jnp.where((lane == lo + p) | (lane == lo + N_SPLIT + p), piece, out)
    return out.astype(BF16)


def _finish_head(o1, o2, lam, gs, lam_init):
    o = o1 - lam * o2
    return o * _inv_rms(o) * gs * (1.0 - lam_init)


def _diff_kernel(lq_ref, gs_ref, q_ref, qm_ref, k1_ref, k2_ref, v_ref, mk_ref, mv_ref, out_ref,
                 qa_ref, acc_ref, m_ref, s_ref, *, tq, lam_init):
    i = pl.program_id(1)
    nblk = tq // HEAD_W
    n_chain = 2 * N_HEADS
    k_refs = (k1_ref, k2_ref)
    row = lax.broadcasted_iota(jnp.int32, (tq, tq), 0)
    col = lax.broadcasted_iota(jnp.int32, (tq, tq), 1)
    ahead = jnp.maximum(col - row, 0).astype(F32)
    vis = (col // CHUNK) <= (row // CHUNK)
    ones = jnp.ones((tq, HEAD_W), BF16)
    lane = lax.broadcasted_iota(jnp.int32, (tq, HEAD_W), 1)
    q0 = pl.multiple_of(i * tq, tq)

    def hslice(n):
        return slice((n // 2) * HEAD_W, (n // 2 + 1) * HEAD_W)

    def scores(n, k0):
        return _dot_nt(qa_ref[n], k_refs[n % 2][0, pl.ds(k0, tq), hslice(n)])

    def v_aug(n, k0):
        return jnp.concatenate([v_ref[0, pl.ds(k0, tq), hslice(n)], ones], axis=1)

    def consume(n, k0, first):
        vt = v_aug(n, k0)
        blocks = [s_ref[n, :, b * HEAD_W:(b + 1) * HEAD_W] for b in range(nblk)]
        m_row = jnp.max(functools.reduce(jnp.maximum, blocks), axis=-1, keepdims=True)
        if first:
            m_new = jnp.broadcast_to(m_row, (tq, HEAD_W))
        else:
            m_old = m_ref[n]
            m_new = jnp.maximum(m_old, m_row)
        p = jnp.concatenate([jnp.exp2(b - m_new) for b in blocks], axis=1).astype(BF16)
        pv = _dot(p, vt)
        if first:
            acc_ref[n] = pv
        else:
            alpha = jnp.exp2(m_old - m_new)
            acc_ref[n] = jnp.concatenate([alpha, alpha], axis=1) * acc_ref[n] + pv
        m_ref[n] = m_new

    _mem_attend(qm_ref[0], mk_ref, mv_ref, out_ref)
    for h in range(N_HEADS):
        qh = q_ref[0, :, h * HEAD_W:(h + 1) * HEAD_W]
        qa_ref[2 * h] = jnp.where(lane < DIFF_D, qh, _slope_lanes(h, DIFF_D, qh.shape))
        qa_ref[2 * h + 1] = jnp.where(lane >= DIFF_D, qh, _slope_lanes(h, 0, qh.shape))
        fix = jnp.where(vis, (-2.0 * _slope2(h)) * ahead, NEG_BIG)
        for n in (2 * h, 2 * h + 1):
            s_ref[n] = scores(n, q0) + fix

    @pl.when(i == 0)
    def _():
        for n in range(n_chain):
            consume(n, q0, True)

    @pl.when(i > 0)
    def _():
        for n in range(n_chain):
            consume(n, q0, True)
            s_ref[n] = scores(n, 0)

        def body(j, carry):
            k_cur = pl.multiple_of((j - 1) * tq, tq)
            k_next = pl.multiple_of(j * tq, tq)
            for n in range(n_chain):
                consume(n, k_cur, False)
                s_ref[n] = scores(n, k_next)
            return carry

        lax.fori_loop(1, i, body, 0)
        k_last = pl.multiple_of((i - 1) * tq, tq)
        for n in range(n_chain):
            consume(n, k_last, False)

    lam = _lambda(lq_ref[...], lam_init)
    for h in range(N_HEADS):
        o = [acc_ref[n, :, :HEAD_W] / acc_ref[n, :, HEAD_W:] for n in (2 * h, 2 * h + 1)]
        out_ref[0, :, h * HEAD_W:(h + 1) * HEAD_W] = _finish_head(
            o[0], o[1], lam, gs_ref[...], lam_init).astype(out_ref.dtype)


def _diff_prompt(z, k1a, k2a, vb, mem_k, mem_v, lq, g_sub, layer, lam_init, tq):
    B, T, _ = z.shape
    n_chain = 2 * N_HEADS
    diff_layer = layer - (mem_k.shape[0] - lq.shape[0])
    return pl.pallas_call(
        functools.partial(_diff_kernel, tq=tq, lam_init=lam_init),
        grid=(B, T // tq),
        in_specs=[_layer(lq, diff_layer), _layer(_rows(g_sub), diff_layer),
                  pl.BlockSpec((1, tq, MIX_W), lambda b, i: (b, i, 0)),
                  pl.BlockSpec((1, tq, MIX_W), lambda b, i: (b, i, 1)),
                  _resident((1, T, MIX_W), lambda b, i: (b, 0, 0)),
                  _resident((1, T, MIX_W), lambda b, i: (b, 0, 0)),
                  _resident((1, T, MIX_W), lambda b, i: (b, 0, 0)),
                  _batch_spec(mem_k, layer), _batch_spec(mem_v, layer)],
        out_specs=pl.BlockSpec((1, tq, 2 * MIX_W), lambda b, i: (b, i, 0)),
        out_shape=jax.ShapeDtypeStruct((B, T, 2 * MIX_W), BF16),
        scratch_shapes=[pltpu.VMEM((n_chain, tq, HEAD_W), BF16),
                        pltpu.VMEM((n_chain, tq, 2 * HEAD_W), F32),
                        pltpu.VMEM((n_chain, tq, HEAD_W), F32),
                        pltpu.VMEM((n_chain, tq, tq), F32)],
        compiler_params=_cparams("arbitrary", "arbitrary"),
        name="diff_prompt",
    )(lq, _rows(g_sub), z, z, k1a, k2a, vb, mem_k, mem_v)


def _cache_copies(hbm_refs, b, buf, slot, sem):
    return [pltpu.make_async_copy(hbm.at[b, :, h, :], buf.at[slot, t, h], sem.at[slot])
            for t, hbm in enumerate(hbm_refs) for h in range(N_HEADS)]


def _diff_sample_kernel(lq_ref, gs_ref, q_ref, qm_ref, ck_hbm, cv_hbm, nk_ref, nv_ref, mk_ref, mv_ref, out_ref,
                        cache_buf, sem, *, p0, lam_init):
    b = pl.program_id(0)
    slot = b % 2
    copies = functools.partial(_cache_copies, (ck_hbm, cv_hbm), buf=cache_buf, sem=sem)

    @pl.when(b == 0)
    def _():
        for c in copies(b, slot=slot):
            c.start()

    @pl.when(b + 1 < pl.num_programs(0))
    def _():
        for c in copies(b + 1, slot=1 - slot):
            c.start()

    for c in copies(b, slot=slot):
        c.wait()

    ts = q_ref.shape[1]
    lam = _lambda(lq_ref[...], lam_init)
    qpos_p = p0 + lax.broadcasted_iota(jnp.int32, (ts, p0), 0)
    dist_p = (qpos_p - lax.broadcasted_iota(jnp.int32, (ts, p0), 1)).astype(F32)
    row = p0 + lax.broadcasted_iota(jnp.int32, (ts, ts), 0)
    col = p0 + lax.broadcasted_iota(jnp.int32, (ts, ts), 1)
    dist_n = jnp.abs(row - col).astype(F32)
    vis_n = (col // CHUNK) <= (row // CHUNK)
    lane = lax.broadcasted_iota(jnp.int32, (ts, HEAD_W), 1)
    for h in range(N_HEADS):
        hs = slice(h * HEAD_W, (h + 1) * HEAD_W)
        qh = q_ref[0, :, hs]
        zero = jnp.zeros_like(qh)
        qc = (jnp.where(lane < DIFF_D, qh, zero), jnp.where(lane >= DIFF_D, qh, zero))
        kp = cache_buf[slot, 0, h].astype(BF16)
        vp = cache_buf[slot, 1, h].astype(BF16)
        kn = _head(nk_ref, h).astype(BF16)
        vn = _head(nv_ref, h).astype(BF16)
        bias_p = -_slope2(h) * dist_p
        bias_n = jnp.where(vis_n, -_slope2(h) * dist_n, NEG_BIG)
        o = []
        for c in range(2):
            sp = _dot_nt(qc[c], kp) + bias_p
            sn = _dot_nt(qc[c], kn) + bias_n
            m = jnp.maximum(jnp.max(sp, axis=-1, keepdims=True), jnp.max(sn, axis=-1, keepdims=True))
            ep = jnp.exp2(sp - m)
            en = jnp.exp2(sn - m)
            l = jnp.sum(ep, axis=-1, keepdims=True) + jnp.sum(en, axis=-1, keepdims=True)
            o.append((_dot(ep.astype(BF16), vp) + _dot(en.astype(BF16), vn)) / l)
        out_ref[0, :, hs] = _finish_head(o[0], o[1], lam, gs_ref[...], lam_init).astype(out_ref.dtype)
    _mem_attend(qm_ref[0], mk_ref, mv_ref, out_ref)


def _diff_sample(z, cache_k, cache_v, k_new, v_new, mem_k, mem_v, lq, g_sub, layer, lam_init):
    B, Ts, _ = z.shape
    P0 = cache_k.shape[1]
    diff_layer = layer - (mem_k.shape[0] - lq.shape[0])
    return pl.pallas_call(
        functools.partial(_diff_sample_kernel, p0=P0, lam_init=lam_init),
        grid=(B,),
        in_specs=[_layer(lq, diff_layer), _layer(_rows(g_sub), diff_layer),
                  pl.BlockSpec((1, Ts, MIX_W), lambda b: (b, 0, 0)),
                  pl.BlockSpec((1, Ts, MIX_W), lambda b: (b, 0, 1)),
                  pl.BlockSpec(memory_space=pl.ANY), pl.BlockSpec(memory_space=pl.ANY),
                  _batch_spec(k_new), _batch_spec(v_new), _batch_spec(mem_k, layer), _batch_spec(mem_v, layer)],
        out_specs=pl.BlockSpec((1, Ts, 2 * MIX_W), lambda b: (b, 0, 0)),
        out_shape=jax.ShapeDtypeStruct((B, Ts, 2 * MIX_W), BF16),
        scratch_shapes=[pltpu.VMEM((2, 2, N_HEADS, P0, HEAD_W), F32), pltpu.SemaphoreType.DMA((2,))],
        compiler_params=_cparams("arbitrary"),
        name="diff_sample",
    )(lq, _rows(g_sub), z, z, cache_k, cache_v, k_new, v_new, mem_k, mem_v)


def _lambda_init(layer_idx):
    return 0.8 - 0.6 * math.exp(-0.3 * layer_idx)


def _trunk(x, pos0, pool_hist, mem_k, mem_v, past_k, past_v, w, tm_rows, tm_seq, tq):
    B, T, D = x.shape
    R = B * T
    xr = x.reshape(R, D)
    hist16 = jnp.pad(pool_hist, ((0, 0), (HALO - POOL_HIST, 0), (0, 0)))

    cat0, tail = _mix0(x, w["g_attn"], w["w_in"], hist16, mem_k, mem_v, w["w_pool"], w["pool_scale"], 0,
                       tm_seq, pos0)
    new_pool = tail[:, HALO - POOL_HIST:]
    x1 = _outmlp(xr, cat0.reshape(R, D), w["w_out"], w["g_ffn"], w["w_ff1"], w["w_ff2"], w["g_final"], 0,
                 tm_rows, final_norm=False)

    prompt = past_k is None
    z, *bf, k, v = _proj1(x1, w["g_attn"], w["w_in"], 1, w["g_kv"], w["w_kv"], tm_rows, T, prompt)
    z3 = z.reshape(B, T, D)
    lam_init = _lambda_init(1)
    if prompt:
        k1a, k2a, vb = (a.reshape(B, T, MIX_W) for a in bf)
        cat1 = _diff_prompt(z3, k1a, k2a, vb, mem_k, mem_v, w["lambda_qk"], w["g_subln"], 1, lam_init, tq)
    else:
        shp = (B, T, N_HEADS, HEAD_W)
        cat1 = _diff_sample(z3, past_k, past_v, k.reshape(shp), v.reshape(shp), mem_k, mem_v,
                            w["lambda_qk"], w["g_subln"], 1, lam_init)
    y = _outmlp(x1, cat1.reshape(R, D), w["w_out"], w["g_ffn"], w["w_ff1"], w["w_ff2"], w["g_final"], 1,
                tm_rows, final_norm=True)
    shp = (B, T, N_HEADS, HEAD_W)
    return y.reshape(B, T, D), new_pool[None], k.reshape(shp), v.reshape(shp)


def kernel(x_prompt, x_sample, mem_prompt, cache_k, cache_v, cache_mem_k, cache_mem_v, state_pool, g_attn, w_in, w_out, g_mem, w_mem_kv, g_ffn, w_ff1, w_ff2, w_pool, pool_scale, lambda_qk, g_subln, g_kv, w_kv, g_final):
    B, T, D = x_prompt.shape
    Bs, Ts, _ = x_sample.shape
    P0 = cache_k.shape[1]
    w = dict(g_attn=g_attn, w_in=w_in.astype(BF16), w_out=w_out.astype(BF16), g_ffn=g_ffn,
             w_ff1=w_ff1.astype(BF16), w_ff2=w_ff2.astype(BF16), w_pool=w_pool.astype(BF16),
             pool_scale=pool_scale, lambda_qk=lambda_qk, g_subln=g_subln, g_kv=g_kv,
             w_kv=w_kv.astype(BF16), g_final=g_final)

    mem_k_p, mem_v_p, mem_k_d, mem_v_d = _mem_kv(mem_prompt, g_mem, w_mem_kv.astype(BF16))
    hist0 = jnp.zeros((B, POOL_HIST, MIX_W), F32)
    y_p, pool_p, k_p, v_p = _trunk(x_prompt, 0, hist0, mem_k_d, mem_v_d, None, None, w,
                                   tm_rows=512, tm_seq=512, tq=512)
    y_s, pool_s, k_s, v_s = _trunk(x_sample, P0, state_pool[0], cache_mem_k, cache_mem_v, cache_k, cache_v, w,
                                   tm_rows=Bs * Ts, tm_seq=Ts, tq=None)
    return (y_p, y_s, mem_k_p, mem_v_p, pool_p, k_p, v_p, pool_s, k_s, v_s)
```

```python
import functools
import math

import jax
import jax.numpy as jnp
import ml_dtypes
import numpy as np
from jax import lax
from jax.experimental import pallas as pl
from jax.experimental.pallas import tpu as pltpu

F32 = jnp.float32
BF16 = jnp.bfloat16

CHUNK = 64
POOL_WINDOWS = (2, 4, 8, 16)
POOL_HIST = 15
HALO = 16
N_HEADS = 4
HEAD_W = 128
DIFF_D = 64
MIX_W = N_HEADS * HEAD_W
EPS = 1e-6
NEG_BIG = -1e30
LOG2E = math.log2(math.e)
Q_SCALE = DIFF_D ** -0.5 * LOG2E
N_SPLIT = 4
VMEM_LIMIT = 56 * 1024 * 1024


def _cparams(*sem):
    return pltpu.CompilerParams(dimension_semantics=sem, vmem_limit_bytes=VMEM_LIMIT)


def _resident(shape, index_map):
    return pl.BlockSpec(shape, index_map, pipeline_mode=pl.Buffered(1))


def _layer(arr, layer):
    zeros = (0,) * (arr.ndim - 1)
    return pl.BlockSpec((None,) + arr.shape[1:], lambda *_: (layer,) + zeros, pipeline_mode=pl.Buffered(1))


def _rows(g):
    return g.reshape(g.shape[0], 1, g.shape[1])


def _inv_rms(x):
    return lax.rsqrt(jnp.mean(x * x, axis=-1, keepdims=True) + EPS)


def _dot(a, b):
    return jnp.dot(a, b, preferred_element_type=F32)


def _dot_nt(a, b):
    return lax.dot_general(a, b, (((1,), (1,)), ((), ())), preferred_element_type=F32)


def _memkv_kernel(x_ref, g_ref, w_ref, k_ref, v_ref, kd_ref, vd_ref):
    x = x_ref[0]
    h = (x * _inv_rms(x) * g_ref[0]).astype(BF16)
    kv = _dot(h, w_ref[0])
    kd_ref[0, 0] = kv[:, :MIX_W]
    vd_ref[0, 0] = kv[:, MIX_W:]
    for hd in range(N_HEADS):
        k_ref[0, 0, :, hd, :] = kv[:, hd * HEAD_W:(hd + 1) * HEAD_W]
        v_ref[0, 0, :, hd, :] = kv[:, MIX_W + hd * HEAD_W:MIX_W + (hd + 1) * HEAD_W]


def _mem_kv(mem, g_mem, w_mem_kv):
    B, M, D = mem.shape
    L = g_mem.shape[0]
    out = jax.ShapeDtypeStruct((L, B, M, N_HEADS, HEAD_W), F32)
    dense = jax.ShapeDtypeStruct((L, B, M, MIX_W), F32)
    return pl.pallas_call(
        _memkv_kernel,
        grid=(L, B),
        in_specs=[pl.BlockSpec((1, M, D), lambda l, b: (b, 0, 0)),
                  pl.BlockSpec((1, 1, D), lambda l, b: (l, 0, 0)),
                  pl.BlockSpec((1, D, 2 * MIX_W), lambda l, b: (l, 0, 0))],
        out_specs=[pl.BlockSpec((1, 1, M, N_HEADS, HEAD_W), lambda l, b: (l, b, 0, 0, 0))] * 2
                  + [pl.BlockSpec((1, 1, M, MIX_W), lambda l, b: (l, b, 0, 0))] * 2,
        out_shape=[out, out, dense, dense],
        compiler_params=_cparams("arbitrary", "arbitrary"),
        name="mem_kv",
    )(mem, g_mem.reshape(L, 1, D), w_mem_kv)


def _batch_spec(arr, layer=None):
    if layer is None:
        zeros = (0,) * (arr.ndim - 1)
        return pl.BlockSpec((1,) + arr.shape[1:], lambda b, *_: (b,) + zeros)
    zeros = (0,) * (arr.ndim - 2)
    return pl.BlockSpec((None, 1) + arr.shape[2:], lambda b, *_: (layer, b) + zeros)


def _head(ref, h):
    if len(ref.shape) == 3:
        return ref[0, :, h * HEAD_W:(h + 1) * HEAD_W]
    return ref[0, :, h, :]


def _mem_attend(qm, mk_ref, mv_ref, out_ref):
    scale = HEAD_W ** -0.5
    for h in range(N_HEADS):
        hs = slice(h * HEAD_W, (h + 1) * HEAD_W)
        kh = _head(mk_ref, h).astype(BF16)
        vh = _head(mv_ref, h).astype(BF16)
        s = _dot_nt(qm[:, hs], kh) * scale
        m = jnp.max(s, axis=-1, keepdims=True)
        e = jnp.exp(s - m)
        l = jnp.sum(e, axis=-1, keepdims=True)
        o = _dot(e.astype(BF16), vh) / l
        out_ref[0, :, MIX_W + h * HEAD_W:MIX_W + (h + 1) * HEAD_W] = o.astype(out_ref.dtype)


def _mix0_kernel(x_ref, g_ref, wi_ref, hist_ref, wp_ref, ps_ref, mk_ref, mv_ref, out_ref, tail_ref, carry_ref,
                 *, tm, pos0):
    i = pl.program_id(1)

    @pl.when(i == 0)
    def _():
        carry_ref[...] = hist_ref[0]

    x = x_ref[0]
    z = _dot((x * _inv_rms(x) * g_ref[...]).astype(BF16), wi_ref[...])
    u = z[:, :MIX_W]
    qm = z[:, MIX_W:].astype(BF16)
    ext = jnp.concatenate([carry_ref[...], u], axis=0)
    carry_ref[...] = u[tm - HALO:]
    tail_ref[0] = u[tm - HALO:]
    pos = pos0 + i * tm + lax.broadcasted_iota(jnp.int32, (tm, HEAD_W), 0)
    for g, w in enumerate(POOL_WINDOWS):
        gs = slice(g * HEAD_W, (g + 1) * HEAD_W)
        acc = ext[:, gs]
        span = 1
        while span < w:
            acc = acc + pltpu.roll(acc, span, 0)
            span *= 2
        cnt = jnp.minimum(pos + 1, w).astype(F32)
        d = acc[HALO:] / cnt - u[:, gs]
        t = _dot(d.astype(BF16), wp_ref[g]) * ps_ref[:, gs]
        out_ref[0, :, gs] = t.astype(out_ref.dtype)
    _mem_attend(qm, mk_ref, mv_ref, out_ref)


def _mix0(x, g_attn, w_in, hist16, mem_k, mem_v, w_pool, pool_scale, layer, tm, pos0):
    B, T, D = x.shape
    assert T % tm == 0 and tm >= HALO
    return pl.pallas_call(
        functools.partial(_mix0_kernel, tm=tm, pos0=pos0),
        grid=(B, T // tm),
        in_specs=[pl.BlockSpec((1, tm, D), lambda b, i: (b, i, 0)),
                  _layer(_rows(g_attn), layer), _layer(w_in, layer),
                  pl.BlockSpec((1, HALO, MIX_W), lambda b, i: (b, 0, 0)),
                  _layer(w_pool, layer), _layer(_rows(pool_scale), layer),
                  _batch_spec(mem_k, layer), _batch_spec(mem_v, layer)],
        out_specs=[pl.BlockSpec((1, tm, 2 * MIX_W), lambda b, i: (b, i, 0)),
                   pl.BlockSpec((1, HALO, MIX_W), lambda b, i: (b, 0, 0))],
        out_shape=[jax.ShapeDtypeStruct((B, T, 2 * MIX_W), BF16),
                   jax.ShapeDtypeStruct((B, HALO, MIX_W), F32)],
        scratch_shapes=[pltpu.VMEM((HALO, MIX_W), F32)],
        compiler_params=_cparams("arbitrary", "arbitrary"),
        name="mix0",
    )(x, _rows(g_attn), w_in, hist16, w_pool, _rows(pool_scale), mem_k, mem_v)


def _lane_window(lane, lo):
    return (lane >= lo) & (lane < lo + N_SPLIT)


def _head_copies(buf, slot, hbm_refs, row0, tm, sem):
    out = []
    for t, hbm in enumerate(hbm_refs):
        for h in range(N_HEADS):
            src = buf.at[slot, :, pl.ds(t * MIX_W + h * HEAD_W, HEAD_W)]
            out.append(pltpu.make_async_copy(src, hbm.at[pl.ds(row0, tm), h, :], sem.at[slot]))
    return out


def _project_layer1(x, ga_ref, wi_ref, gk_ref, wk_ref, z_ref, bf_refs, k_hbm, v_hbm, kv_buf, sem, *, seq_len):
    tm = x.shape[0]
    i = pl.program_id(0)
    n_steps = pl.num_programs(0)
    slot = i % 2
    copies = functools.partial(_head_copies, kv_buf, hbm_refs=(k_hbm, v_hbm), tm=tm, sem=sem)

    xn = x * _inv_rms(x)
    z = _dot((xn * ga_ref[...]).astype(BF16), wi_ref[...])
    z_ref[:, :MIX_W] = (z[:, :MIX_W] * Q_SCALE).astype(BF16)
    z_ref[:, MIX_W:] = z[:, MIX_W:].astype(BF16)
    kv = _dot((xn * gk_ref[...]).astype(BF16), wk_ref[...])

    @pl.when(i >= 2)
    def _():
        for c in copies(slot, row0=(i - 2) * tm):
            c.wait()

    kv_buf[slot] = kv
    for c in copies(slot, row0=i * tm):
        c.start()

    @pl.when(i == n_steps - 1)
    def _():
        @pl.when(i >= 1)
        def _():
            for c in copies(1 - slot, row0=(i - 1) * tm):
                c.wait()
        for c in copies(slot, row0=i * tm):
            c.wait()

    if bf_refs:
        k = kv[:, :MIX_W]
        v = kv[:, MIX_W:]
        k1a_ref, k2a_ref, vb_ref = bf_refs
        lane = lax.broadcasted_iota(jnp.int32, (tm, HEAD_W), 1)
        pos = (i * tm) % seq_len + lax.broadcasted_iota(jnp.int32, (tm, HEAD_W), 0)
        off = pos % CHUNK
        base = (pos - off).astype(F32)
        off = off.astype(F32)

        def pos_lanes(lo):
            return jnp.where(_lane_window(lane, lo), base, jnp.where(_lane_window(lane, lo + N_SPLIT), off, 0.0))

        pos_hi = pos_lanes(DIFF_D)
        pos_lo = pos_lanes(0)
        for h in range(N_HEADS):
            hs = slice(h * HEAD_W, (h + 1) * HEAD_W)
            k1a_ref[:, hs] = jnp.where(lane < DIFF_D, k[:, hs], pos_hi).astype(BF16)
            k2a_ref[:, hs] = jnp.where(lane >= DIFF_D, k[:, hs], pos_lo).astype(BF16)
        vb_ref[...] = v.astype(BF16)


def _outmlp_kernel(x_ref, cat_ref, wo_ref, g_ref, w1_ref, w2_ref, gf_ref, *rest, final_norm, ff_chunk, n_bf, seq_len):
    if final_norm:
        out_ref, a_ref = rest
    else:
        ga_ref, wi_ref, gk_ref, wk_ref, out_ref, z_ref = rest[:6]
        bf_refs = rest[6:6 + n_bf]
        k_hbm, v_hbm, a_ref, kv_buf, sem = rest[6 + n_bf:]
    x1 = x_ref[...] + _dot(cat_ref[...], wo_ref[...])
    h = (x1 * _inv_rms(x1) * g_ref[...]).astype(BF16)
    for c in range(w1_ref.shape[1] // ff_chunk):
        cs = slice(c * ff_chunk, (c + 1) * ff_chunk)
        a = jnp.maximum(_dot(h, w1_ref[:, cs]), 0.0)
        a_ref[:, cs] = (a * a).astype(BF16)
    x2 = x1 + _dot(a_ref[...], w2_ref[...])
    if final_norm:
        out_ref[...] = x2 * _inv_rms(x2) * gf_ref[...]
    else:
        out_ref[...] = x2
        _project_layer1(x2, ga_ref, wi_ref, gk_ref, wk_ref, z_ref, bf_refs, k_hbm, v_hbm, kv_buf, sem,
                        seq_len=seq_len)


def _outmlp(x, cat, w, layer, tm, final_norm, seq_len=None, attn_operands=False):
    R, D = x.shape
    FF = w["w_ff1"].shape[-1]
    row = lambda i: (i, 0)
    fix = lambda i: (0, 0)
    n_bf = 3 if attn_operands else 0
    in_specs = [pl.BlockSpec((tm, D), row), pl.BlockSpec((tm, D), row),
                _layer(w["w_out"], layer), _layer(_rows(w["g_ffn"]), layer),
                _layer(w["w_ff1"], layer), _layer(w["w_ff2"], layer), _resident((1, D), fix)]
    args = [x, cat, w["w_out"], _rows(w["g_ffn"]), w["w_ff1"], w["w_ff2"], w["g_final"].reshape(1, D)]
    out_specs = [pl.BlockSpec((tm, D), row)]
    out_shape = [jax.ShapeDtypeStruct((R, D), F32)]
    scratch = [pltpu.VMEM((tm, FF), BF16)]
    if not final_norm:
        in_specs += [_layer(_rows(w["g_attn"]), layer + 1), _layer(w["w_in"], layer + 1),
                     _resident((1, D), fix), _resident((D, 2 * MIX_W), fix)]
        args += [_rows(w["g_attn"]), w["w_in"], w["g_kv"].reshape(1, D), w["w_kv"]]
        out_specs += ([pl.BlockSpec((tm, 2 * MIX_W), row)] + [pl.BlockSpec((tm, MIX_W), row)] * n_bf
                      + [pl.BlockSpec(memory_space=pl.ANY)] * 2)
        out_shape += ([jax.ShapeDtypeStruct((R, 2 * MIX_W), BF16)] + [jax.ShapeDtypeStruct((R, MIX_W), BF16)] * n_bf
                      + [jax.ShapeDtypeStruct((R, N_HEADS, HEAD_W), F32)] * 2)
        scratch += [pltpu.VMEM((2, tm, 2 * MIX_W), F32), pltpu.SemaphoreType.DMA((2,))]
    return pl.pallas_call(
        functools.partial(_outmlp_kernel, final_norm=final_norm, ff_chunk=1024, n_bf=n_bf, seq_len=seq_len),
        grid=(R // tm,),
        in_specs=in_specs,
        out_specs=out_specs,
        out_shape=out_shape,
        scratch_shapes=scratch,
        compiler_params=_cparams("arbitrary"),
        name="outmlp",
    )(*args)


def _lambda(lq, lam_init):
    a = jnp.sum(lq[0:1] * lq[1:2], axis=-1, keepdims=True)
    b = jnp.sum(lq[2:3] * lq[3:4], axis=-1, keepdims=True)
    return jnp.exp(a) - jnp.exp(b) + lam_init


def _slope2(h):
    return 2.0 ** (-8.0 * (h + 1) / N_HEADS) * LOG2E


def _bf16_pieces(c):
    out = []
    for _ in range(N_SPLIT):
        p = float(np.asarray(c, dtype=ml_dtypes.bfloat16))
        out.append(p)
        c -= p
    return out


def _slope_lanes(h, lo, shape):
    lane = lax.broadcasted_iota(jnp.int32, shape, 1)
    out = jnp.zeros(shape, F32)
    for p, piece in enumerate(_bf16_pieces(_slope2(h))):
        out = jnp.where((lane == lo + p) | (lane == lo + N_SPLIT + p), piece, out)
    return out.astype(BF16)


def _finish_head(o1, o2, lam, gs, lam_init):
    o = o1 - lam * o2
    return o * _inv_rms(o) * gs * (1.0 - lam_init)


def _diff_kernel(lq_ref, gs_ref, q_ref, qm_ref, k1_ref, k2_ref, v_ref, mk_ref, mv_ref, out_ref,
                 qa_ref, acc_ref, m_ref, s_ref, *, tq, lam_init):
    i = pl.program_id(1)
    nblk = tq // HEAD_W
    n_chain = 2 * N_HEADS
    k_refs = (k1_ref, k2_ref)
    row = lax.broadcasted_iota(jnp.int32, (tq, tq), 0)
    col = lax.broadcasted_iota(jnp.int32, (tq, tq), 1)
    ahead = jnp.maximum(col - row, 0).astype(F32)
    vis = (col // CHUNK) <= (row // CHUNK)
    ones = jnp.ones((tq, HEAD_W), BF16)
    lane = lax.broadcasted_iota(jnp.int32, (tq, HEAD_W), 1)
    q0 = pl.multiple_of(i * tq, tq)

    def hslice(n):
        return slice((n // 2) * HEAD_W, (n // 2 + 1) * HEAD_W)

    def scores(n, k0):
        return _dot_nt(qa_ref[n], k_refs[n % 2][0, pl.ds(k0, tq), hslice(n)])

    def v_aug(n, k0):
        return jnp.concatenate([v_ref[0, pl.ds(k0, tq), hslice(n)], ones], axis=1)

    def consume(n, k0, first):
        vt = v_aug(n, k0)
        blocks = [s_ref[n, :, b * HEAD_W:(b + 1) * HEAD_W] for b in range(nblk)]
        m_row = jnp.max(functools.reduce(jnp.maximum, blocks), axis=-1, keepdims=True)
        if first:
            m_new = jnp.broadcast_to(m_row, (tq, HEAD_W))
        else:
            m_old = m_ref[n]
            m_new = jnp.maximum(m_old, m_row)
        p = jnp.concatenate([jnp.exp2(b - m_new) for b in blocks], axis=1).astype(BF16)
        pv = _dot(p, vt)
        if first:
            acc_ref[n] = pv
        else:
            alpha = jnp.exp2(m_old - m_new)
            acc_ref[n] = jnp.concatenate([alpha, alpha], axis=1) * acc_ref[n] + pv
        m_ref[n] = m_new

    _mem_attend(qm_ref[0], mk_ref, mv_ref, out_ref)
    for h in range(N_HEADS):
        qh = q_ref[0, :, h * HEAD_W:(h + 1) * HEAD_W]
        qa_ref[2 * h] = jnp.where(lane < DIFF_D, qh, _slope_lanes(h, DIFF_D, qh.shape))
        qa_ref[2 * h + 1] = jnp.where(lane >= DIFF_D, qh, _slope_lanes(h, 0, qh.shape))
        fix = jnp.where(vis, (-2.0 * _slope2(h)) * ahead, NEG_BIG)
        for n in (2 * h, 2 * h + 1):
            s_ref[n] = scores(n, q0) + fix

    @pl.when(i == 0)
    def _():
        for n in range(n_chain):
            consume(n, q0, True)

    @pl.when(i > 0)
    def _():
        for n in range(n_chain):
            consume(n, q0, True)
            s_ref[n] = scores(n, 0)

        def body(j, carry):
            k_cur = pl.multiple_of((j - 1) * tq, tq)
            k_next = pl.multiple_of(j * tq, tq)
            for n in range(n_chain):
                consume(n, k_cur, False)
                s_ref[n] = scores(n, k_next)
            return carry

        lax.fori_loop(1, i, body, 0)
        k_last = pl.multiple_of((i - 1) * tq, tq)
        for n in range(n_chain):
            consume(n, k_last, False)

    lam = _lambda(lq_ref[...], lam_init)
    for h in range(N_HEADS):
        o = [acc_ref[n, :, :HEAD_W] / acc_ref[n, :, HEAD_W:] for n in (2 * h, 2 * h + 1)]
        out_ref[0, :, h * HEAD_W:(h + 1) * HEAD_W] = _finish_head(
            o[0], o[1], lam, gs_ref[...], lam_init).astype(out_ref.dtype)


def _diff_prompt(z, k1a, k2a, vb, mem_k, mem_v, lq, g_sub, layer, lam_init, tq):
    B, T, _ = z.shape
    n_chain = 2 * N_HEADS
    diff_layer = layer - (mem_k.shape[0] - lq.shape[0])
    return pl.pallas_call(
        functools.partial(_diff_kernel, tq=tq, lam_init=lam_init),
        grid=(B, T // tq),
        in_specs=[_layer(lq, diff_layer), _layer(_rows(g_sub), diff_layer),
                  pl.BlockSpec((1, tq, MIX_W), lambda b, i: (b, i, 0)),
                  pl.BlockSpec((1, tq, MIX_W), lambda b, i: (b, i, 1)),
                  _resident((1, T, MIX_W), lambda b, i: (b, 0, 0)),
                  _resident((1, T, MIX_W), lambda b, i: (b, 0, 0)),
                  _resident((1, T, MIX_W), lambda b, i: (b, 0, 0)),
                  _batch_spec(mem_k, layer), _batch_spec(mem_v, layer)],
        out_specs=pl.BlockSpec((1, tq, 2 * MIX_W), lambda b, i: (b, i, 0)),
        out_shape=jax.ShapeDtypeStruct((B, T, 2 * MIX_W), BF16),
        scratch_shapes=[pltpu.VMEM((n_chain, tq, HEAD_W), BF16),
                        pltpu.VMEM((n_chain, tq, 2 * HEAD_W), F32),
                        pltpu.VMEM((n_chain, tq, HEAD_W), F32),
                        pltpu.VMEM((n_chain, tq, tq), F32)],
        compiler_params=_cparams("arbitrary", "arbitrary"),
        name="diff_prompt",
    )(lq, _rows(g_sub), z, z, k1a, k2a, vb, mem_k, mem_v)


def _cache_copies(hbm_refs, b, buf, slot, sem):
    return [pltpu.make_async_copy(hbm.at[b, :, h, :], buf.at[slot, t, h], sem.at[slot])
            for t, hbm in enumerate(hbm_refs) for h in range(N_HEADS)]


def _diff_sample_kernel(lq_ref, gs_ref, q_ref, qm_ref, ck_hbm, cv_hbm, nk_ref, nv_ref, mk_ref, mv_ref, out_ref,
                        cache_buf, sem, *, p0, lam_init):
    b = pl.program_id(0)
    slot = b % 2
    copies = functools.partial(_cache_copies, (ck_hbm, cv_hbm), buf=cache_buf, sem=sem)

    @pl.when(b == 0)
    def _():
        for c in copies(b, slot=slot):
            c.start()

    @pl.when(b + 1 < pl.num_programs(0))
    def _():
        for c in copies(b + 1, slot=1 - slot):
            c.start()

    for c in copies(b, slot=slot):
        c.wait()

    ts = q_ref.shape[1]
    lam = _lambda(lq_ref[...], lam_init)
    qpos_p = p0 + lax.broadcasted_iota(jnp.int32, (ts, p0), 0)
    dist_p = (qpos_p - lax.broadcasted_iota(jnp.int32, (ts, p0), 1)).astype(F32)
    row = p0 + lax.broadcasted_iota(jnp.int32, (ts, ts), 0)
    col = p0 + lax.broadcasted_iota(jnp.int32, (ts, ts), 1)
    dist_n = jnp.abs(row - col).astype(F32)
    vis_n = (col // CHUNK) <= (row // CHUNK)
    lane = lax.broadcasted_iota(jnp.int32, (ts, HEAD_W), 1)
    for h in range(N_HEADS):
        hs = slice(h * HEAD_W, (h + 1) * HEAD_W)
        qh = q_ref[0, :, hs]
        zero = jnp.zeros_like(qh)
        qc = (jnp.where(lane < DIFF_D, qh, zero), jnp.where(lane >= DIFF_D, qh, zero))
        kp = cache_buf[slot, 0, h].astype(BF16)
        vp = cache_buf[slot, 1, h].astype(BF16)
        kn = _head(nk_ref, h).astype(BF16)
        vn = _head(nv_ref, h).astype(BF16)
        bias_p = -_slope2(h) * dist_p
        bias_n = jnp.where(vis_n, -_slope2(h) * dist_n, NEG_BIG)
        o = []
        for c in range(2):
            sp = _dot_nt(qc[c], kp) + bias_p
            sn = _dot_nt(qc[c], kn) + bias_n
            m = jnp.maximum(jnp.max(sp, axis=-1, keepdims=True), jnp.max(sn, axis=-1, keepdims=True))
            ep = jnp.exp2(sp - m)
            en = jnp.exp2(sn - m)
            l = jnp.sum(ep, axis=-1, keepdims=True) + jnp.sum(en, axis=-1, keepdims=True)
            o.append((_dot(ep.astype(BF16), vp) + _dot(en.astype(BF16), vn)) / l)
        out_ref[0, :, hs] = _finish_head(o[0], o[1], lam, gs_ref[...], lam_init).astype(out_ref.dtype)
    _mem_attend(qm_ref[0], mk_ref, mv_ref, out_ref)


def _diff_sample(z, cache_k, cache_v, k_new, v_new, mem_k, mem_v, lq, g_sub, layer, lam_init):
    B, Ts, _ = z.shape
    P0 = cache_k.shape[1]
    diff_layer = layer - (mem_k.shape[0] - lq.shape[0])
    return pl.pallas_call(
        functools.partial(_diff_sample_kernel, p0=P0, lam_init=lam_init),
        grid=(B,),
        in_specs=[_layer(lq, diff_layer), _layer(_rows(g_sub), diff_layer),
                  pl.BlockSpec((1, Ts, MIX_W), lambda b: (b, 0, 0)),
                  pl.BlockSpec((1, Ts, MIX_W), lambda b: (b, 0, 1)),
                  pl.BlockSpec(memory_space=pl.ANY), pl.BlockSpec(memory_space=pl.ANY),
                  _batch_spec(k_new), _batch_spec(v_new), _batch_spec(mem_k, layer), _batch_spec(mem_v, layer)],
        out_specs=pl.BlockSpec((1, Ts, 2 * MIX_W), lambda b: (b, 0, 0)),
        out_shape=jax.ShapeDtypeStruct((B, Ts, 2 * MIX_W), BF16),
        scratch_shapes=[pltpu.VMEM((2, 2, N_HEADS, P0, HEAD_W), F32), pltpu.SemaphoreType.DMA((2,))],
        compiler_params=_cparams("arbitrary"),
        name="diff_sample",
    )(lq, _rows(g_sub), z, z, cache_k, cache_v, k_new, v_new, mem_k, mem_v)


def _lambda_init(layer_idx):
    return 0.8 - 0.6 * math.exp(-0.3 * layer_idx)


def _trunk(x, pos0, pool_hist, mem_k, mem_v, past_k, past_v, w, tm_rows, tm_seq, tq):
    B, T, D = x.shape
    R = B * T
    xr = x.reshape(R, D)
    hist16 = jnp.pad(pool_hist, ((0, 0), (HALO - POOL_HIST, 0), (0, 0)))

    cat0, tail = _mix0(x, w["g_attn"], w["w_in"], hist16, mem_k, mem_v, w["w_pool"], w["pool_scale"], 0,
                       tm_seq, pos0)
    new_pool = tail[:, HALO - POOL_HIST:]
    prompt = past_k is None
    x1, z, *bf, k, v = _outmlp(xr, cat0.reshape(R, D), w, 0, tm_rows, final_norm=False, seq_len=T,
                               attn_operands=prompt)
    z3 = z.reshape(B, T, D)
    lam_init = _lambda_init(1)
    if prompt:
        k1a, k2a, vb = (a.reshape(B, T, MIX_W) for a in bf)
        cat1 = _diff_prompt(z3, k1a, k2a, vb, mem_k, mem_v, w["lambda_qk"], w["g_subln"], 1, lam_init, tq)
    else:
        shp = (B, T, N_HEADS, HEAD_W)
        cat1 = _diff_sample(z3, past_k, past_v, k.reshape(shp), v.reshape(shp), mem_k, mem_v,
                            w["lambda_qk"], w["g_subln"], 1, lam_init)
    y, = _outmlp(x1, cat1.reshape(R, D), w, 1, tm_rows, final_norm=True)
    shp = (B, T, N_HEADS, HEAD_W)
    return y.reshape(B, T, D), new_pool[None], k.reshape(shp), v.reshape(shp)


def kernel(x_prompt, x_sample, mem_prompt, cache_k, cache_v, cache_mem_k, cache_mem_v, state_pool, g_attn, w_in, w_out, g_mem, w_mem_kv, g_ffn, w_ff1, w_ff2, w_pool, pool_scale, lambda_qk, g_subln, g_kv, w_kv, g_final):
    B, T, D = x_prompt.shape
    Bs, Ts, _ = x_sample.shape
    P0 = cache_k.shape[1]
    w = dict(g_attn=g_attn, w_in=w_in.astype(BF16), w_out=w_out.astype(BF16), g_ffn=g_ffn,
             w_ff1=w_ff1.astype(BF16), w_ff2=w_ff2.astype(BF16), w_pool=w_pool.astype(BF16),
             pool_scale=pool_scale, lambda_qk=lambda_qk, g_subln=g_subln, g_kv=g_kv,
             w_kv=w_kv.astype(BF16), g_final=g_final)

    mem_k_p, mem_v_p, mem_k_d, mem_v_d = _mem_kv(mem_prompt, g_mem, w_mem_kv.astype(BF16))
    hist0 = jnp.zeros((B, POOL_HIST, MIX_W), F32)
    y_p, pool_p, k_p, v_p = _trunk(x_prompt, 0, hist0, mem_k_d, mem_v_d, None, None, w,
                                   tm_rows=512, tm_seq=512, tq=512)
    y_s, pool_s, k_s, v_s = _trunk(x_sample, P0, state_pool[0], cache_mem_k, cache_mem_v, cache_k, cache_v, w,
                                   tm_rows=Bs * Ts, tm_seq=Ts, tq=None)
    return (y_p, y_s, mem_k_p, mem_v_p, pool_p, k_p, v_p, pool_s, k_s, v_s)
```

```python
import functools
import math

import jax
import jax.numpy as jnp
import ml_dtypes
import numpy as np
from jax import lax
from jax.experimental import pallas as pl
from jax.experimental.pallas import tpu as pltpu

F32 = jnp.float32
BF16 = jnp.bfloat16

CHUNK = 64
POOL_WINDOWS = (2, 4, 8, 16)
POOL_HIST = 15
HALO = 16
N_HEADS = 4
HEAD_W = 128
DIFF_D = 64
MIX_W = N_HEADS * HEAD_W
EPS = 1e-6
NEG_BIG = -1e30
LOG2E = math.log2(math.e)
Q_SCALE = DIFF_D ** -0.5 * LOG2E
N_SPLIT = 4
VMEM_LIMIT = 56 * 1024 * 1024


def _cparams(*sem):
    return pltpu.CompilerParams(dimension_semantics=sem, vmem_limit_bytes=VMEM_LIMIT)


def _resident(shape, index_map):
    return pl.BlockSpec(shape, index_map, pipeline_mode=pl.Buffered(1))


def _layer(arr, layer):
    zeros = (0,) * (arr.ndim - 1)
    return pl.BlockSpec((None,) + arr.shape[1:], lambda *_: (layer,) + zeros, pipeline_mode=pl.Buffered(1))


def _rows(g):
    return g.reshape(g.shape[0], 1, g.shape[1])


def _inv_rms(x):
    return lax.rsqrt(jnp.mean(x * x, axis=-1, keepdims=True) + EPS)


def _dot(a, b):
    return jnp.dot(a, b, preferred_element_type=F32)


def _dot_nt(a, b):
    return lax.dot_general(a, b, (((1,), (1,)), ((), ())), preferred_element_type=F32)


def _memkv_kernel(x_ref, g_ref, w_ref, k_ref, v_ref, kd_ref, vd_ref):
    x = x_ref[0]
    h = (x * _inv_rms(x) * g_ref[0]).astype(BF16)
    kv = _dot(h, w_ref[0])
    kd_ref[0, 0] = kv[:, :MIX_W]
    vd_ref[0, 0] = kv[:, MIX_W:]
    for hd in range(N_HEADS):
        k_ref[0, 0, :, hd, :] = kv[:, hd * HEAD_W:(hd + 1) * HEAD_W]
        v_ref[0, 0, :, hd, :] = kv[:, MIX_W + hd * HEAD_W:MIX_W + (hd + 1) * HEAD_W]


def _mem_kv(mem, g_mem, w_mem_kv):
    B, M, D = mem.shape
    L = g_mem.shape[0]
    out = jax.ShapeDtypeStruct((L, B, M, N_HEADS, HEAD_W), F32)
    dense = jax.ShapeDtypeStruct((L, B, M, MIX_W), F32)
    return pl.pallas_call(
        _memkv_kernel,
        grid=(L, B),
        in_specs=[pl.BlockSpec((1, M, D), lambda l, b: (b, 0, 0)),
                  pl.BlockSpec((1, 1, D), lambda l, b: (l, 0, 0)),
                  pl.BlockSpec((1, D, 2 * MIX_W), lambda l, b: (l, 0, 0))],
        out_specs=[pl.BlockSpec((1, 1, M, N_HEADS, HEAD_W), lambda l, b: (l, b, 0, 0, 0))] * 2
                  + [pl.BlockSpec((1, 1, M, MIX_W), lambda l, b: (l, b, 0, 0))] * 2,
        out_shape=[out, out, dense, dense],
        compiler_params=_cparams("arbitrary", "arbitrary"),
        name="mem_kv",
    )(mem, g_mem.reshape(L, 1, D), w_mem_kv)


def _batch_spec(arr, layer=None):
    if layer is None:
        zeros = (0,) * (arr.ndim - 1)
        return pl.BlockSpec((1,) + arr.shape[1:], lambda b, *_: (b,) + zeros)
    zeros = (0,) * (arr.ndim - 2)
    return pl.BlockSpec((None, 1) + arr.shape[2:], lambda b, *_: (layer, b) + zeros)


def _head(ref, h):
    if len(ref.shape) == 3:
        return ref[0, :, h * HEAD_W:(h + 1) * HEAD_W]
    return ref[0, :, h, :]


def _mem_attend(qm, mk_ref, mv_ref, out_ref):
    scale = HEAD_W ** -0.5
    for h in range(N_HEADS):
        hs = slice(h * HEAD_W, (h + 1) * HEAD_W)
        kh = _head(mk_ref, h).astype(BF16)
        vh = _head(mv_ref, h).astype(BF16)
        s = _dot_nt(qm[:, hs], kh) * scale
        m = jnp.max(s, axis=-1, keepdims=True)
        e = jnp.exp(s - m)
        l = jnp.sum(e, axis=-1, keepdims=True)
        o = _dot(e.astype(BF16), vh) / l
        out_ref[0, :, MIX_W + h * HEAD_W:MIX_W + (h + 1) * HEAD_W] = o.astype(out_ref.dtype)


def _mix0_kernel(x_ref, g_ref, wi_ref, hist_ref, wp_ref, ps_ref, mk_ref, mv_ref, out_ref, tail_ref, carry_ref,
                 *, tm, pos0):
    i = pl.program_id(1)

    @pl.when(i == 0)
    def _():
        carry_ref[...] = hist_ref[0]

    x = x_ref[0]
    z = _dot((x * _inv_rms(x) * g_ref[...]).astype(BF16), wi_ref[...])
    u = z[:, :MIX_W]
    qm = z[:, MIX_W:].astype(BF16)
    ext = jnp.concatenate([carry_ref[...], u], axis=0)
    carry_ref[...] = u[tm - HALO:]
    tail_ref[0] = u[tm - HALO:]
    pos = pos0 + i * tm + lax.broadcasted_iota(jnp.int32, (tm, HEAD_W), 0)
    for g, w in enumerate(POOL_WINDOWS):
        gs = slice(g * HEAD_W, (g + 1) * HEAD_W)
        acc = ext[:, gs]
        span = 1
        while span < w:
            acc = acc + pltpu.roll(acc, span, 0)
            span *= 2
        cnt = jnp.minimum(pos + 1, w).astype(F32)
        d = acc[HALO:] / cnt - u[:, gs]
        t = _dot(d.astype(BF16), wp_ref[g]) * ps_ref[:, gs]
        out_ref[0, :, gs] = t.astype(out_ref.dtype)
    _mem_attend(qm, mk_ref, mv_ref, out_ref)


def _mix0(x, g_attn, w_in, hist16, mem_k, mem_v, w_pool, pool_scale, layer, tm, pos0):
    B, T, D = x.shape
    assert T % tm == 0 and tm >= HALO
    return pl.pallas_call(
        functools.partial(_mix0_kernel, tm=tm, pos0=pos0),
        grid=(B, T // tm),
        in_specs=[pl.BlockSpec((1, tm, D), lambda b, i: (b, i, 0)),
                  _layer(_rows(g_attn), layer), _layer(w_in, layer),
                  pl.BlockSpec((1, HALO, MIX_W), lambda b, i: (b, 0, 0)),
                  _layer(w_pool, layer), _layer(_rows(pool_scale), layer),
                  _batch_spec(mem_k, layer), _batch_spec(mem_v, layer)],
        out_specs=[pl.BlockSpec((1, tm, 2 * MIX_W), lambda b, i: (b, i, 0)),
                   pl.BlockSpec((1, HALO, MIX_W), lambda b, i: (b, 0, 0))],
        out_shape=[jax.ShapeDtypeStruct((B, T, 2 * MIX_W), BF16),
                   jax.ShapeDtypeStruct((B, HALO, MIX_W), F32)],
        scratch_shapes=[pltpu.VMEM((HALO, MIX_W), F32)],
        compiler_params=_cparams("arbitrary", "arbitrary"),
        name="mix0",
    )(x, _rows(g_attn), w_in, hist16, w_pool, _rows(pool_scale), mem_k, mem_v)


def _lane_window(lane, lo):
    return (lane >= lo) & (lane < lo + N_SPLIT)


def _head_copies(buf, slot, hbm_refs, row0, tm, sem):
    out = []
    for t, hbm in enumerate(hbm_refs):
        for h in range(N_HEADS):
            src = buf.at[slot, :, pl.ds(t * MIX_W + h * HEAD_W, HEAD_W)]
            out.append(pltpu.make_async_copy(src, hbm.at[pl.ds(row0, tm), h, :], sem.at[slot]))
    return out


def _project_layer1(x, ga_ref, wi_ref, gk_ref, wk_ref, z_ref, bf_refs, k_hbm, v_hbm, kv_buf, sem, *, seq_len):
    tm = x.shape[0]
    i = pl.program_id(0)
    n_steps = pl.num_programs(0)
    slot = i % 2
    copies = functools.partial(_head_copies, kv_buf, hbm_refs=(k_hbm, v_hbm), tm=tm, sem=sem)

    xn = x * _inv_rms(x)
    z = _dot((xn * ga_ref[...]).astype(BF16), wi_ref[...])
    z_ref[:, :MIX_W] = (z[:, :MIX_W] * Q_SCALE).astype(BF16)
    z_ref[:, MIX_W:] = z[:, MIX_W:].astype(BF16)
    kv = _dot((xn * gk_ref[...]).astype(BF16), wk_ref[...])

    @pl.when(i >= 2)
    def _():
        for c in copies(slot, row0=(i - 2) * tm):
            c.wait()

    kv_buf[slot] = kv
    for c in copies(slot, row0=i * tm):
        c.start()

    @pl.when(i == n_steps - 1)
    def _():
        @pl.when(i >= 1)
        def _():
            for c in copies(1 - slot, row0=(i - 1) * tm):
                c.wait()
        for c in copies(slot, row0=i * tm):
            c.wait()

    if bf_refs:
        k = kv[:, :MIX_W]
        v = kv[:, MIX_W:]
        k1a_ref, k2a_ref, vb_ref = bf_refs
        lane = lax.broadcasted_iota(jnp.int32, (tm, HEAD_W), 1)
        pos = (i * tm) % seq_len + lax.broadcasted_iota(jnp.int32, (tm, HEAD_W), 0)
        off = pos % CHUNK
        base = (pos - off).astype(F32)
        off = off.astype(F32)

        def pos_lanes(lo):
            return jnp.where(_lane_window(lane, lo), base, jnp.where(_lane_window(lane, lo + N_SPLIT), off, 0.0))

        pos_hi = pos_lanes(DIFF_D)
        pos_lo = pos_lanes(0)
        for h in range(N_HEADS):
            hs = slice(h * HEAD_W, (h + 1) * HEAD_W)
            k1a_ref[:, hs] = jnp.where(lane < DIFF_D, k[:, hs], pos_hi).astype(BF16)
            k2a_ref[:, hs] = jnp.where(lane >= DIFF_D, k[:, hs], pos_lo).astype(BF16)
        vb_ref[...] = v.astype(BF16)


def _outmlp_kernel(x_ref, cat_ref, wo_ref, g_ref, w1_ref, w2_ref, gf_ref, *rest, final_norm, ff_chunk, n_bf, seq_len):
    if final_norm:
        out_ref, a_ref = rest
    else:
        ga_ref, wi_ref, gk_ref, wk_ref, out_ref, z_ref = rest[:6]
        bf_refs = rest[6:6 + n_bf]
        k_hbm, v_hbm, a_ref, kv_buf, sem = rest[6 + n_bf:]
    x1 = x_ref[...] + _dot(cat_ref[...], wo_ref[...])
    h = (x1 * _inv_rms(x1) * g_ref[...]).astype(BF16)
    for c in range(w1_ref.shape[1] // ff_chunk):
        cs = slice(c * ff_chunk, (c + 1) * ff_chunk)
        a = jnp.maximum(_dot(h, w1_ref[:, cs]), 0.0)
        a_ref[:, cs] = (a * a).astype(BF16)
    x2 = x1 + _dot(a_ref[...], w2_ref[...])
    if final_norm:
        out_ref[...] = x2 * _inv_rms(x2) * gf_ref[...]
    else:
        out_ref[...] = x2
        _project_layer1(x2, ga_ref, wi_ref, gk_ref, wk_ref, z_ref, bf_refs, k_hbm, v_hbm, kv_buf, sem,
                        seq_len=seq_len)


def _outmlp(x, cat, w, layer, tm, final_norm, seq_len=None, attn_operands=False):
    R, D = x.shape
    FF = w["w_ff1"].shape[-1]
    row = lambda i: (i, 0)
    fix = lambda i: (0, 0)
    n_bf = 3 if attn_operands else 0
    in_specs = [pl.BlockSpec((tm, D), row), pl.BlockSpec((tm, D), row),
                _layer(w["w_out"], layer), _layer(_rows(w["g_ffn"]), layer),
                _layer(w["w_ff1"], layer), _layer(w["w_ff2"], layer), _resident((1, D), fix)]
    args = [x, cat, w["w_out"], _rows(w["g_ffn"]), w["w_ff1"], w["w_ff2"], w["g_final"].reshape(1, D)]
    out_specs = [pl.BlockSpec((tm, D), row)]
    out_shape = [jax.ShapeDtypeStruct((R, D), F32)]
    scratch = [pltpu.VMEM((tm, FF), BF16)]
    if not final_norm:
        in_specs += [_layer(_rows(w["g_attn"]), layer + 1), _layer(w["w_in"], layer + 1),
                     _resident((1, D), fix), _resident((D, 2 * MIX_W), fix)]
        args += [_rows(w["g_attn"]), w["w_in"], w["g_kv"].reshape(1, D), w["w_kv"]]
        out_specs += ([pl.BlockSpec((tm, 2 * MIX_W), row)] + [pl.BlockSpec((tm, MIX_W), row)] * n_bf
                      + [pl.BlockSpec(memory_space=pl.ANY)] * 2)
        out_shape += ([jax.ShapeDtypeStruct((R, 2 * MIX_W), BF16)] + [jax.ShapeDtypeStruct((R, MIX_W), BF16)] * n_bf
                      + [jax.ShapeDtypeStruct((R, N_HEADS, HEAD_W), F32)] * 2)
        scratch += [pltpu.VMEM((2, tm, 2 * MIX_W), F32), pltpu.SemaphoreType.DMA((2,))]
    return pl.pallas_call(
        functools.partial(_outmlp_kernel, final_norm=final_norm, ff_chunk=1024, n_bf=n_bf, seq_len=seq_len),
        grid=(R // tm,),
        in_specs=in_specs,
        out_specs=out_specs,
        out_shape=out_shape,
        scratch_shapes=scratch,
        compiler_params=_cparams("arbitrary"),
        name="outmlp",
    )(*args)


def _lambda(lq, lam_init):
    a = jnp.sum(lq[0:1] * lq[1:2], axis=-1, keepdims=True)
    b = jnp.sum(lq[2:3] * lq[3:4], axis=-1, keepdims=True)
    return jnp.exp(a) - jnp.exp(b) + lam_init


def _slope2(h):
    return 2.0 ** (-8.0 * (h + 1) / N_HEADS) * LOG2E


def _bf16_pieces(c):
    out = []
    for _ in range(N_SPLIT):
        p = float(np.asarray(c, dtype=ml_dtypes.bfloat16))
        out.append(p)
        c -= p
    return out


def _slope_lanes(h, lo, shape):
    lane = lax.broadcasted_iota(jnp.int32, shape, 1)
    out = jnp.zeros(shape, F32)
    for p, piece in enumerate(_bf16_pieces(_slope2(h))):
        out = jnp.where((lane == lo + p) | (lane == lo + N_SPLIT + p), piece, out)
    return out.astype(BF16)


def _finish_head(o1, o2, lam, gs, lam_init):
    o = o1 - lam * o2
    return o * _inv_rms(o) * gs * (1.0 - lam_init)


def _diff_kernel(lq_ref, gs_ref, q_ref, qm_ref, k1_ref, k2_ref, v_ref, mk_ref, mv_ref, out_ref,
                 qa_ref, acc_ref, m_ref, s_ref, *, tq, lam_init):
    i = pl.program_id(1)
    nblk = tq // HEAD_W
    n_chain = 2 * N_HEADS
    k_refs = (k1_ref, k2_ref)
    row = lax.broadcasted_iota(jnp.int32, (tq, tq), 0)
    col = lax.broadcasted_iota(jnp.int32, (tq, tq), 1)
    ahead = jnp.maximum(col - row, 0).astype(F32)
    vis = (col // CHUNK) <= (row // CHUNK)
    ones = jnp.ones((tq, HEAD_W), BF16)
    lane = lax.broadcasted_iota(jnp.int32, (tq, HEAD_W), 1)
    q0 = pl.multiple_of(i * tq, tq)

    def hslice(n):
        return slice((n // 2) * HEAD_W, (n // 2 + 1) * HEAD_W)

    def scores(n, k0):
        return _dot_nt(qa_ref[n], k_refs[n % 2][0, pl.ds(k0, tq), hslice(n)])

    def v_aug(n, k0):
        return jnp.concatenate([v_ref[0, pl.ds(k0, tq), hslice(n)], ones], axis=1)

    def consume(n, k0, first):
        vt = v_aug(n, k0)
        blocks = [s_ref[n, :, b * HEAD_W:(b + 1) * HEAD_W] for b in range(nblk)]
        m_row = jnp.max(functools.reduce(jnp.maximum, blocks), axis=-1, keepdims=True)
        if first:
            m_new = jnp.broadcast_to(m_row, (tq, HEAD_W))
        else:
            m_old = m_ref[n]
            m_new = jnp.maximum(m_old, m_row)
        p = jnp.concatenate([jnp.exp2(b - m_new) for b in blocks], axis=1).astype(BF16)
        pv = _dot(p, vt)
        if first:
            acc_ref[n] = pv
        else:
            alpha = jnp.exp2(m_old - m_new)
            acc_ref[n] = jnp.concatenate([alpha, alpha], axis=1) * acc_ref[n] + pv
        m_ref[n] = m_new

    _mem_attend(qm_ref[0], mk_ref, mv_ref, out_ref)
    for h in range(N_HEADS):
        qh = q_ref[0, :, h * HEAD_W:(h + 1) * HEAD_W]
        qa_ref[2 * h] = jnp.where(lane < DIFF_D, qh, _slope_lanes(h, DIFF_D, qh.shape))
        qa_ref[2 * h + 1] = jnp.where(lane >= DIFF_D, qh, _slope_lanes(h, 0, qh.shape))
        fix = jnp.where(vis, (-2.0 * _slope2(h)) * ahead, NEG_BIG)
        for n in (2 * h, 2 * h + 1):
            s_ref[n] = scores(n, q0) + fix

    @pl.when(i == 0)
    def _():
        for n in range(n_chain):
            consume(n, q0, True)

    @pl.when(i > 0)
    def _():
        for n in range(n_chain):
            consume(n, q0, True)
            s_ref[n] = scores(n, 0)

        def body(j, carry):
            k_cur = pl.multiple_of((j - 1) * tq, tq)
            k_next = pl.multiple_of(j * tq, tq)
            for n in range(n_chain):
                consume(n, k_cur, False)
                s_ref[n] = scores(n, k_next)
            return carry

        lax.fori_loop(1, i, body, 0)
        k_last = pl.multiple_of((i - 1) * tq, tq)
        for n in range(n_chain):
            consume(n, k_last, False)

    lam = _lambda(lq_ref[...], lam_init)
    for h in range(N_HEADS):
        o = [acc_ref[n, :, :HEAD_W] / acc_ref[n, :, HEAD_W:] for n in (2 * h, 2 * h + 1)]
        out_ref[0, :, h * HEAD_W:(h + 1) * HEAD_W] = _finish_head(
            o[0], o[1], lam, gs_ref[...], lam_init).astype(out_ref.dtype)


def _diff_prompt(z, k1a, k2a, vb, mem_k, mem_v, lq, g_sub, layer, lam_init, tq):
    B, T, _ = z.shape
    n_chain = 2 * N_HEADS
    diff_layer = layer - (mem_k.shape[0] - lq.shape[0])
    return pl.pallas_call(
        functools.partial(_diff_kernel, tq=tq, lam_init=lam_init),
        grid=(B, T // tq),
        in_specs=[_layer(lq, diff_layer), _layer(_rows(g_sub), diff_layer),
                  pl.BlockSpec((1, tq, MIX_W), lambda b, i: (b, i, 0)),
                  pl.BlockSpec((1, tq, MIX_W), lambda b, i: (b, i, 1)),
                  _resident((1, T, MIX_W), lambda b, i: (b, 0, 0)),
                  _resident((1, T, MIX_W), lambda b, i: (b, 0, 0)),
                  _resident((1, T, MIX_W), lambda b, i: (b, 0, 0)),
                  _batch_spec(mem_k, layer), _batch_spec(mem_v, layer)],
        out_specs=pl.BlockSpec((1, tq, 2 * MIX_W), lambda b, i: (b, i, 0)),
        out_shape=jax.ShapeDtypeStruct((B, T, 2 * MIX_W), BF16),
        scratch_shapes=[pltpu.VMEM((n_chain, tq, HEAD_W), BF16),
                        pltpu.VMEM((n_chain, tq, 2 * HEAD_W), F32),
                        pltpu.VMEM((n_chain, tq, HEAD_W), F32),
                        pltpu.VMEM((n_chain, tq, tq), F32)],
        compiler_params=_cparams("arbitrary", "arbitrary"),
        name="diff_prompt",
    )(lq, _rows(g_sub), z, z, k1a, k2a, vb, mem_k, mem_v)


def _cache_copies(hbm_refs, b, buf, slot, sem):
    return [pltpu.make_async_copy(hbm.at[b, :, h, :], buf.at[slot, t, h], sem.at[slot])
            for t, hbm in enumerate(hbm_refs) for h in range(N_HEADS)]


def _diff_sample_kernel(lq_ref, gs_ref, q_ref, qm_ref, ck_hbm, cv_hbm, nk_ref, nv_ref, mk_ref, mv_ref, out_ref,
                        cache_buf, sem, *, p0, lam_init):
    b = pl.program_id(0)
    slot = b % 2
    copies = functools.partial(_cache_copies, (ck_hbm, cv_hbm), buf=cache_buf, sem=sem)

    @pl.when(b == 0)
    def _():
        for c in copies(b, slot=slot):
            c.start()

    @pl.when(b + 1 < pl.num_programs(0))
    def _():
        for c in copies(b + 1, slot=1 - slot):
            c.start()

    for c in copies(b, slot=slot):
        c.wait()

    ts = q_ref.shape[1]
    lam = _lambda(lq_ref[...], lam_init)
    qpos = p0 + lax.broadcasted_iota(jnp.int32, (2 * ts, p0), 0) % ts
    dist_p = (qpos - lax.broadcasted_iota(jnp.int32, (2 * ts, p0), 1)).astype(F32)
    row = p0 + lax.broadcasted_iota(jnp.int32, (2 * ts, ts), 0) % ts
    col = p0 + lax.broadcasted_iota(jnp.int32, (2 * ts, ts), 1)
    dist_n = jnp.abs(row - col).astype(F32)
    vis_n = (col // CHUNK) <= (row // CHUNK)
    lane = lax.broadcasted_iota(jnp.int32, (ts, HEAD_W), 1)
    for h in range(N_HEADS):
        hs = slice(h * HEAD_W, (h + 1) * HEAD_W)
        qh = q_ref[0, :, hs]
        zero = jnp.zeros_like(qh)
        q2 = jnp.concatenate([jnp.where(lane < DIFF_D, qh, zero), jnp.where(lane >= DIFF_D, qh, zero)], axis=0)
        kp = cache_buf[slot, 0, h].astype(BF16)
        vp = cache_buf[slot, 1, h].astype(BF16)
        kn = _head(nk_ref, h).astype(BF16)
        vn = _head(nv_ref, h).astype(BF16)
        sp = _dot_nt(q2, kp) - _slope2(h) * dist_p
        sn = _dot_nt(q2, kn) + jnp.where(vis_n, -_slope2(h) * dist_n, NEG_BIG)
        m = jnp.maximum(jnp.max(sp, axis=-1, keepdims=True), jnp.max(sn, axis=-1, keepdims=True))
        ep = jnp.exp2(sp - m)
        en = jnp.exp2(sn - m)
        l = jnp.sum(ep, axis=-1, keepdims=True) + jnp.sum(en, axis=-1, keepdims=True)
        o = (_dot(ep.astype(BF16), vp) + _dot(en.astype(BF16), vn)) / l
        out_ref[0, :, hs] = _finish_head(o[:ts], o[ts:], lam, gs_ref[...], lam_init).astype(out_ref.dtype)
    _mem_attend(qm_ref[0], mk_ref, mv_ref, out_ref)


def _diff_sample(z, cache_k, cache_v, k_new, v_new, mem_k, mem_v, lq, g_sub, layer, lam_init):
    B, Ts, _ = z.shape
    P0 = cache_k.shape[1]
    diff_layer = layer - (mem_k.shape[0] - lq.shape[0])
    return pl.pallas_call(
        functools.partial(_diff_sample_kernel, p0=P0, lam_init=lam_init),
        grid=(B,),
        in_specs=[_layer(lq, diff_layer), _layer(_rows(g_sub), diff_layer),
                  pl.BlockSpec((1, Ts, MIX_W), lambda b: (b, 0, 0)),
                  pl.BlockSpec((1, Ts, MIX_W), lambda b: (b, 0, 1)),
                  pl.BlockSpec(memory_space=pl.ANY), pl.BlockSpec(memory_space=pl.ANY),
                  _batch_spec(k_new), _batch_spec(v_new), _batch_spec(mem_k, layer), _batch_spec(mem_v, layer)],
        out_specs=pl.BlockSpec((1, Ts, 2 * MIX_W), lambda b: (b, 0, 0)),
        out_shape=jax.ShapeDtypeStruct((B, Ts, 2 * MIX_W), BF16),
        scratch_shapes=[pltpu.VMEM((2, 2, N_HEADS, P0, HEAD_W), F32), pltpu.SemaphoreType.DMA((2,))],
        compiler_params=_cparams("arbitrary"),
        name="diff_sample",
    )(lq, _rows(g_sub), z, z, cache_k, cache_v, k_new, v_new, mem_k, mem_v)


def _lambda_init(layer_idx):
    return 0.8 - 0.6 * math.exp(-0.3 * layer_idx)


def _trunk(x, pos0, pool_hist, mem_k, mem_v, past_k, past_v, w, tm_rows, tm_seq, tq):
    B, T, D = x.shape
    R = B * T
    xr = x.reshape(R, D)
    hist16 = jnp.pad(pool_hist, ((0, 0), (HALO - POOL_HIST, 0), (0, 0)))

    cat0, tail = _mix0(x, w["g_attn"], w["w_in"], hist16, mem_k, mem_v, w["w_pool"], w["pool_scale"], 0,
                       tm_seq, pos0)
    new_pool = tail[:, HALO - POOL_HIST:]
    prompt = past_k is None
    x1, z, *bf, k, v = _outmlp(xr, cat0.reshape(R, D), w, 0, tm_rows, final_norm=False, seq_len=T,
                               attn_operands=prompt)
    z3 = z.reshape(B, T, D)
    lam_init = _lambda_init(1)
    if prompt:
        k1a, k2a, vb = (a.reshape(B, T, MIX_W) for a in bf)
        cat1 = _diff_prompt(z3, k1a, k2a, vb, mem_k, mem_v, w["lambda_qk"], w["g_subln"], 1, lam_init, tq)
    else:
        shp = (B, T, N_HEADS, HEAD_W)
        cat1 = _diff_sample(z3, past_k, past_v, k.reshape(shp), v.reshape(shp), mem_k, mem_v,
                            w["lambda_qk"], w["g_subln"], 1, lam_init)
    y, = _outmlp(x1, cat1.reshape(R, D), w, 1, tm_rows, final_norm=True)
    shp = (B, T, N_HEADS, HEAD_W)
    return y.reshape(B, T, D), new_pool[None], k.reshape(shp), v.reshape(shp)


def kernel(x_prompt, x_sample, mem_prompt, cache_k, cache_v, cache_mem_k, cache_mem_v, state_pool, g_attn, w_in, w_out, g_mem, w_mem_kv, g_ffn, w_ff1, w_ff2, w_pool, pool_scale, lambda_qk, g_subln, g_kv, w_kv, g_final):
    B, T, D = x_prompt.shape
    Bs, Ts, _ = x_sample.shape
    P0 = cache_k.shape[1]
    w = dict(g_attn=g_attn, w_in=w_in.astype(BF16), w_out=w_out.astype(BF16), g_ffn=g_ffn,
             w_ff1=w_ff1.astype(BF16), w_ff2=w_ff2.astype(BF16), w_pool=w_pool.astype(BF16),
             pool_scale=pool_scale, lambda_qk=lambda_qk, g_subln=g_subln, g_kv=g_kv,
             w_kv=w_kv.astype(BF16), g_final=g_final)

    mem_k_p, mem_v_p, mem_k_d, mem_v_d = _mem_kv(mem_prompt, g_mem, w_mem_kv.astype(BF16))
    hist0 = jnp.zeros((B, POOL_HIST, MIX_W), F32)
    y_p, pool_p, k_p, v_p = _trunk(x_prompt, 0, hist0, mem_k_d, mem_v_d, None, None, w,
                                   tm_rows=512, tm_seq=512, tq=512)
    y_s, pool_s, k_s, v_s = _trunk(x_sample, P0, state_pool[0], cache_mem_k, cache_mem_v, cache_k, cache_v, w,
                                   tm_rows=Bs * Ts, tm_seq=Ts, tq=None)
    return (y_p, y_s, mem_k_p, mem_v_p, pool_p, k_p, v_p, pool_s, k_s, v_s)
```

```python
import functools
import math

import jax
import jax.numpy as jnp
import ml_dtypes
import numpy as np
from jax import lax
from jax.experimental import pallas as pl
from jax.experimental.pallas import tpu as pltpu

F32 = jnp.float32
BF16 = jnp.bfloat16

CHUNK = 64
POOL_WINDOWS = (2, 4, 8, 16)
POOL_HIST = 15
HALO = 16
N_HEADS = 4
HEAD_W = 128
DIFF_D = 64
MIX_W = N_HEADS * HEAD_W
EPS = 1e-6
NEG_BIG = -1e30
LOG2E = math.log2(math.e)
Q_SCALE = DIFF_D ** -0.5 * LOG2E
QM_SCALE = HEAD_W ** -0.5 * LOG2E
N_SPLIT = 4
MIX0_SUB_ROWS = 512
VMEM_LIMIT = 56 * 1024 * 1024


def _cparams(*sem):
    return pltpu.CompilerParams(dimension_semantics=sem, vmem_limit_bytes=VMEM_LIMIT)


def _resident(shape, index_map):
    return pl.BlockSpec(shape, index_map, pipeline_mode=pl.Buffered(1))


def _layer(arr, layer):
    zeros = (0,) * (arr.ndim - 1)
    return pl.BlockSpec((None,) + arr.shape[1:], lambda *_: (layer,) + zeros, pipeline_mode=pl.Buffered(1))


def _rows(g):
    return g.reshape(g.shape[0], 1, g.shape[1])


def _inv_rms(x):
    return lax.rsqrt(jnp.mean(x * x, axis=-1, keepdims=True) + EPS)


def _dot(a, b):
    return jnp.dot(a, b, preferred_element_type=F32)


def _dot_nt(a, b):
    return lax.dot_general(a, b, (((1,), (1,)), ((), ())), preferred_element_type=F32)


def _memkv_kernel(x_ref, g_ref, w_ref, k_ref, v_ref, kd_ref, vd_ref):
    x = x_ref[0]
    h = (x * _inv_rms(x) * g_ref[0]).astype(BF16)
    kv = _dot(h, w_ref[0])
    kd_ref[0, 0] = kv[:, :MIX_W]
    vd_ref[0, 0] = kv[:, MIX_W:]
    for hd in range(N_HEADS):
        k_ref[0, 0, :, hd, :] = kv[:, hd * HEAD_W:(hd + 1) * HEAD_W]
        v_ref[0, 0, :, hd, :] = kv[:, MIX_W + hd * HEAD_W:MIX_W + (hd + 1) * HEAD_W]


def _mem_kv(mem, g_mem, w_mem_kv):
    B, M, D = mem.shape
    L = g_mem.shape[0]
    out = jax.ShapeDtypeStruct((L, B, M, N_HEADS, HEAD_W), F32)
    dense = jax.ShapeDtypeStruct((L, B, M, MIX_W), F32)
    return pl.pallas_call(
        _memkv_kernel,
        grid=(L, B),
        in_specs=[pl.BlockSpec((1, M, D), lambda l, b: (b, 0, 0)),
                  pl.BlockSpec((1, 1, D), lambda l, b: (l, 0, 0)),
                  pl.BlockSpec((1, D, 2 * MIX_W), lambda l, b: (l, 0, 0))],
        out_specs=[pl.BlockSpec((1, 1, M, N_HEADS, HEAD_W), lambda l, b: (l, b, 0, 0, 0))] * 2
                  + [pl.BlockSpec((1, 1, M, MIX_W), lambda l, b: (l, b, 0, 0))] * 2,
        out_shape=[out, out, dense, dense],
        compiler_params=_cparams("arbitrary", "arbitrary"),
        name="mem_kv",
    )(mem, g_mem.reshape(L, 1, D), w_mem_kv)


def _batch_spec(arr, layer=None):
    if layer is None:
        zeros = (0,) * (arr.ndim - 1)
        return pl.BlockSpec((1,) + arr.shape[1:], lambda b, *_: (b,) + zeros)
    zeros = (0,) * (arr.ndim - 2)
    return pl.BlockSpec((None, 1) + arr.shape[2:], lambda b, *_: (layer, b) + zeros)


def _head(ref, h):
    if len(ref.shape) == 3:
        return ref[0, :, h * HEAD_W:(h + 1) * HEAD_W]
    return ref[0, :, h, :]


def _mem_attend(qm, mk_ref, mv_ref, out_ref, rs=slice(None)):
    ones = jnp.ones((mk_ref.shape[1], HEAD_W), BF16)
    for h in range(N_HEADS):
        hs = slice(h * HEAD_W, (h + 1) * HEAD_W)
        kh = _head(mk_ref, h).astype(BF16)
        vh = jnp.concatenate([_head(mv_ref, h).astype(BF16), ones], axis=1)
        s = _dot_nt(qm[:, hs], kh)
        e = jnp.exp2(s - jnp.max(s, axis=-1, keepdims=True))
        pv = _dot(e.astype(BF16), vh)
        o = pv[:, :HEAD_W] / pv[:, HEAD_W:]
        out_ref[0, rs, MIX_W + h * HEAD_W:MIX_W + (h + 1) * HEAD_W] = o.astype(out_ref.dtype)


def _mix0_kernel(x_ref, g_ref, wi_ref, hist_ref, wp_ref, ps_ref, mk_ref, mv_ref, out_ref, tail_ref, carry_ref,
                 *, tm, sub, pos0):
    i = pl.program_id(1)

    @pl.when(i == 0)
    def _():
        carry_ref[...] = hist_ref[0]

    halo = carry_ref[...]
    for r in range(tm // sub):
        rs = slice(r * sub, (r + 1) * sub)
        x = x_ref[0, rs, :]
        z = _dot((x * _inv_rms(x) * g_ref[...]).astype(BF16), wi_ref[...])
        u = z[:, :MIX_W]
        qm = (z[:, MIX_W:] * QM_SCALE).astype(BF16)
        ext = jnp.concatenate([halo, u], axis=0)
        halo = u[sub - HALO:]
        pos = pos0 + i * tm + r * sub + lax.broadcasted_iota(jnp.int32, (sub, HEAD_W), 0)
        for g, w in enumerate(POOL_WINDOWS):
            gs = slice(g * HEAD_W, (g + 1) * HEAD_W)
            acc = ext[:, gs]
            span = 1
            while span < w:
                acc = acc + pltpu.roll(acc, span, 0)
                span *= 2
            cnt = jnp.minimum(pos + 1, w).astype(F32)
            d = acc[HALO:] / cnt - u[:, gs]
            t = _dot(d.astype(BF16), wp_ref[g]) * ps_ref[:, gs]
            out_ref[0, rs, gs] = t.astype(out_ref.dtype)
        _mem_attend(qm, mk_ref, mv_ref, out_ref, rs)
    carry_ref[...] = halo
    tail_ref[0] = halo


def _mix0(x, g_attn, w_in, hist16, mem_k, mem_v, w_pool, pool_scale, layer, tm, pos0):
    B, T, D = x.shape
    sub = min(tm, MIX0_SUB_ROWS)
    assert T % tm == 0 and tm % sub == 0 and sub >= HALO
    return pl.pallas_call(
        functools.partial(_mix0_kernel, tm=tm, sub=sub, pos0=pos0),
        grid=(B, T // tm),
        in_specs=[pl.BlockSpec((1, tm, D), lambda b, i: (b, i, 0)),
                  _layer(_rows(g_attn), layer), _layer(w_in, layer),
                  pl.BlockSpec((1, HALO, MIX_W), lambda b, i: (b, 0, 0)),
                  _layer(w_pool, layer), _layer(_rows(pool_scale), layer),
                  _batch_spec(mem_k, layer), _batch_spec(mem_v, layer)],
        out_specs=[pl.BlockSpec((1, tm, 2 * MIX_W), lambda b, i: (b, i, 0)),
                   pl.BlockSpec((1, HALO, MIX_W), lambda b, i: (b, 0, 0))],
        out_shape=[jax.ShapeDtypeStruct((B, T, 2 * MIX_W), BF16),
                   jax.ShapeDtypeStruct((B, HALO, MIX_W), F32)],
        scratch_shapes=[pltpu.VMEM((HALO, MIX_W), F32)],
        compiler_params=_cparams("arbitrary", "arbitrary"),
        name="mix0",
    )(x, _rows(g_attn), w_in, hist16, w_pool, _rows(pool_scale), mem_k, mem_v)


def _lane_window(lane, lo):
    return (lane >= lo) & (lane < lo + N_SPLIT)


def _head_copies(buf, slot, hbm_refs, row0, tm, sem):
    out = []
    for t, hbm in enumerate(hbm_refs):
        for h in range(N_HEADS):
            src = buf.at[slot, :, pl.ds(t * MIX_W + h * HEAD_W, HEAD_W)]
            out.append(pltpu.make_async_copy(src, hbm.at[pl.ds(row0, tm), h, :], sem.at[slot]))
    return out


def _project_layer1(x, ga_ref, wi_ref, gk_ref, wk_ref, z_ref, bf_refs, k_hbm, v_hbm, kv_buf, sem, *, seq_len):
    tm = x.shape[0]
    i = pl.program_id(0)
    n_steps = pl.num_programs(0)
    slot = i % 2
    copies = functools.partial(_head_copies, kv_buf, hbm_refs=(k_hbm, v_hbm), tm=tm, sem=sem)

    xn = x * _inv_rms(x)
    z = _dot((xn * ga_ref[...]).astype(BF16), wi_ref[...])
    z_ref[:, :MIX_W] = (z[:, :MIX_W] * Q_SCALE).astype(BF16)
    z_ref[:, MIX_W:] = (z[:, MIX_W:] * QM_SCALE).astype(BF16)
    kv = _dot((xn * gk_ref[...]).astype(BF16), wk_ref[...])

    @pl.when(i >= 2)
    def _():
        for c in copies(slot, row0=(i - 2) * tm):
            c.wait()

    kv_buf[slot] = kv
    for c in copies(slot, row0=i * tm):
        c.start()

    @pl.when(i == n_steps - 1)
    def _():
        @pl.when(i >= 1)
        def _():
            for c in copies(1 - slot, row0=(i - 1) * tm):
                c.wait()
        for c in copies(slot, row0=i * tm):
            c.wait()

    if bf_refs:
        k = kv[:, :MIX_W]
        v = kv[:, MIX_W:]
        k1a_ref, k2a_ref, vb_ref = bf_refs
        lane = lax.broadcasted_iota(jnp.int32, (tm, HEAD_W), 1)
        pos = (i * tm) % seq_len + lax.broadcasted_iota(jnp.int32, (tm, HEAD_W), 0)
        off = pos % CHUNK
        base = (pos - off).astype(F32)
        off = off.astype(F32)

        def pos_lanes(lo):
            return jnp.where(_lane_window(lane, lo), base, jnp.where(_lane_window(lane, lo + N_SPLIT), off, 0.0))

        pos_hi = pos_lanes(DIFF_D)
        pos_lo = pos_lanes(0)
        for h in range(N_HEADS):
            hs = slice(h * HEAD_W, (h + 1) * HEAD_W)
            k1a_ref[:, hs] = jnp.where(lane < DIFF_D, k[:, hs], pos_hi).astype(BF16)
            k2a_ref[:, hs] = jnp.where(lane >= DIFF_D, k[:, hs], pos_lo).astype(BF16)
        vb_ref[...] = v.astype(BF16)


def _outmlp_kernel(x_ref, cat_ref, wo_ref, g_ref, w1_ref, w2_ref, gf_ref, *rest, final_norm, ff_chunk, n_bf, seq_len):
    if final_norm:
        out_ref, a_ref = rest
    else:
        ga_ref, wi_ref, gk_ref, wk_ref, out_ref, z_ref = rest[:6]
        bf_refs = rest[6:6 + n_bf]
        k_hbm, v_hbm, a_ref, kv_buf, sem = rest[6 + n_bf:]
    x1 = x_ref[...] + _dot(cat_ref[...], wo_ref[...])
    h = (x1 * _inv_rms(x1) * g_ref[...]).astype(BF16)
    for c in range(w1_ref.shape[1] // ff_chunk):
        cs = slice(c * ff_chunk, (c + 1) * ff_chunk)
        a = jnp.maximum(_dot(h, w1_ref[:, cs]), 0.0)
        a_ref[:, cs] = (a * a).astype(BF16)
    x2 = x1 + _dot(a_ref[...], w2_ref[...])
    if final_norm:
        out_ref[...] = x2 * _inv_rms(x2) * gf_ref[...]
    else:
        out_ref[...] = x2
        _project_layer1(x2, ga_ref, wi_ref, gk_ref, wk_ref, z_ref, bf_refs, k_hbm, v_hbm, kv_buf, sem,
                        seq_len=seq_len)


def _outmlp(x, cat, w, layer, tm, final_norm, seq_len=None, attn_operands=False):
    R, D = x.shape
    FF = w["w_ff1"].shape[-1]
    row = lambda i: (i, 0)
    fix = lambda i: (0, 0)
    n_bf = 3 if attn_operands else 0
    in_specs = [pl.BlockSpec((tm, D), row), pl.BlockSpec((tm, D), row),
                _layer(w["w_out"], layer), _layer(_rows(w["g_ffn"]), layer),
                _layer(w["w_ff1"], layer), _layer(w["w_ff2"], layer), _resident((1, D), fix)]
    args = [x, cat, w["w_out"], _rows(w["g_ffn"]), w["w_ff1"], w["w_ff2"], w["g_final"].reshape(1, D)]
    out_specs = [pl.BlockSpec((tm, D), row)]
    out_shape = [jax.ShapeDtypeStruct((R, D), F32)]
    scratch = [pltpu.VMEM((tm, FF), BF16)]
    if not final_norm:
        in_specs += [_layer(_rows(w["g_attn"]), layer + 1), _layer(w["w_in"], layer + 1),
                     _resident((1, D), fix), _resident((D, 2 * MIX_W), fix)]
        args += [_rows(w["g_attn"]), w["w_in"], w["g_kv"].reshape(1, D), w["w_kv"]]
        out_specs += ([pl.BlockSpec((tm, 2 * MIX_W), row)] + [pl.BlockSpec((tm, MIX_W), row)] * n_bf
                      + [pl.BlockSpec(memory_space=pl.ANY)] * 2)
        out_shape += ([jax.ShapeDtypeStruct((R, 2 * MIX_W), BF16)] + [jax.ShapeDtypeStruct((R, MIX_W), BF16)] * n_bf
                      + [jax.ShapeDtypeStruct((R, N_HEADS, HEAD_W), F32)] * 2)
        scratch += [pltpu.VMEM((2, tm, 2 * MIX_W), F32), pltpu.SemaphoreType.DMA((2,))]
    return pl.pallas_call(
        functools.partial(_outmlp_kernel, final_norm=final_norm, ff_chunk=1024, n_bf=n_bf, seq_len=seq_len),
        grid=(R // tm,),
        in_specs=in_specs,
        out_specs=out_specs,
        out_shape=out_shape,
        scratch_shapes=scratch,
        compiler_params=_cparams("arbitrary"),
        name="outmlp",
    )(*args)


def _lambda(lq, lam_init):
    a = jnp.sum(lq[0:1] * lq[1:2], axis=-1, keepdims=True)
    b = jnp.sum(lq[2:3] * lq[3:4], axis=-1, keepdims=True)
    return jnp.exp(a) - jnp.exp(b) + lam_init


def _slope2(h):
    return 2.0 ** (-8.0 * (h + 1) / N_HEADS) * LOG2E


def _bf16_pieces(c):
    out = []
    for _ in range(N_SPLIT):
        p = float(np.asarray(c, dtype=ml_dtypes.bfloat16))
        out.append(p)
        c -= p
    return out


def _slope_lanes(h, lo, shape):
    lane = lax.broadcasted_iota(jnp.int32, shape, 1)
    out = jnp.zeros(shape, F32)
    for p, piece in enumerate(_bf16_pieces(_slope2(h))):
        out = jnp.where((lane == lo + p) | (lane == lo + N_SPLIT + p), piece, out)
    return out.astype(BF16)


def _finish_head(o1, o2, lam, gs, lam_init):
    o = o1 - lam * o2
    return o * _inv_rms(o) * gs * (1.0 - lam_init)


def _diff_kernel(lq_ref, gs_ref, q_ref, qm_ref, k1_ref, k2_ref, v_ref, mk_ref, mv_ref, out_ref,
                 qa_ref, acc_ref, m_ref, s_ref, *, tq, lam_init):
    i = pl.program_id(1)
    nblk = tq // HEAD_W
    n_chain = 2 * N_HEADS
    k_refs = (k1_ref, k2_ref)
    row = lax.broadcasted_iota(jnp.int32, (tq, tq), 0)
    col = lax.broadcasted_iota(jnp.int32, (tq, tq), 1)
    ahead = jnp.maximum(col - row, 0).astype(F32)
    vis = (col // CHUNK) <= (row // CHUNK)
    ones = jnp.ones((tq, HEAD_W), BF16)
    lane = lax.broadcasted_iota(jnp.int32, (tq, HEAD_W), 1)
    q0 = pl.multiple_of(i * tq, tq)

    def hslice(n):
        return slice((n // 2) * HEAD_W, (n // 2 + 1) * HEAD_W)

    def scores(n, k0):
        return _dot_nt(qa_ref[n], k_refs[n % 2][0, pl.ds(k0, tq), hslice(n)])

    def v_aug(n, k0):
        return jnp.concatenate([v_ref[0, pl.ds(k0, tq), hslice(n)], ones], axis=1)

    def consume(n, k0, first):
        vt = v_aug(n, k0)
        blocks = [s_ref[n, :, b * HEAD_W:(b + 1) * HEAD_W] for b in range(nblk)]
        m_row = jnp.max(functools.reduce(jnp.maximum, blocks), axis=-1, keepdims=True)
        if first:
            m_new = jnp.broadcast_to(m_row, (tq, HEAD_W))
        else:
            m_old = m_ref[n]
            m_new = jnp.maximum(m_old, m_row)
        p = jnp.concatenate([jnp.exp2(b - m_new) for b in blocks], axis=1).astype(BF16)
        pv = _dot(p, vt)
        if first:
            acc_ref[n] = pv
        else:
            alpha = jnp.exp2(m_old - m_new)
            acc_ref[n] = jnp.concatenate([alpha, alpha], axis=1) * acc_ref[n] + pv
        m_ref[n] = m_new

    _mem_attend(qm_ref[0], mk_ref, mv_ref, out_ref)
    for h in range(N_HEADS):
        qh = q_ref[0, :, h * HEAD_W:(h + 1) * HEAD_W]
        qa_ref[2 * h] = jnp.where(lane < DIFF_D, qh, _slope_lanes(h, DIFF_D, qh.shape))
        qa_ref[2 * h + 1] = jnp.where(lane >= DIFF_D, qh, _slope_lanes(h, 0, qh.shape))
        fix = jnp.where(vis, (-2.0 * _slope2(h)) * ahead, NEG_BIG)
        for n in (2 * h, 2 * h + 1):
            s_ref[n] = scores(n, q0) + fix

    @pl.when(i == 0)
    def _():
        for n in range(n_chain):
            consume(n, q0, True)

    @pl.when(i > 0)
    def _():
        for n in range(n_chain):
            consume(n, q0, True)
            s_ref[n] = scores(n, 0)

        def body(j, carry):
            k_cur = pl.multiple_of((j - 1) * tq, tq)
            k_next = pl.multiple_of(j * tq, tq)
            for n in range(n_chain):
                consume(n, k_cur, False)
                s_ref[n] = scores(n, k_next)
            return carry

        lax.fori_loop(1, i, body, 0)
        k_last = pl.multiple_of((i - 1) * tq, tq)
        for n in range(n_chain):
            consume(n, k_last, False)

    lam = _lambda(lq_ref[...], lam_init)
    for h in range(N_HEADS):
        o = [acc_ref[n, :, :HEAD_W] / acc_ref[n, :, HEAD_W:] for n in (2 * h, 2 * h + 1)]
        out_ref[0, :, h * HEAD_W:(h + 1) * HEAD_W] = _finish_head(
            o[0], o[1], lam, gs_ref[...], lam_init).astype(out_ref.dtype)


def _diff_prompt(z, k1a, k2a, vb, mem_k, mem_v, lq, g_sub, layer, lam_init, tq):
    B, T, _ = z.shape
    n_chain = 2 * N_HEADS
    diff_layer = layer - (mem_k.shape[0] - lq.shape[0])
    return pl.pallas_call(
        functools.partial(_diff_kernel, tq=tq, lam_init=lam_init),
        grid=(B, T // tq),
        in_specs=[_layer(lq, diff_layer), _layer(_rows(g_sub), diff_layer),
                  pl.BlockSpec((1, tq, MIX_W), lambda b, i: (b, i, 0)),
                  pl.BlockSpec((1, tq, MIX_W), lambda b, i: (b, i, 1)),
                  _resident((1, T, MIX_W), lambda b, i: (b, 0, 0)),
                  _resident((1, T, MIX_W), lambda b, i: (b, 0, 0)),
                  _resident((1, T, MIX_W), lambda b, i: (b, 0, 0)),
                  _batch_spec(mem_k, layer), _batch_spec(mem_v, layer)],
        out_specs=pl.BlockSpec((1, tq, 2 * MIX_W), lambda b, i: (b, i, 0)),
        out_shape=jax.ShapeDtypeStruct((B, T, 2 * MIX_W), BF16),
        scratch_shapes=[pltpu.VMEM((n_chain, tq, HEAD_W), BF16),
                        pltpu.VMEM((n_chain, tq, 2 * HEAD_W), F32),
                        pltpu.VMEM((n_chain, tq, HEAD_W), F32),
                        pltpu.VMEM((n_chain, tq, tq), F32)],
        compiler_params=_cparams("arbitrary", "arbitrary"),
        name="diff_prompt",
    )(lq, _rows(g_sub), z, z, k1a, k2a, vb, mem_k, mem_v)


def _cache_copies(hbm_refs, b, buf, slot, sem):
    return [pltpu.make_async_copy(hbm.at[b, :, h, :], buf.at[slot, t, h], sem.at[slot])
            for t, hbm in enumerate(hbm_refs) for h in range(N_HEADS)]


def _diff_sample_kernel(lq_ref, gs_ref, q_ref, qm_ref, ck_hbm, cv_hbm, nk_ref, nv_ref, mk_ref, mv_ref, out_ref,
                        cache_buf, sem, *, p0, lam_init):
    b = pl.program_id(0)
    slot = b % 2
    copies = functools.partial(_cache_copies, (ck_hbm, cv_hbm), buf=cache_buf, sem=sem)

    @pl.when(b == 0)
    def _():
        for c in copies(b, slot=slot):
            c.start()

    @pl.when(b + 1 < pl.num_programs(0))
    def _():
        for c in copies(b + 1, slot=1 - slot):
            c.start()

    for c in copies(b, slot=slot):
        c.wait()

    ts = q_ref.shape[1]
    lam = _lambda(lq_ref[...], lam_init)
    qpos = p0 + lax.broadcasted_iota(jnp.int32, (2 * ts, p0), 0) % ts
    dist_p = (qpos - lax.broadcasted_iota(jnp.int32, (2 * ts, p0), 1)).astype(F32)
    row = p0 + lax.broadcasted_iota(jnp.int32, (2 * ts, ts), 0) % ts
    col = p0 + lax.broadcasted_iota(jnp.int32, (2 * ts, ts), 1)
    dist_n = jnp.abs(row - col).astype(F32)
    vis_n = (col // CHUNK) <= (row // CHUNK)
    lane = lax.broadcasted_iota(jnp.int32, (ts, HEAD_W), 1)
    for h in range(N_HEADS):
        hs = slice(h * HEAD_W, (h + 1) * HEAD_W)
        qh = q_ref[0, :, hs]
        zero = jnp.zeros_like(qh)
        q2 = jnp.concatenate([jnp.where(lane < DIFF_D, qh, zero), jnp.where(lane >= DIFF_D, qh, zero)], axis=0)
        kp = cache_buf[slot, 0, h].astype(BF16)
        vp = cache_buf[slot, 1, h].astype(BF16)
        kn = _head(nk_ref, h).astype(BF16)
        vn = _head(nv_ref, h).astype(BF16)
        sp = _dot_nt(q2, kp) - _slope2(h) * dist_p
        sn = _dot_nt(q2, kn) + jnp.where(vis_n, -_slope2(h) * dist_n, NEG_BIG)
        m = jnp.maximum(jnp.max(sp, axis=-1, keepdims=True), jnp.max(sn, axis=-1, keepdims=True))
        ep = jnp.exp2(sp - m)
        en = jnp.exp2(sn - m)
        l = jnp.sum(ep, axis=-1, keepdims=True) + jnp.sum(en, axis=-1, keepdims=True)
        o = (_dot(ep.astype(BF16), vp) + _dot(en.astype(BF16), vn)) / l
        out_ref[0, :, hs] = _finish_head(o[:ts], o[ts:], lam, gs_ref[...], lam_init).astype(out_ref.dtype)
    _mem_attend(qm_ref[0], mk_ref, mv_ref, out_ref)


def _diff_sample(z, cache_k, cache_v, k_new, v_new, mem_k, mem_v, lq, g_sub, layer, lam_init):
    B, Ts, _ = z.shape
    P0 = cache_k.shape[1]
    diff_layer = layer - (mem_k.shape[0] - lq.shape[0])
    return pl.pallas_call(
        functools.partial(_diff_sample_kernel, p0=P0, lam_init=lam_init),
        grid=(B,),
        in_specs=[_layer(lq, diff_layer), _layer(_rows(g_sub), diff_layer),
                  pl.BlockSpec((1, Ts, MIX_W), lambda b: (b, 0, 0)),
                  pl.BlockSpec((1, Ts, MIX_W), lambda b: (b, 0, 1)),
                  pl.BlockSpec(memory_space=pl.ANY), pl.BlockSpec(memory_space=pl.ANY),
                  _batch_spec(k_new), _batch_spec(v_new), _batch_spec(mem_k, layer), _batch_spec(mem_v, layer)],
        out_specs=pl.BlockSpec((1, Ts, 2 * MIX_W), lambda b: (b, 0, 0)),
        out_shape=jax.ShapeDtypeStruct((B, Ts, 2 * MIX_W), BF16),
        scratch_shapes=[pltpu.VMEM((2, 2, N_HEADS, P0, HEAD_W), F32), pltpu.SemaphoreType.DMA((2,))],
        compiler_params=_cparams("arbitrary"),
        name="diff_sample",
    )(lq, _rows(g_sub), z, z, cache_k, cache_v, k_new, v_new, mem_k, mem_v)


def _lambda_init(layer_idx):
    return 0.8 - 0.6 * math.exp(-0.3 * layer_idx)


def _trunk(x, pos0, pool_hist, mem_k, mem_v, past_k, past_v, w, tm_rows, tm_seq, tq):
    B, T, D = x.shape
    R = B * T
    xr = x.reshape(R, D)
    hist16 = jnp.pad(pool_hist, ((0, 0), (HALO - POOL_HIST, 0), (0, 0)))

    cat0, tail = _mix0(x, w["g_attn"], w["w_in"], hist16, mem_k, mem_v, w["w_pool"], w["pool_scale"], 0,
                       tm_seq, pos0)
    new_pool = tail[:, HALO - POOL_HIST:]
    prompt = past_k is None
    x1, z, *bf, k, v = _outmlp(xr, cat0.reshape(R, D), w, 0, tm_rows, final_norm=False, seq_len=T,
                               attn_operands=prompt)
    z3 = z.reshape(B, T, D)
    lam_init = _lambda_init(1)
    if prompt:
        k1a, k2a, vb = (a.reshape(B, T, MIX_W) for a in bf)
        cat1 = _diff_prompt(z3, k1a, k2a, vb, mem_k, mem_v, w["lambda_qk"], w["g_subln"], 1, lam_init, tq)
    else:
        shp = (B, T, N_HEADS, HEAD_W)
        cat1 = _diff_sample(z3, past_k, past_v, k.reshape(shp), v.reshape(shp), mem_k, mem_v,
                            w["lambda_qk"], w["g_subln"], 1, lam_init)
    y, = _outmlp(x1, cat1.reshape(R, D), w, 1, tm_rows, final_norm=True)
    shp = (B, T, N_HEADS, HEAD_W)
    return y.reshape(B, T, D), new_pool[None], k.reshape(shp), v.reshape(shp)


def kernel(x_prompt, x_sample, mem_prompt, cache_k, cache_v, cache_mem_k, cache_mem_v, state_pool, g_attn, w_in, w_out, g_mem, w_mem_kv, g_ffn, w_ff1, w_ff2, w_pool, pool_scale, lambda_qk, g_subln, g_kv, w_kv, g_final):
    B, T, D = x_prompt.shape
    Bs, Ts, _ = x_sample.shape
    P0 = cache_k.shape[1]
    w = dict(g_attn=g_attn, w_in=w_in.astype(BF16), w_out=w_out.astype(BF16), g_ffn=g_ffn,
             w_ff1=w_ff1.astype(BF16), w_ff2=w_ff2.astype(BF16), w_pool=w_pool.astype(BF16),
             pool_scale=pool_scale, lambda_qk=lambda_qk, g_subln=g_subln, g_kv=g_kv,
             w_kv=w_kv.astype(BF16), g_final=g_final)

    mem_k_p, mem_v_p, mem_k_d, mem_v_d = _mem_kv(mem_prompt, g_mem, w_mem_kv.astype(BF16))
    hist0 = jnp.zeros((B, POOL_HIST, MIX_W), F32)
    y_p, pool_p, k_p, v_p = _trunk(x_prompt, 0, hist0, mem_k_d, mem_v_d, None, None, w,
                                   tm_rows=512, tm_seq=2048, tq=512)
    y_s, pool_s, k_s, v_s = _trunk(x_sample, P0, state_pool[0], cache_mem_k, cache_mem_v, cache_k, cache_v, w,
                                   tm_rows=Bs * Ts, tm_seq=Ts, tq=None)
    return (y_p, y_s, mem_k_p, mem_v_p, pool_p, k_p, v_p, pool_s, k_s, v_s)
```

```python
import functools
import math

import jax
import jax.numpy as jnp
import ml_dtypes
import numpy as np
from jax import lax
from jax.experimental import pallas as pl
from jax.experimental.pallas import tpu as pltpu

F32 = jnp.float32
BF16 = jnp.bfloat16

CHUNK = 64
POOL_WINDOWS = (2, 4, 8, 16)
POOL_HIST = 15
HALO = 16
N_HEADS = 4
HEAD_W = 128
DIFF_D = 64
MIX_W = N_HEADS * HEAD_W
EPS = 1e-6
NEG_BIG = -1e30
LOG2E = math.log2(math.e)
Q_SCALE = DIFF_D ** -0.5 * LOG2E
QM_SCALE = HEAD_W ** -0.5 * LOG2E
N_SPLIT = 4
MIX0_SUB_ROWS = 512
VMEM_LIMIT = 56 * 1024 * 1024


def _cparams(*sem):
    return pltpu.CompilerParams(dimension_semantics=sem, vmem_limit_bytes=VMEM_LIMIT)


def _resident(shape, index_map):
    return pl.BlockSpec(shape, index_map, pipeline_mode=pl.Buffered(1))


def _layer(arr, layer):
    zeros = (0,) * (arr.ndim - 1)
    return pl.BlockSpec((None,) + arr.shape[1:], lambda *_: (layer,) + zeros, pipeline_mode=pl.Buffered(1))


def _rows(g):
    return g.reshape(g.shape[0], 1, g.shape[1])


def _inv_rms(x):
    return lax.rsqrt(jnp.mean(x * x, axis=-1, keepdims=True) + EPS)


def _dot(a, b):
    return jnp.dot(a, b, preferred_element_type=F32)


def _dot_nt(a, b):
    return lax.dot_general(a, b, (((1,), (1,)), ((), ())), preferred_element_type=F32)


def _memkv_kernel(x_ref, g_ref, w_ref, k_ref, v_ref, kd_ref, vd_ref):
    x = x_ref[0]
    h = (x * _inv_rms(x) * g_ref[0]).astype(BF16)
    kv = _dot(h, w_ref[0])
    kd_ref[0, 0] = kv[:, :MIX_W]
    vd_ref[0, 0] = kv[:, MIX_W:]
    for hd in range(N_HEADS):
        k_ref[0, 0, :, hd, :] = kv[:, hd * HEAD_W:(hd + 1) * HEAD_W]
        v_ref[0, 0, :, hd, :] = kv[:, MIX_W + hd * HEAD_W:MIX_W + (hd + 1) * HEAD_W]


def _mem_kv(mem, g_mem, w_mem_kv):
    B, M, D = mem.shape
    L = g_mem.shape[0]
    out = jax.ShapeDtypeStruct((L, B, M, N_HEADS, HEAD_W), F32)
    dense = jax.ShapeDtypeStruct((L, B, M, MIX_W), F32)
    return pl.pallas_call(
        _memkv_kernel,
        grid=(L, B),
        in_specs=[pl.BlockSpec((1, M, D), lambda l, b: (b, 0, 0)),
                  pl.BlockSpec((1, 1, D), lambda l, b: (l, 0, 0)),
                  pl.BlockSpec((1, D, 2 * MIX_W), lambda l, b: (l, 0, 0))],
        out_specs=[pl.BlockSpec((1, 1, M, N_HEADS, HEAD_W), lambda l, b: (l, b, 0, 0, 0))] * 2
                  + [pl.BlockSpec((1, 1, M, MIX_W), lambda l, b: (l, b, 0, 0))] * 2,
        out_shape=[out, out, dense, dense],
        compiler_params=_cparams("arbitrary", "arbitrary"),
        name="mem_kv",
    )(mem, g_mem.reshape(L, 1, D), w_mem_kv)


def _batch_spec(arr, layer=None):
    if layer is None:
        zeros = (0,) * (arr.ndim - 1)
        return pl.BlockSpec((1,) + arr.shape[1:], lambda b, *_: (b,) + zeros)
    zeros = (0,) * (arr.ndim - 2)
    return pl.BlockSpec((None, 1) + arr.shape[2:], lambda b, *_: (layer, b) + zeros)


def _head(ref, h):
    if len(ref.shape) == 3:
        return ref[0, :, h * HEAD_W:(h + 1) * HEAD_W]
    return ref[0, :, h, :]


def _mem_attend(qm, mk_ref, mv_ref, out_ref, rs=slice(None)):
    ones = jnp.ones((mk_ref.shape[1], HEAD_W), BF16)
    for h in range(N_HEADS):
        hs = slice(h * HEAD_W, (h + 1) * HEAD_W)
        kh = _head(mk_ref, h).astype(BF16)
        vh = jnp.concatenate([_head(mv_ref, h).astype(BF16), ones], axis=1)
        s = _dot_nt(qm[:, hs], kh)
        e = jnp.exp2(s - jnp.max(s, axis=-1, keepdims=True))
        pv = _dot(e.astype(BF16), vh)
        o = pv[:, :HEAD_W] / pv[:, HEAD_W:]
        out_ref[0, rs, MIX_W + h * HEAD_W:MIX_W + (h + 1) * HEAD_W] = o.astype(out_ref.dtype)


def _mix0_kernel(x_ref, g_ref, wi_ref, hist_ref, wp_ref, ps_ref, mk_ref, mv_ref, out_ref, tail_ref, carry_ref,
                 *, tm, sub, pos0):
    i = pl.program_id(1)

    @pl.when(i == 0)
    def _():
        carry_ref[...] = hist_ref[0]

    halo = carry_ref[...]
    for r in range(tm // sub):
        rs = slice(r * sub, (r + 1) * sub)
        x = x_ref[0, rs, :]
        z = _dot((x * _inv_rms(x) * g_ref[...]).astype(BF16), wi_ref[...])
        u = z[:, :MIX_W]
        qm = (z[:, MIX_W:] * QM_SCALE).astype(BF16)
        ext = jnp.concatenate([halo, u], axis=0)
        halo = u[sub - HALO:]
        pos = pos0 + i * tm + r * sub + lax.broadcasted_iota(jnp.int32, (sub, HEAD_W), 0)
        for g, w in enumerate(POOL_WINDOWS):
            gs = slice(g * HEAD_W, (g + 1) * HEAD_W)
            acc = ext[:, gs]
            span = 1
            while span < w:
                acc = acc + pltpu.roll(acc, span, 0)
                span *= 2
            cnt = jnp.minimum(pos + 1, w).astype(F32)
            d = acc[HALO:] / cnt - u[:, gs]
            t = _dot(d.astype(BF16), wp_ref[g]) * ps_ref[:, gs]
            out_ref[0, rs, gs] = t.astype(out_ref.dtype)
        _mem_attend(qm, mk_ref, mv_ref, out_ref, rs)
    carry_ref[...] = halo
    tail_ref[0] = halo


def _mix0(x, g_attn, w_in, hist16, mem_k, mem_v, w_pool, pool_scale, layer, tm, pos0):
    B, T, D = x.shape
    sub = min(tm, MIX0_SUB_ROWS)
    assert T % tm == 0 and tm % sub == 0 and sub >= HALO
    return pl.pallas_call(
        functools.partial(_mix0_kernel, tm=tm, sub=sub, pos0=pos0),
        grid=(B, T // tm),
        in_specs=[pl.BlockSpec((1, tm, D), lambda b, i: (b, i, 0)),
                  _layer(_rows(g_attn), layer), _layer(w_in, layer),
                  pl.BlockSpec((1, HALO, MIX_W), lambda b, i: (b, 0, 0)),
                  _layer(w_pool, layer), _layer(_rows(pool_scale), layer),
                  _batch_spec(mem_k, layer), _batch_spec(mem_v, layer)],
        out_specs=[pl.BlockSpec((1, tm, 2 * MIX_W), lambda b, i: (b, i, 0)),
                   pl.BlockSpec((1, HALO, MIX_W), lambda b, i: (b, 0, 0))],
        out_shape=[jax.ShapeDtypeStruct((B, T, 2 * MIX_W), BF16),
                   jax.ShapeDtypeStruct((B, HALO, MIX_W), F32)],
        scratch_shapes=[pltpu.VMEM((HALO, MIX_W), F32)],
        compiler_params=_cparams("arbitrary", "arbitrary"),
        name="mix0",
    )(x, _rows(g_attn), w_in, hist16, w_pool, _rows(pool_scale), mem_k, mem_v)


def _lane_window(lane, lo):
    return (lane >= lo) & (lane < lo + N_SPLIT)


def _head_copies(buf, slot, hbm_refs, row0, tm, sem):
    out = []
    for t, hbm in enumerate(hbm_refs):
        for h in range(N_HEADS):
            src = buf.at[slot, :, pl.ds(t * MIX_W + h * HEAD_W, HEAD_W)]
            out.append(pltpu.make_async_copy(src, hbm.at[pl.ds(row0, tm), h, :], sem.at[slot]))
    return out


def _project_layer1(x, ga_ref, wi_ref, gk_ref, wk_ref, z_ref, bf_refs, k_hbm, v_hbm, kv_buf, sem, *, seq_len):
    tm = x.shape[0]
    i = pl.program_id(0)
    n_steps = pl.num_programs(0)
    slot = i % 2
    copies = functools.partial(_head_copies, kv_buf, hbm_refs=(k_hbm, v_hbm), tm=tm, sem=sem)

    xn = x * _inv_rms(x)
    z = _dot((xn * ga_ref[...]).astype(BF16), wi_ref[...])
    z_ref[:, :MIX_W] = (z[:, :MIX_W] * Q_SCALE).astype(BF16)
    z_ref[:, MIX_W:] = (z[:, MIX_W:] * QM_SCALE).astype(BF16)
    kv = _dot((xn * gk_ref[...]).astype(BF16), wk_ref[...])

    @pl.when(i >= 2)
    def _():
        for c in copies(slot, row0=(i - 2) * tm):
            c.wait()

    kv_buf[slot] = kv
    for c in copies(slot, row0=i * tm):
        c.start()

    @pl.when(i == n_steps - 1)
    def _():
        @pl.when(i >= 1)
        def _():
            for c in copies(1 - slot, row0=(i - 1) * tm):
                c.wait()
        for c in copies(slot, row0=i * tm):
            c.wait()

    if bf_refs:
        k = kv[:, :MIX_W]
        v = kv[:, MIX_W:]
        k1a_ref, k2a_ref, vb_ref = bf_refs
        lane = lax.broadcasted_iota(jnp.int32, (tm, HEAD_W), 1)
        pos = (i * tm) % seq_len + lax.broadcasted_iota(jnp.int32, (tm, HEAD_W), 0)
        off = pos % CHUNK
        base = (pos - off).astype(F32)
        off = off.astype(F32)

        def pos_lanes(lo):
            return jnp.where(_lane_window(lane, lo), base, jnp.where(_lane_window(lane, lo + N_SPLIT), off, 0.0))

        pos_hi = pos_lanes(DIFF_D)
        pos_lo = pos_lanes(0)
        for h in range(N_HEADS):
            hs = slice(h * HEAD_W, (h + 1) * HEAD_W)
            k1a_ref[:, hs] = jnp.where(lane < DIFF_D, k[:, hs], pos_hi).astype(BF16)
            k2a_ref[:, hs] = jnp.where(lane >= DIFF_D, k[:, hs], pos_lo).astype(BF16)
        vb_ref[...] = v.astype(BF16)


def _outmlp_kernel(x_ref, cat_ref, wo_ref, g_ref, w1_ref, w2_ref, gf_ref, *rest, final_norm, ff_chunk, n_bf, seq_len):
    if final_norm:
        out_ref, a_ref = rest
    else:
        ga_ref, wi_ref, gk_ref, wk_ref, out_ref, z_ref = rest[:6]
        bf_refs = rest[6:6 + n_bf]
        k_hbm, v_hbm, a_ref, kv_buf, sem = rest[6 + n_bf:]
    x1 = x_ref[...] + _dot(cat_ref[...], wo_ref[...])
    h = (x1 * _inv_rms(x1) * g_ref[...]).astype(BF16)
    for c in range(w1_ref.shape[1] // ff_chunk):
        cs = slice(c * ff_chunk, (c + 1) * ff_chunk)
        a = jnp.maximum(_dot(h, w1_ref[:, cs]), 0.0)
        a_ref[:, cs] = (a * a).astype(BF16)
    x2 = x1 + _dot(a_ref[...], w2_ref[...])
    if final_norm:
        out_ref[...] = x2 * _inv_rms(x2) * gf_ref[...]
    else:
        out_ref[...] = x2
        _project_layer1(x2, ga_ref, wi_ref, gk_ref, wk_ref, z_ref, bf_refs, k_hbm, v_hbm, kv_buf, sem,
                        seq_len=seq_len)


def _outmlp(x, cat, w, layer, tm, final_norm, seq_len=None, attn_operands=False):
    R, D = x.shape
    FF = w["w_ff1"].shape[-1]
    row = lambda i: (i, 0)
    fix = lambda i: (0, 0)
    n_bf = 3 if attn_operands else 0
    in_specs = [pl.BlockSpec((tm, D), row), pl.BlockSpec((tm, D), row),
                _layer(w["w_out"], layer), _layer(_rows(w["g_ffn"]), layer),
                _layer(w["w_ff1"], layer), _layer(w["w_ff2"], layer), _resident((1, D), fix)]
    args = [x, cat, w["w_out"], _rows(w["g_ffn"]), w["w_ff1"], w["w_ff2"], w["g_final"].reshape(1, D)]
    out_specs = [pl.BlockSpec((tm, D), row)]
    out_shape = [jax.ShapeDtypeStruct((R, D), F32)]
    scratch = [pltpu.VMEM((tm, FF), BF16)]
    if not final_norm:
        in_specs += [_layer(_rows(w["g_attn"]), layer + 1), _layer(w["w_in"], layer + 1),
                     _resident((1, D), fix), _resident((D, 2 * MIX_W), fix)]
        args += [_rows(w["g_attn"]), w["w_in"], w["g_kv"].reshape(1, D), w["w_kv"]]
        out_specs += ([pl.BlockSpec((tm, 2 * MIX_W), row)] + [pl.BlockSpec((tm, MIX_W), row)] * n_bf
                      + [pl.BlockSpec(memory_space=pl.ANY)] * 2)
        out_shape += ([jax.ShapeDtypeStruct((R, 2 * MIX_W), BF16)] + [jax.ShapeDtypeStruct((R, MIX_W), BF16)] * n_bf
                      + [jax.ShapeDtypeStruct((R, N_HEADS, HEAD_W), F32)] * 2)
        scratch += [pltpu.VMEM((2, tm, 2 * MIX_W), F32), pltpu.SemaphoreType.DMA((2,))]
    return pl.pallas_call(
        functools.partial(_outmlp_kernel, final_norm=final_norm, ff_chunk=1024, n_bf=n_bf, seq_len=seq_len),
        grid=(R // tm,),
        in_specs=in_specs,
        out_specs=out_specs,
        out_shape=out_shape,
        scratch_shapes=scratch,
        compiler_params=_cparams("arbitrary"),
        name="outmlp",
    )(*args)


def _lambda(lq, lam_init):
    a = jnp.sum(lq[0:1] * lq[1:2], axis=-1, keepdims=True)
    b = jnp.sum(lq[2:3] * lq[3:4], axis=-1, keepdims=True)
    return jnp.exp(a) - jnp.exp(b) + lam_init


def _slope2(h):
    return 2.0 ** (-8.0 * (h + 1) / N_HEADS) * LOG2E


def _bf16_pieces(c):
    out = []
    for _ in range(N_SPLIT):
        p = float(np.asarray(c, dtype=ml_dtypes.bfloat16))
        out.append(p)
        c -= p
    return out


def _slope_lanes(h, lo, shape):
    lane = lax.broadcasted_iota(jnp.int32, shape, 1)
    out = jnp.zeros(shape, F32)
    for p, piece in enumerate(_bf16_pieces(_slope2(h))):
        out = jnp.where((lane == lo + p) | (lane == lo + N_SPLIT + p), piece, out)
    return out.astype(BF16)


def _finish_head(o1, o2, lam, gs, lam_init):
    o = o1 - lam * o2
    return o * _inv_rms(o) * gs * (1.0 - lam_init)


def _diff_kernel(lq_ref, gs_ref, q_ref, qn_ref, qm_ref, k10_ref, k20_ref, v0_ref, k1n_ref, k2n_ref, vn_ref,
                 mk_ref, mv_ref, out_ref, kv_ref, qa_ref, acc_ref, m_ref, s_ref, *, tq, lam_init):
    kv0_refs = (k10_ref, k20_ref, v0_ref)
    kvn_refs = (k1n_ref, k2n_ref, vn_ref)
    i = pl.program_id(1)
    n_tiles = pl.num_programs(1)
    nblk = tq // HEAD_W
    n_chain = 2 * N_HEADS
    slot = i % 2
    row = lax.broadcasted_iota(jnp.int32, (tq, tq), 0)
    col = lax.broadcasted_iota(jnp.int32, (tq, tq), 1)
    ahead = jnp.where((col // CHUNK) <= (row // CHUNK), jnp.maximum(col - row, 0).astype(F32), -NEG_BIG)
    ones = jnp.ones((tq, HEAD_W), BF16)
    lane = lax.broadcasted_iota(jnp.int32, (tq, HEAD_W), 1)
    q0 = pl.multiple_of(i * tq, tq)
    qn0 = pl.multiple_of(jnp.minimum(i + 1, n_tiles - 1) * tq, tq)

    def hslice(n):
        return slice((n // 2) * HEAD_W, (n // 2 + 1) * HEAD_W)

    def scores(n, k0, q_slot=slot):
        return _dot_nt(qa_ref[q_slot * n_chain + n], kv_ref[n % 2, pl.ds(k0, tq), hslice(n)])

    def v_aug(n, k0):
        return jnp.concatenate([kv_ref[2, pl.ds(k0, tq), hslice(n)], ones], axis=1)

    def diag_fix(h):
        return (-2.0 * _slope2(h)) * ahead

    def build_queries(src_ref, q_slot):
        for h in range(N_HEADS):
            qh = src_ref[0, :, h * HEAD_W:(h + 1) * HEAD_W]
            base = q_slot * n_chain + 2 * h
            qa_ref[base] = jnp.where(lane < DIFF_D, qh, _slope_lanes(h, DIFF_D, qh.shape))
            qa_ref[base + 1] = jnp.where(lane >= DIFF_D, qh, _slope_lanes(h, 0, qh.shape))

    def consume(n, k0, first):
        vt = v_aug(n, k0)
        blocks = [s_ref[n, :, b * HEAD_W:(b + 1) * HEAD_W] for b in range(nblk)]
        m_row = jnp.max(functools.reduce(jnp.maximum, blocks), axis=-1, keepdims=True)
        if first:
            m_new = jnp.broadcast_to(m_row, (tq, HEAD_W))
        else:
            m_old = m_ref[n]
            m_new = jnp.maximum(m_old, m_row)
        p = jnp.concatenate([jnp.exp2(b - m_new) for b in blocks], axis=1).astype(BF16)
        pv = _dot(p, vt)
        if first:
            acc_ref[n] = pv
        else:
            alpha = jnp.exp2(m_old - m_new)
            acc_ref[n] = jnp.concatenate([alpha, alpha], axis=1) * acc_ref[n] + pv
        m_ref[n] = m_new

    def stage_next_diag(n):
        s_ref[n] = scores(n, qn0, 1 - slot) + diag_fix(n // 2)

    @pl.when(i == 0)
    def _():
        for c in range(3):
            kv_ref[c, pl.ds(0, tq), :] = kv0_refs[c][0]
        build_queries(q_ref, slot)
        for n in range(n_chain):
            s_ref[n] = scores(n, q0) + diag_fix(n // 2)

    for c in range(3):
        kv_ref[c, pl.ds(qn0, tq), :] = kvn_refs[c][0]
    build_queries(qn_ref, 1 - slot)
    _mem_attend(qm_ref[0], mk_ref, mv_ref, out_ref)

    @pl.when(i == 0)
    def _():
        for n in range(n_chain):
            consume(n, q0, True)
            stage_next_diag(n)

    @pl.when(i > 0)
    def _():
        for n in range(n_chain):
            consume(n, q0, True)
            s_ref[n] = scores(n, 0)

        def body(j, carry):
            k_cur = pl.multiple_of((j - 1) * tq, tq)
            k_next = pl.multiple_of(j * tq, tq)
            for n in range(n_chain):
                consume(n, k_cur, False)
                s_ref[n] = scores(n, k_next)
            return carry

        lax.fori_loop(1, i, body, 0)
        k_last = pl.multiple_of((i - 1) * tq, tq)
        for n in range(n_chain):
            consume(n, k_last, False)
            stage_next_diag(n)

    lam = _lambda(lq_ref[...], lam_init)
    for h in range(N_HEADS):
        o = [acc_ref[n, :, :HEAD_W] / acc_ref[n, :, HEAD_W:] for n in (2 * h, 2 * h + 1)]
        out_ref[0, :, h * HEAD_W:(h + 1) * HEAD_W] = _finish_head(
            o[0], o[1], lam, gs_ref[...], lam_init).astype(out_ref.dtype)


def _diff_prompt(z, k1a, k2a, vb, mem_k, mem_v, lq, g_sub, layer, lam_init, tq):
    B, T, _ = z.shape
    n_chain = 2 * N_HEADS
    n_tiles = T // tq
    diff_layer = layer - (mem_k.shape[0] - lq.shape[0])
    nxt = lambda i: jnp.minimum(i + 1, n_tiles - 1)
    return pl.pallas_call(
        functools.partial(_diff_kernel, tq=tq, lam_init=lam_init),
        grid=(B, n_tiles),
        in_specs=[_layer(lq, diff_layer), _layer(_rows(g_sub), diff_layer),
                  pl.BlockSpec((1, tq, MIX_W), lambda b, i: (b, i, 0)),
                  pl.BlockSpec((1, tq, MIX_W), lambda b, i: (b, nxt(i), 0)),
                  pl.BlockSpec((1, tq, MIX_W), lambda b, i: (b, i, 1))]
                 + [_resident((1, tq, MIX_W), lambda b, i: (b, 0, 0))] * 3
                 + [pl.BlockSpec((1, tq, MIX_W), lambda b, i: (b, nxt(i), 0))] * 3
                 + [_batch_spec(mem_k, layer), _batch_spec(mem_v, layer)],
        out_specs=pl.BlockSpec((1, tq, 2 * MIX_W), lambda b, i: (b, i, 0)),
        out_shape=jax.ShapeDtypeStruct((B, T, 2 * MIX_W), BF16),
        scratch_shapes=[pltpu.VMEM((3, T, MIX_W), BF16),
                        pltpu.VMEM((2 * n_chain, tq, HEAD_W), BF16),
                        pltpu.VMEM((n_chain, tq, 2 * HEAD_W), F32),
                        pltpu.VMEM((n_chain, tq, HEAD_W), F32),
                        pltpu.VMEM((n_chain, tq, tq), F32)],
        compiler_params=_cparams("arbitrary", "arbitrary"),
        name="diff_prompt",
    )(lq, _rows(g_sub), z, z, z, k1a, k2a, vb, k1a, k2a, vb, mem_k, mem_v)


def _cache_copies(hbm_refs, b, buf, slot, sem):
    return [pltpu.make_async_copy(hbm.at[b, :, h, :], buf.at[slot, t, h], sem.at[slot])
            for t, hbm in enumerate(hbm_refs) for h in range(N_HEADS)]


def _diff_sample_kernel(lq_ref, gs_ref, q_ref, qm_ref, ck_hbm, cv_hbm, nk_ref, nv_ref, mk_ref, mv_ref, out_ref,
                        cache_buf, sem, *, p0, lam_init):
    b = pl.program_id(0)
    slot = b % 2
    copies = functools.partial(_cache_copies, (ck_hbm, cv_hbm), buf=cache_buf, sem=sem)

    @pl.when(b == 0)
    def _():
        for c in copies(b, slot=slot):
            c.start()

    @pl.when(b + 1 < pl.num_programs(0))
    def _():
        for c in copies(b + 1, slot=1 - slot):
            c.start()

    for c in copies(b, slot=slot):
        c.wait()

    ts = q_ref.shape[1]
    lam = _lambda(lq_ref[...], lam_init)
    qpos = p0 + lax.broadcasted_iota(jnp.int32, (2 * ts, p0), 0) % ts
    dist_p = (qpos - lax.broadcasted_iota(jnp.int32, (2 * ts, p0), 1)).astype(F32)
    row = p0 + lax.broadcasted_iota(jnp.int32, (2 * ts, ts), 0) % ts
    col = p0 + lax.broadcasted_iota(jnp.int32, (2 * ts, ts), 1)
    dist_n = jnp.abs(row - col).astype(F32)
    vis_n = (col // CHUNK) <= (row // CHUNK)
    lane = lax.broadcasted_iota(jnp.int32, (ts, HEAD_W), 1)
    for h in range(N_HEADS):
        hs = slice(h * HEAD_W, (h + 1) * HEAD_W)
        qh = q_ref[0, :, hs]
        zero = jnp.zeros_like(qh)
        q2 = jnp.concatenate([jnp.where(lane < DIFF_D, qh, zero), jnp.where(lane >= DIFF_D, qh, zero)], axis=0)
        kp = cache_buf[slot, 0, h].astype(BF16)
        vp = cache_buf[slot, 1, h].astype(BF16)
        kn = _head(nk_ref, h).astype(BF16)
        vn = _head(nv_ref, h).astype(BF16)
        sp = _dot_nt(q2, kp) - _slope2(h) * dist_p
        sn = _dot_nt(q2, kn) + jnp.where(vis_n, -_slope2(h) * dist_n, NEG_BIG)
        m = jnp.maximum(jnp.max(sp, axis=-1, keepdims=True), jnp.max(sn, axis=-1, keepdims=True))
        ep = jnp.exp2(sp - m)
        en = jnp.exp2(sn - m)
        l = jnp.sum(ep, axis=-1, keepdims=True) + jnp.sum(en, axis=-1, keepdims=True)
        o = (_dot(ep.astype(BF16), vp) + _dot(en.astype(BF16), vn)) / l
        out_ref[0, :, hs] = _finish_head(o[:ts], o[ts:], lam, gs_ref[...], lam_init).astype(out_ref.dtype)
    _mem_attend(qm_ref[0], mk_ref, mv_ref, out_ref)


def _diff_sample(z, cache_k, cache_v, k_new, v_new, mem_k, mem_v, lq, g_sub, layer, lam_init):
    B, Ts, _ = z.shape
    P0 = cache_k.shape[1]
    diff_layer = layer - (mem_k.shape[0] - lq.shape[0])
    return pl.pallas_call(
        functools.partial(_diff_sample_kernel, p0=P0, lam_init=lam_init),
        grid=(B,),
        in_specs=[_layer(lq, diff_layer), _layer(_rows(g_sub), diff_layer),
                  pl.BlockSpec((1, Ts, MIX_W), lambda b: (b, 0, 0)),
                  pl.BlockSpec((1, Ts, MIX_W), lambda b: (b, 0, 1)),
                  pl.BlockSpec(memory_space=pl.ANY), pl.BlockSpec(memory_space=pl.ANY),
                  _batch_spec(k_new), _batch_spec(v_new), _batch_spec(mem_k, layer), _batch_spec(mem_v, layer)],
        out_specs=pl.BlockSpec((1, Ts, 2 * MIX_W), lambda b: (b, 0, 0)),
        out_shape=jax.ShapeDtypeStruct((B, Ts, 2 * MIX_W), BF16),
        scratch_shapes=[pltpu.VMEM((2, 2, N_HEADS, P0, HEAD_W), F32), pltpu.SemaphoreType.DMA((2,))],
        compiler_params=_cparams("arbitrary"),
        name="diff_sample",
    )(lq, _rows(g_sub), z, z, cache_k, cache_v, k_new, v_new, mem_k, mem_v)


def _lambda_init(layer_idx):
    return 0.8 - 0.6 * math.exp(-0.3 * layer_idx)


def _trunk(x, pos0, pool_hist, mem_k, mem_v, past_k, past_v, w, tm_rows, tm_seq, tq):
    B, T, D = x.shape
    R = B * T
    xr = x.reshape(R, D)
    hist16 = jnp.pad(pool_hist, ((0, 0), (HALO - POOL_HIST, 0), (0, 0)))

    cat0, tail = _mix0(x, w["g_attn"], w["w_in"], hist16, mem_k, mem_v, w["w_pool"], w["pool_scale"], 0,
                       tm_seq, pos0)
    new_pool = tail[:, HALO - POOL_HIST:]
    prompt = past_k is None
    x1, z, *bf, k, v = _outmlp(xr, cat0.reshape(R, D), w, 0, tm_rows, final_norm=False, seq_len=T,
                               attn_operands=prompt)
    z3 = z.reshape(B, T, D)
    lam_init = _lambda_init(1)
    if prompt:
        k1a, k2a, vb = (a.reshape(B, T, MIX_W) for a in bf)
        cat1 = _diff_prompt(z3, k1a, k2a, vb, mem_k, mem_v, w["lambda_qk"], w["g_subln"], 1, lam_init, tq)
    else:
        shp = (B, T, N_HEADS, HEAD_W)
        cat1 = _diff_sample(z3, past_k, past_v, k.reshape(shp), v.reshape(shp), mem_k, mem_v,
                            w["lambda_qk"], w["g_subln"], 1, lam_init)
    y, = _outmlp(x1, cat1.reshape(R, D), w, 1, tm_rows, final_norm=True)
    shp = (B, T, N_HEADS, HEAD_W)
    return y.reshape(B, T, D), new_pool[None], k.reshape(shp), v.reshape(shp)


def kernel(x_prompt, x_sample, mem_prompt, cache_k, cache_v, cache_mem_k, cache_mem_v, state_pool, g_attn, w_in, w_out, g_mem, w_mem_kv, g_ffn, w_ff1, w_ff2, w_pool, pool_scale, lambda_qk, g_subln, g_kv, w_kv, g_final):
    B, T, D = x_prompt.shape
    Bs, Ts, _ = x_sample.shape
    P0 = cache_k.shape[1]
    w = dict(g_attn=g_attn, w_in=w_in.astype(BF16), w_out=w_out.astype(BF16), g_ffn=g_ffn,
             w_ff1=w_ff1.astype(BF16), w_ff2=w_ff2.astype(BF16), w_pool=w_pool.astype(BF16),
             pool_scale=pool_scale, lambda_qk=lambda_qk, g_subln=g_subln, g_kv=g_kv,
             w_kv=w_kv.astype(BF16), g_final=g_final)

    mem_k_p, mem_v_p, mem_k_d, mem_v_d = _mem_kv(mem_prompt, g_mem, w_mem_kv.astype(BF16))
    hist0 = jnp.zeros((B, POOL_HIST, MIX_W), F32)
    y_p, pool_p, k_p, v_p = _trunk(x_prompt, 0, hist0, mem_k_d, mem_v_d, None, None, w,
                                   tm_rows=512, tm_seq=2048, tq=512)
    y_s, pool_s, k_s, v_s = _trunk(x_sample, P0, state_pool[0], cache_mem_k, cache_mem_v, cache_k, cache_v, w,
                                   tm_rows=Bs * Ts, tm_seq=Ts, tq=None)
    return (y_p, y_s, mem_k_p, mem_v_p, pool_p, k_p, v_p, pool_s, k_s, v_s)
```

```python
import functools
import math

import jax
import jax.numpy as jnp
import ml_dtypes
import numpy as np
from jax import lax
from jax.experimental import pallas as pl
from jax.experimental.pallas import tpu as pltpu

F32 = jnp.float32
BF16 = jnp.bfloat16

CHUNK = 64
POOL_WINDOWS = (2, 4, 8, 16)
POOL_HIST = 15
HALO = 16
N_HEADS = 4
HEAD_W = 128
DIFF_D = 64
MIX_W = N_HEADS * HEAD_W
EPS = 1e-6
NEG_BIG = -1e30
LOG2E = math.log2(math.e)
Q_SCALE = DIFF_D ** -0.5 * LOG2E
QM_SCALE = HEAD_W ** -0.5 * LOG2E
N_SPLIT = 4
MIX0_SUB_ROWS = 512
VMEM_LIMIT = 56 * 1024 * 1024


def _cparams(*sem):
    return pltpu.CompilerParams(dimension_semantics=sem, vmem_limit_bytes=VMEM_LIMIT)


def _resident(shape, index_map):
    return pl.BlockSpec(shape, index_map, pipeline_mode=pl.Buffered(1))


def _layer(arr, layer):
    zeros = (0,) * (arr.ndim - 1)
    return pl.BlockSpec((None,) + arr.shape[1:], lambda *_: (layer,) + zeros, pipeline_mode=pl.Buffered(1))


def _rows(g):
    return g.reshape(g.shape[0], 1, g.shape[1])


def _inv_rms(x):
    return lax.rsqrt(jnp.mean(x * x, axis=-1, keepdims=True) + EPS)


def _dot(a, b):
    return jnp.dot(a, b, preferred_element_type=F32)


def _dot_nt(a, b):
    return lax.dot_general(a, b, (((1,), (1,)), ((), ())), preferred_element_type=F32)


def _memkv_kernel(x_ref, g_ref, w_ref, k_ref, v_ref, kd_ref, vd_ref):
    x = x_ref[0]
    h = (x * _inv_rms(x) * g_ref[0]).astype(BF16)
    kv = _dot(h, w_ref[0])
    kd_ref[0, 0] = kv[:, :MIX_W]
    vd_ref[0, 0] = kv[:, MIX_W:]
    for hd in range(N_HEADS):
        k_ref[0, 0, :, hd, :] = kv[:, hd * HEAD_W:(hd + 1) * HEAD_W]
        v_ref[0, 0, :, hd, :] = kv[:, MIX_W + hd * HEAD_W:MIX_W + (hd + 1) * HEAD_W]


def _mem_kv(mem, g_mem, w_mem_kv):
    B, M, D = mem.shape
    L = g_mem.shape[0]
    out = jax.ShapeDtypeStruct((L, B, M, N_HEADS, HEAD_W), F32)
    dense = jax.ShapeDtypeStruct((L, B, M, MIX_W), F32)
    return pl.pallas_call(
        _memkv_kernel,
        grid=(L, B),
        in_specs=[pl.BlockSpec((1, M, D), lambda l, b: (b, 0, 0)),
                  pl.BlockSpec((1, 1, D), lambda l, b: (l, 0, 0)),
                  pl.BlockSpec((1, D, 2 * MIX_W), lambda l, b: (l, 0, 0))],
        out_specs=[pl.BlockSpec((1, 1, M, N_HEADS, HEAD_W), lambda l, b: (l, b, 0, 0, 0))] * 2
                  + [pl.BlockSpec((1, 1, M, MIX_W), lambda l, b: (l, b, 0, 0))] * 2,
        out_shape=[out, out, dense, dense],
        compiler_params=_cparams("arbitrary", "arbitrary"),
        name="mem_kv",
    )(mem, g_mem.reshape(L, 1, D), w_mem_kv)


def _batch_spec(arr, layer=None):
    if layer is None:
        zeros = (0,) * (arr.ndim - 1)
        return pl.BlockSpec((1,) + arr.shape[1:], lambda b, *_: (b,) + zeros)
    zeros = (0,) * (arr.ndim - 2)
    return pl.BlockSpec((None, 1) + arr.shape[2:], lambda b, *_: (layer, b) + zeros)


def _head(ref, h):
    if len(ref.shape) == 3:
        return ref[0, :, h * HEAD_W:(h + 1) * HEAD_W]
    return ref[0, :, h, :]


def _mem_attend(qm, mk_ref, mv_ref, out_ref, rs=slice(None)):
    ones = jnp.ones((mk_ref.shape[1], HEAD_W), BF16)
    for h in range(N_HEADS):
        hs = slice(h * HEAD_W, (h + 1) * HEAD_W)
        kh = _head(mk_ref, h).astype(BF16)
        vh = jnp.concatenate([_head(mv_ref, h).astype(BF16), ones], axis=1)
        s = _dot_nt(qm[:, hs], kh)
        e = jnp.exp2(s - jnp.max(s, axis=-1, keepdims=True))
        pv = _dot(e.astype(BF16), vh)
        o = pv[:, :HEAD_W] / pv[:, HEAD_W:]
        out_ref[0, rs, MIX_W + h * HEAD_W:MIX_W + (h + 1) * HEAD_W] = o.astype(out_ref.dtype)


def _mix0_kernel(x_ref, g_ref, wi_ref, hist_ref, wp_ref, ps_ref, mk_ref, mv_ref, *rest, tm, sub, pos0, n_cast):
    cast_in = rest[:n_cast]
    out_ref, tail_ref = rest[n_cast:n_cast + 2]
    cast_out = rest[n_cast + 2:2 * n_cast + 2]
    carry_ref = rest[2 * n_cast + 2]
    for src, dst in zip(cast_in, cast_out):
        dst[...] = src[...].astype(dst.dtype)
    i = pl.program_id(1)

    @pl.when(i == 0)
    def _():
        carry_ref[...] = hist_ref[0]

    halo = carry_ref[...]
    for r in range(tm // sub):
        rs = slice(r * sub, (r + 1) * sub)
        x = x_ref[0, rs, :]
        z = _dot((x * _inv_rms(x) * g_ref[...]).astype(BF16), wi_ref[...])
        u = z[:, :MIX_W]
        qm = (z[:, MIX_W:] * QM_SCALE).astype(BF16)
        ext = jnp.concatenate([halo, u], axis=0)
        halo = u[sub - HALO:]
        pos = pos0 + i * tm + r * sub + lax.broadcasted_iota(jnp.int32, (sub, HEAD_W), 0)
        for g, w in enumerate(POOL_WINDOWS):
            gs = slice(g * HEAD_W, (g + 1) * HEAD_W)
            acc = ext[:, gs]
            span = 1
            while span < w:
                acc = acc + pltpu.roll(acc, span, 0)
                span *= 2
            cnt = jnp.minimum(pos + 1, w).astype(F32)
            d = acc[HALO:] / cnt - u[:, gs]
            t = _dot(d.astype(BF16), wp_ref[g]) * ps_ref[:, gs]
            out_ref[0, rs, gs] = t.astype(out_ref.dtype)
        _mem_attend(qm, mk_ref, mv_ref, out_ref, rs)
    carry_ref[...] = halo
    tail_ref[0] = halo


def _mix0(x, g_attn, w_in, hist16, mem_k, mem_v, w_pool, pool_scale, layer, tm, pos0, cast=()):
    B, T, D = x.shape
    sub = min(tm, MIX0_SUB_ROWS)
    n_i = T // tm
    assert T % tm == 0 and tm % sub == 0 and sub >= HALO

    def chunk(a):
        assert a.shape[1] % (B * n_i) == 0
        return pl.BlockSpec((a.shape[0], a.shape[1] // (B * n_i), a.shape[2]), lambda b, i: (0, b * n_i + i, 0))

    return pl.pallas_call(
        functools.partial(_mix0_kernel, tm=tm, sub=sub, pos0=pos0, n_cast=len(cast)),
        grid=(B, n_i),
        in_specs=[pl.BlockSpec((1, tm, D), lambda b, i: (b, i, 0)),
                  _layer(_rows(g_attn), layer), _layer(w_in, layer),
                  pl.BlockSpec((1, HALO, MIX_W), lambda b, i: (b, 0, 0)),
                  _layer(w_pool, layer), _layer(_rows(pool_scale), layer),
                  _batch_spec(mem_k, layer), _batch_spec(mem_v, layer)] + [chunk(a) for a in cast],
        out_specs=[pl.BlockSpec((1, tm, 2 * MIX_W), lambda b, i: (b, i, 0)),
                   pl.BlockSpec((1, HALO, MIX_W), lambda b, i: (b, 0, 0))] + [chunk(a) for a in cast],
        out_shape=[jax.ShapeDtypeStruct((B, T, 2 * MIX_W), BF16),
                   jax.ShapeDtypeStruct((B, HALO, MIX_W), F32)]
                  + [jax.ShapeDtypeStruct(a.shape, BF16) for a in cast],
        scratch_shapes=[pltpu.VMEM((HALO, MIX_W), F32)],
        compiler_params=_cparams("arbitrary", "arbitrary"),
        name="mix0",
    )(x, _rows(g_attn), w_in, hist16, w_pool, _rows(pool_scale), mem_k, mem_v, *cast)


def _lane_window(lane, lo):
    return (lane >= lo) & (lane < lo + N_SPLIT)


def _head_copies(buf, slot, hbm_refs, row0, tm, sem):
    out = []
    for t, hbm in enumerate(hbm_refs):
        for h in range(N_HEADS):
            src = buf.at[slot, :, pl.ds(t * MIX_W + h * HEAD_W, HEAD_W)]
            out.append(pltpu.make_async_copy(src, hbm.at[pl.ds(row0, tm), h, :], sem.at[slot]))
    return out


def _project_layer1(x, ga_ref, wi_ref, gk_ref, wk_ref, z_ref, bf_refs, k_hbm, v_hbm, kv_buf, sem, *, seq_len):
    tm = x.shape[0]
    i = pl.program_id(0)
    n_steps = pl.num_programs(0)
    slot = i % 2
    copies = functools.partial(_head_copies, kv_buf, hbm_refs=(k_hbm, v_hbm), tm=tm, sem=sem)

    xn = x * _inv_rms(x)
    z = _dot((xn * ga_ref[...]).astype(BF16), wi_ref[...])
    z_ref[:, :MIX_W] = (z[:, :MIX_W] * Q_SCALE).astype(BF16)
    z_ref[:, MIX_W:] = (z[:, MIX_W:] * QM_SCALE).astype(BF16)
    kv = _dot((xn * gk_ref[...]).astype(BF16), wk_ref[...])

    @pl.when(i >= 2)
    def _():
        for c in copies(slot, row0=(i - 2) * tm):
            c.wait()

    kv_buf[slot] = kv
    for c in copies(slot, row0=i * tm):
        c.start()

    @pl.when(i == n_steps - 1)
    def _():
        @pl.when(i >= 1)
        def _():
            for c in copies(1 - slot, row0=(i - 1) * tm):
                c.wait()
        for c in copies(slot, row0=i * tm):
            c.wait()

    if bf_refs:
        k = kv[:, :MIX_W]
        v = kv[:, MIX_W:]
        k1a_ref, k2a_ref, vb_ref = bf_refs
        lane = lax.broadcasted_iota(jnp.int32, (tm, HEAD_W), 1)
        pos = (i * tm) % seq_len + lax.broadcasted_iota(jnp.int32, (tm, HEAD_W), 0)
        off = pos % CHUNK
        base = (pos - off).astype(F32)
        off = off.astype(F32)

        def pos_lanes(lo):
            return jnp.where(_lane_window(lane, lo), base, jnp.where(_lane_window(lane, lo + N_SPLIT), off, 0.0))

        pos_hi = pos_lanes(DIFF_D)
        pos_lo = pos_lanes(0)
        for h in range(N_HEADS):
            hs = slice(h * HEAD_W, (h + 1) * HEAD_W)
            k1a_ref[:, hs] = jnp.where(lane < DIFF_D, k[:, hs], pos_hi).astype(BF16)
            k2a_ref[:, hs] = jnp.where(lane >= DIFF_D, k[:, hs], pos_lo).astype(BF16)
        vb_ref[...] = v.astype(BF16)


def _outmlp_kernel(x_ref, cat_ref, wo_ref, g_ref, w1_ref, w2_ref, gf_ref, *rest, final_norm, ff_chunk, n_bf, seq_len):
    if final_norm:
        out_ref, a_ref = rest
    else:
        ga_ref, wi_ref, gk_ref, wk_ref, out_ref, z_ref = rest[:6]
        bf_refs = rest[6:6 + n_bf]
        k_hbm, v_hbm, a_ref, kv_buf, sem = rest[6 + n_bf:]
    x1 = x_ref[...] + _dot(cat_ref[...], wo_ref[...])
    h = (x1 * _inv_rms(x1) * g_ref[...]).astype(BF16)
    for c in range(w1_ref.shape[1] // ff_chunk):
        cs = slice(c * ff_chunk, (c + 1) * ff_chunk)
        a = jnp.maximum(_dot(h, w1_ref[:, cs]), 0.0)
        a_ref[:, cs] = (a * a).astype(BF16)
    x2 = x1 + _dot(a_ref[...], w2_ref[...])
    if final_norm:
        out_ref[...] = x2 * _inv_rms(x2) * gf_ref[...]
    else:
        out_ref[...] = x2
        _project_layer1(x2, ga_ref, wi_ref, gk_ref, wk_ref, z_ref, bf_refs, k_hbm, v_hbm, kv_buf, sem,
                        seq_len=seq_len)


def _outmlp(x, cat, w, layer, tm, final_norm, seq_len=None, attn_operands=False):
    R, D = x.shape
    FF = w["w_ff1"].shape[-1]
    row = lambda i: (i, 0)
    fix = lambda i: (0, 0)
    n_bf = 3 if attn_operands else 0
    in_specs = [pl.BlockSpec((tm, D), row), pl.BlockSpec((tm, D), row),
                _layer(w["w_out"], layer), _layer(_rows(w["g_ffn"]), layer),
                _layer(w["w_ff1"], layer), _layer(w["w_ff2"], layer), _resident((1, D), fix)]
    args = [x, cat, w["w_out"], _rows(w["g_ffn"]), w["w_ff1"], w["w_ff2"], w["g_final"].reshape(1, D)]
    out_specs = [pl.BlockSpec((tm, D), row)]
    out_shape = [jax.ShapeDtypeStruct((R, D), F32)]
    scratch = [pltpu.VMEM((tm, FF), BF16)]
    if not final_norm:
        in_specs += [_layer(_rows(w["g_attn"]), layer + 1), _layer(w["w_in"], layer + 1),
                     _resident((1, D), fix), _resident((D, 2 * MIX_W), fix)]
        args += [_rows(w["g_attn"]), w["w_in"], w["g_kv"].reshape(1, D), w["w_kv"]]
        out_specs += ([pl.BlockSpec((tm, 2 * MIX_W), row)] + [pl.BlockSpec((tm, MIX_W), row)] * n_bf
                      + [pl.BlockSpec(memory_space=pl.ANY)] * 2)
        out_shape += ([jax.ShapeDtypeStruct((R, 2 * MIX_W), BF16)] + [jax.ShapeDtypeStruct((R, MIX_W), BF16)] * n_bf
                      + [jax.ShapeDtypeStruct((R, N_HEADS, HEAD_W), F32)] * 2)
        scratch += [pltpu.VMEM((2, tm, 2 * MIX_W), F32), pltpu.SemaphoreType.DMA((2,))]
    return pl.pallas_call(
        functools.partial(_outmlp_kernel, final_norm=final_norm, ff_chunk=1024, n_bf=n_bf, seq_len=seq_len),
        grid=(R // tm,),
        in_specs=in_specs,
        out_specs=out_specs,
        out_shape=out_shape,
        scratch_shapes=scratch,
        compiler_params=_cparams("arbitrary"),
        name="outmlp",
    )(*args)


def _lambda(lq, lam_init):
    a = jnp.sum(lq[0:1] * lq[1:2], axis=-1, keepdims=True)
    b = jnp.sum(lq[2:3] * lq[3:4], axis=-1, keepdims=True)
    return jnp.exp(a) - jnp.exp(b) + lam_init


def _slope2(h):
    return 2.0 ** (-8.0 * (h + 1) / N_HEADS) * LOG2E


def _bf16_pieces(c):
    out = []
    for _ in range(N_SPLIT):
        p = float(np.asarray(c, dtype=ml_dtypes.bfloat16))
        out.append(p)
        c -= p
    return out


def _slope_lanes(h, lo, shape):
    lane = lax.broadcasted_iota(jnp.int32, shape, 1)
    out = jnp.zeros(shape, F32)
    for p, piece in enumerate(_bf16_pieces(_slope2(h))):
        out = jnp.where((lane == lo + p) | (lane == lo + N_SPLIT + p), piece, out)
    return out.astype(BF16)


def _finish_head(o1, o2, lam, gs, lam_init):
    o = o1 - lam * o2
    return o * _inv_rms(o) * gs * (1.0 - lam_init)


def _diff_kernel(lq_ref, gs_ref, q_ref, qn_ref, qm_ref, k10_ref, k20_ref, v0_ref, k1n_ref, k2n_ref, vn_ref,
                 mk_ref, mv_ref, out_ref, kv_ref, qa_ref, acc_ref, m_ref, s_ref, *, tq, lam_init):
    kv0_refs = (k10_ref, k20_ref, v0_ref)
    kvn_refs = (k1n_ref, k2n_ref, vn_ref)
    i = pl.program_id(1)
    n_tiles = pl.num_programs(1)
    nblk = tq // HEAD_W
    n_chain = 2 * N_HEADS
    slot = i % 2
    row = lax.broadcasted_iota(jnp.int32, (tq, tq), 0)
    col = lax.broadcasted_iota(jnp.int32, (tq, tq), 1)
    ahead = jnp.where((col // CHUNK) <= (row // CHUNK), jnp.maximum(col - row, 0).astype(F32), -NEG_BIG)
    ones = jnp.ones((tq, HEAD_W), BF16)
    lane = lax.broadcasted_iota(jnp.int32, (tq, HEAD_W), 1)
    q0 = pl.multiple_of(i * tq, tq)
    qn0 = pl.multiple_of(jnp.minimum(i + 1, n_tiles - 1) * tq, tq)

    def hslice(n):
        return slice((n // 2) * HEAD_W, (n // 2 + 1) * HEAD_W)

    def scores(n, k0, q_slot=slot):
        return _dot_nt(qa_ref[q_slot * n_chain + n], kv_ref[n % 2, pl.ds(k0, tq), hslice(n)])

    def v_aug(n, k0):
        return jnp.concatenate([kv_ref[2, pl.ds(k0, tq), hslice(n)], ones], axis=1)

    def diag_fix(h):
        return (-2.0 * _slope2(h)) * ahead

    def build_queries(src_ref, q_slot):
        for h in range(N_HEADS):
            qh = src_ref[0, :, h * HEAD_W:(h + 1) * HEAD_W]
            base = q_slot * n_chain + 2 * h
            qa_ref[base] = jnp.where(lane < DIFF_D, qh, _slope_lanes(h, DIFF_D, qh.shape))
            qa_ref[base + 1] = jnp.where(lane >= DIFF_D, qh, _slope_lanes(h, 0, qh.shape))

    def consume(n, k0, first):
        vt = v_aug(n, k0)
        blocks = [s_ref[n, :, b * HEAD_W:(b + 1) * HEAD_W] for b in range(nblk)]
        m_row = jnp.max(functools.reduce(jnp.maximum, blocks), axis=-1, keepdims=True)
        if first:
            m_new = jnp.broadcast_to(m_row, (tq, HEAD_W))
        else:
            m_old = m_ref[n]
            m_new = jnp.maximum(m_old, m_row)
        p = jnp.concatenate([jnp.exp2(b - m_new) for b in blocks], axis=1).astype(BF16)
        pv = _dot(p, vt)
        if first:
            acc_ref[n] = pv
        else:
            alpha = jnp.exp2(m_old - m_new)
            acc_ref[n] = jnp.concatenate([alpha, alpha], axis=1) * acc_ref[n] + pv
        m_ref[n] = m_new

    def stage_next_diag(n):
        s_ref[n] = scores(n, qn0, 1 - slot) + diag_fix(n // 2)

    @pl.when(i == 0)
    def _():
        for c in range(3):
            kv_ref[c, pl.ds(0, tq), :] = kv0_refs[c][0]
        build_queries(q_ref, slot)
        for n in range(n_chain):
            s_ref[n] = scores(n, q0) + diag_fix(n // 2)

    for c in range(3):
        kv_ref[c, pl.ds(qn0, tq), :] = kvn_refs[c][0]
    build_queries(qn_ref, 1 - slot)
    _mem_attend(qm_ref[0], mk_ref, mv_ref, out_ref)

    @pl.when(i == 0)
    def _():
        for n in range(n_chain):
            consume(n, q0, True)
            stage_next_diag(n)

    @pl.when(i > 0)
    def _():
        for n in range(n_chain):
            consume(n, q0, True)
            s_ref[n] = scores(n, 0)

        def body(j, carry):
            k_cur = pl.multiple_of((j - 1) * tq, tq)
            k_next = pl.multiple_of(j * tq, tq)
            for n in range(n_chain):
                consume(n, k_cur, False)
                s_ref[n] = scores(n, k_next)
            return carry

        lax.fori_loop(1, i, body, 0)
        k_last = pl.multiple_of((i - 1) * tq, tq)
        for n in range(n_chain):
            consume(n, k_last, False)
            stage_next_diag(n)

    lam = _lambda(lq_ref[...], lam_init)
    for h in range(N_HEADS):
        o = [acc_ref[n, :, :HEAD_W] / acc_ref[n, :, HEAD_W:] for n in (2 * h, 2 * h + 1)]
        out_ref[0, :, h * HEAD_W:(h + 1) * HEAD_W] = _finish_head(
            o[0], o[1], lam, gs_ref[...], lam_init).astype(out_ref.dtype)


def _diff_prompt(z, k1a, k2a, vb, mem_k, mem_v, lq, g_sub, layer, lam_init, tq):
    B, T, _ = z.shape
    n_chain = 2 * N_HEADS
    n_tiles = T // tq
    diff_layer = layer - (mem_k.shape[0] - lq.shape[0])
    nxt = lambda i: jnp.minimum(i + 1, n_tiles - 1)
    return pl.pallas_call(
        functools.partial(_diff_kernel, tq=tq, lam_init=lam_init),
        grid=(B, n_tiles),
        in_specs=[_layer(lq, diff_layer), _layer(_rows(g_sub), diff_layer),
                  pl.BlockSpec((1, tq, MIX_W), lambda b, i: (b, i, 0)),
                  pl.BlockSpec((1, tq, MIX_W), lambda b, i: (b, nxt(i), 0)),
                  pl.BlockSpec((1, tq, MIX_W), lambda b, i: (b, i, 1))]
                 + [_resident((1, tq, MIX_W), lambda b, i: (b, 0, 0))] * 3
                 + [pl.BlockSpec((1, tq, MIX_W), lambda b, i: (b, nxt(i), 0))] * 3
                 + [_batch_spec(mem_k, layer), _batch_spec(mem_v, layer)],
        out_specs=pl.BlockSpec((1, tq, 2 * MIX_W), lambda b, i: (b, i, 0)),
        out_shape=jax.ShapeDtypeStruct((B, T, 2 * MIX_W), BF16),
        scratch_shapes=[pltpu.VMEM((3, T, MIX_W), BF16),
                        pltpu.VMEM((2 * n_chain, tq, HEAD_W), BF16),
                        pltpu.VMEM((n_chain, tq, 2 * HEAD_W), F32),
                        pltpu.VMEM((n_chain, tq, HEAD_W), F32),
                        pltpu.VMEM((n_chain, tq, tq), F32)],
        compiler_params=_cparams("arbitrary", "arbitrary"),
        name="diff_prompt",
    )(lq, _rows(g_sub), z, z, z, k1a, k2a, vb, k1a, k2a, vb, mem_k, mem_v)


def _cache_copies(hbm_refs, b, buf, slot, sem):
    return [pltpu.make_async_copy(hbm.at[b, :, h, :], buf.at[slot, t, h], sem.at[slot])
            for t, hbm in enumerate(hbm_refs) for h in range(N_HEADS)]


def _diff_sample_kernel(lq_ref, gs_ref, q_ref, qm_ref, ck_hbm, cv_hbm, nk_ref, nv_ref, mk_ref, mv_ref, out_ref,
                        cache_buf, sem, *, p0, lam_init):
    b = pl.program_id(0)
    slot = b % 2
    copies = functools.partial(_cache_copies, (ck_hbm, cv_hbm), buf=cache_buf, sem=sem)

    @pl.when(b == 0)
    def _():
        for c in copies(b, slot=slot):
            c.start()

    @pl.when(b + 1 < pl.num_programs(0))
    def _():
        for c in copies(b + 1, slot=1 - slot):
            c.start()

    for c in copies(b, slot=slot):
        c.wait()

    ts = q_ref.shape[1]
    lam = _lambda(lq_ref[...], lam_init)
    qpos = p0 + lax.broadcasted_iota(jnp.int32, (2 * ts, p0), 0) % ts
    dist_p = (qpos - lax.broadcasted_iota(jnp.int32, (2 * ts, p0), 1)).astype(F32)
    row = p0 + lax.broadcasted_iota(jnp.int32, (2 * ts, ts), 0) % ts
    col = p0 + lax.broadcasted_iota(jnp.int32, (2 * ts, ts), 1)
    dist_n = jnp.abs(row - col).astype(F32)
    vis_n = (col // CHUNK) <= (row // CHUNK)
    lane = lax.broadcasted_iota(jnp.int32, (ts, HEAD_W), 1)
    for h in range(N_HEADS):
        hs = slice(h * HEAD_W, (h + 1) * HEAD_W)
        qh = q_ref[0, :, hs]
        zero = jnp.zeros_like(qh)
        q2 = jnp.concatenate([jnp.where(lane < DIFF_D, qh, zero), jnp.where(lane >= DIFF_D, qh, zero)], axis=0)
        kp = cache_buf[slot, 0, h].astype(BF16)
        vp = cache_buf[slot, 1, h].astype(BF16)
        kn = _head(nk_ref, h).astype(BF16)
        vn = _head(nv_ref, h).astype(BF16)
        sp = _dot_nt(q2, kp) - _slope2(h) * dist_p
        sn = _dot_nt(q2, kn) + jnp.where(vis_n, -_slope2(h) * dist_n, NEG_BIG)
        m = jnp.maximum(jnp.max(sp, axis=-1, keepdims=True), jnp.max(sn, axis=-1, keepdims=True))
        ep = jnp.exp2(sp - m)
        en = jnp.exp2(sn - m)
        l = jnp.sum(ep, axis=-1, keepdims=True) + jnp.sum(en, axis=-1, keepdims=True)
        o = (_dot(ep.astype(BF16), vp) + _dot(en.astype(BF16), vn)) / l
        out_ref[0, :, hs] = _finish_head(o[:ts], o[ts:], lam, gs_ref[...], lam_init).astype(out_ref.dtype)
    _mem_attend(qm_ref[0], mk_ref, mv_ref, out_ref)


def _diff_sample(z, cache_k, cache_v, k_new, v_new, mem_k, mem_v, lq, g_sub, layer, lam_init):
    B, Ts, _ = z.shape
    P0 = cache_k.shape[1]
    diff_layer = layer - (mem_k.shape[0] - lq.shape[0])
    return pl.pallas_call(
        functools.partial(_diff_sample_kernel, p0=P0, lam_init=lam_init),
        grid=(B,),
        in_specs=[_layer(lq, diff_layer), _layer(_rows(g_sub), diff_layer),
                  pl.BlockSpec((1, Ts, MIX_W), lambda b: (b, 0, 0)),
                  pl.BlockSpec((1, Ts, MIX_W), lambda b: (b, 0, 1)),
                  pl.BlockSpec(memory_space=pl.ANY), pl.BlockSpec(memory_space=pl.ANY),
                  _batch_spec(k_new), _batch_spec(v_new), _batch_spec(mem_k, layer), _batch_spec(mem_v, layer)],
        out_specs=pl.BlockSpec((1, Ts, 2 * MIX_W), lambda b: (b, 0, 0)),
        out_shape=jax.ShapeDtypeStruct((B, Ts, 2 * MIX_W), BF16),
        scratch_shapes=[pltpu.VMEM((2, 2, N_HEADS, P0, HEAD_W), F32), pltpu.SemaphoreType.DMA((2,))],
        compiler_params=_cparams("arbitrary"),
        name="diff_sample",
    )(lq, _rows(g_sub), z, z, cache_k, cache_v, k_new, v_new, mem_k, mem_v)


def _lambda_init(layer_idx):
    return 0.8 - 0.6 * math.exp(-0.3 * layer_idx)


def _trunk(x, pos0, pool_hist, mem_k, mem_v, past_k, past_v, w, tm_rows, tm_seq, tq):
    B, T, D = x.shape
    R = B * T
    xr = x.reshape(R, D)
    hist16 = jnp.pad(pool_hist, ((0, 0), (HALO - POOL_HIST, 0), (0, 0)))

    to_cast = [name for name in ("w_ff1", "w_ff2") if w[name].dtype != BF16]
    cat0, tail, *cast = _mix0(x, w["g_attn"], w["w_in"], hist16, mem_k, mem_v, w["w_pool"], w["pool_scale"], 0,
                              tm_seq, pos0, cast=[w[name] for name in to_cast])
    w = dict(w, **dict(zip(to_cast, cast)))
    new_pool = tail[:, HALO - POOL_HIST:]
    prompt = past_k is None
    x1, z, *bf, k, v = _outmlp(xr, cat0.reshape(R, D), w, 0, tm_rows, final_norm=False, seq_len=T,
                               attn_operands=prompt)
    z3 = z.reshape(B, T, D)
    lam_init = _lambda_init(1)
    if prompt:
        k1a, k2a, vb = (a.reshape(B, T, MIX_W) for a in bf)
        cat1 = _diff_prompt(z3, k1a, k2a, vb, mem_k, mem_v, w["lambda_qk"], w["g_subln"], 1, lam_init, tq)
    else:
        shp = (B, T, N_HEADS, HEAD_W)
        cat1 = _diff_sample(z3, past_k, past_v, k.reshape(shp), v.reshape(shp), mem_k, mem_v,
                            w["lambda_qk"], w["g_subln"], 1, lam_init)
    y, = _outmlp(x1, cat1.reshape(R, D), w, 1, tm_rows, final_norm=True)
    shp = (B, T, N_HEADS, HEAD_W)
    return y.reshape(B, T, D), new_pool[None], k.reshape(shp), v.reshape(shp), w


def kernel(x_prompt, x_sample, mem_prompt, cache_k, cache_v, cache_mem_k, cache_mem_v, state_pool, g_attn, w_in, w_out, g_mem, w_mem_kv, g_ffn, w_ff1, w_ff2, w_pool, pool_scale, lambda_qk, g_subln, g_kv, w_kv, g_final):
    B, T, D = x_prompt.shape
    Bs, Ts, _ = x_sample.shape
    P0 = cache_k.shape[1]
    w = dict(g_attn=g_attn, w_in=w_in.astype(BF16), w_out=w_out.astype(BF16), g_ffn=g_ffn,
             w_ff1=w_ff1, w_ff2=w_ff2, w_pool=w_pool.astype(BF16),
             pool_scale=pool_scale, lambda_qk=lambda_qk, g_subln=g_subln, g_kv=g_kv,
             w_kv=w_kv.astype(BF16), g_final=g_final)

    mem_k_p, mem_v_p, mem_k_d, mem_v_d = _mem_kv(mem_prompt, g_mem, w_mem_kv.astype(BF16))
    hist0 = jnp.zeros((B, POOL_HIST, MIX_W), F32)
    y_p, pool_p, k_p, v_p, w = _trunk(x_prompt, 0, hist0, mem_k_d, mem_v_d, None, None, w,
                                      tm_rows=512, tm_seq=2048, tq=512)
    y_s, pool_s, k_s, v_s, _ = _trunk(x_sample, P0, state_pool[0], cache_mem_k, cache_mem_v, cache_k, cache_v, w,
                                      tm_rows=Bs * Ts, tm_seq=Ts, tq=None)
    return (y_p, y_s, mem_k_p, mem_v_p, pool_p, k_p, v_p, pool_s, k_s, v_s)
```

```python
import functools
import math

import jax
import jax.numpy as jnp
import ml_dtypes
import numpy as np
from jax import lax
from jax.experimental import pallas as pl
from jax.experimental.pallas import tpu as pltpu

F32 = jnp.float32
BF16 = jnp.bfloat16

CHUNK = 64
POOL_WINDOWS = (2, 4, 8, 16)
POOL_HIST = 15
HALO = 16
N_HEADS = 4
HEAD_W = 128
DIFF_D = 64
MIX_W = N_HEADS * HEAD_W
EPS = 1e-6
NEG_BIG = -1e30
LOG2E = math.log2(math.e)
Q_SCALE = DIFF_D ** -0.5 * LOG2E
QM_SCALE = HEAD_W ** -0.5 * LOG2E
N_SPLIT = 4
MIX0_SUB_ROWS = 512
VMEM_LIMIT = 56 * 1024 * 1024


def _cparams(*sem):
    return pltpu.CompilerParams(dimension_semantics=sem, vmem_limit_bytes=VMEM_LIMIT)


def _resident(shape, index_map):
    return pl.BlockSpec(shape, index_map, pipeline_mode=pl.Buffered(1))


def _layer(arr, layer):
    zeros = (0,) * (arr.ndim - 1)
    return pl.BlockSpec((None,) + arr.shape[1:], lambda *_: (layer,) + zeros, pipeline_mode=pl.Buffered(1))


def _rows(g):
    return g.reshape(g.shape[0], 1, g.shape[1])


def _inv_rms(x):
    return lax.rsqrt(jnp.mean(x * x, axis=-1, keepdims=True) + EPS)


def _dot(a, b):
    return jnp.dot(a, b, preferred_element_type=F32)


def _dot_nt(a, b):
    return lax.dot_general(a, b, (((1,), (1,)), ((), ())), preferred_element_type=F32)


def _memkv_kernel(x_ref, g_ref, w_ref, k_ref, v_ref, kd_ref, vd_ref):
    nb, m, d = x_ref.shape
    x = x_ref[...].reshape(nb * m, d)
    h = (x * _inv_rms(x) * g_ref[0]).astype(BF16)
    kv = _dot(h, w_ref[0])
    for b in range(nb):
        rows = kv[b * m:(b + 1) * m]
        kd_ref[0, b] = rows[:, :MIX_W]
        vd_ref[0, b] = rows[:, MIX_W:]
        for hd in range(N_HEADS):
            k_ref[0, b, :, hd, :] = rows[:, hd * HEAD_W:(hd + 1) * HEAD_W]
            v_ref[0, b, :, hd, :] = rows[:, MIX_W + hd * HEAD_W:MIX_W + (hd + 1) * HEAD_W]


def _mem_kv(mem, g_mem, w_mem_kv):
    B, M, D = mem.shape
    L = g_mem.shape[0]
    out = jax.ShapeDtypeStruct((L, B, M, N_HEADS, HEAD_W), F32)
    dense = jax.ShapeDtypeStruct((L, B, M, MIX_W), F32)
    return pl.pallas_call(
        _memkv_kernel,
        grid=(L,),
        in_specs=[_resident((B, M, D), lambda l: (0, 0, 0)),
                  pl.BlockSpec((1, 1, D), lambda l: (l, 0, 0)),
                  pl.BlockSpec((1, D, 2 * MIX_W), lambda l: (l, 0, 0))],
        out_specs=[pl.BlockSpec((1, B, M, N_HEADS, HEAD_W), lambda l: (l, 0, 0, 0, 0))] * 2
                  + [pl.BlockSpec((1, B, M, MIX_W), lambda l: (l, 0, 0, 0))] * 2,
        out_shape=[out, out, dense, dense],
        compiler_params=_cparams("arbitrary"),
        name="mem_kv",
    )(mem, g_mem.reshape(L, 1, D), w_mem_kv)


def _batch_spec(arr, layer=None):
    if layer is None:
        zeros = (0,) * (arr.ndim - 1)
        return pl.BlockSpec((1,) + arr.shape[1:], lambda b, *_: (b,) + zeros)
    zeros = (0,) * (arr.ndim - 2)
    return pl.BlockSpec((None, 1) + arr.shape[2:], lambda b, *_: (layer, b) + zeros)


def _head(ref, h):
    if len(ref.shape) == 3:
        return ref[0, :, h * HEAD_W:(h + 1) * HEAD_W]
    return ref[0, :, h, :]


def _mem_attend(qm, mk_ref, mv_ref, out_ref, rs=slice(None)):
    ones = jnp.ones((mk_ref.shape[1], HEAD_W), BF16)
    for h in range(N_HEADS):
        hs = slice(h * HEAD_W, (h + 1) * HEAD_W)
        kh = _head(mk_ref, h).astype(BF16)
        vh = jnp.concatenate([_head(mv_ref, h).astype(BF16), ones], axis=1)
        s = _dot_nt(qm[:, hs], kh)
        e = jnp.exp2(s - jnp.max(s, axis=-1, keepdims=True))
        pv = _dot(e.astype(BF16), vh)
        o = pv[:, :HEAD_W] / pv[:, HEAD_W:]
        out_ref[0, rs, MIX_W + h * HEAD_W:MIX_W + (h + 1) * HEAD_W] = o.astype(out_ref.dtype)


def _mix0_kernel(x_ref, g_ref, wi_ref, hist_ref, wp_ref, ps_ref, mk_ref, mv_ref, *rest, tm, sub, pos0, n_cast):
    cast_in = rest[:n_cast]
    out_ref, tail_ref = rest[n_cast:n_cast + 2]
    cast_out = rest[n_cast + 2:2 * n_cast + 2]
    carry_ref = rest[2 * n_cast + 2]
    for src, dst in zip(cast_in, cast_out):
        dst[...] = src[...].astype(dst.dtype)
    i = pl.program_id(1)

    @pl.when(i == 0)
    def _():
        carry_ref[...] = hist_ref[0]

    halo = carry_ref[...]
    for r in range(tm // sub):
        rs = slice(r * sub, (r + 1) * sub)
        x = x_ref[0, rs, :]
        z = _dot((x * _inv_rms(x) * g_ref[...]).astype(BF16), wi_ref[...])
        u = z[:, :MIX_W]
        qm = (z[:, MIX_W:] * QM_SCALE).astype(BF16)
        ext = jnp.concatenate([halo, u], axis=0)
        halo = u[sub - HALO:]
        pos = pos0 + i * tm + r * sub + lax.broadcasted_iota(jnp.int32, (sub, HEAD_W), 0)
        for g, w in enumerate(POOL_WINDOWS):
            gs = slice(g * HEAD_W, (g + 1) * HEAD_W)
            acc = ext[:, gs]
            span = 1
            while span < w:
                acc = acc + pltpu.roll(acc, span, 0)
                span *= 2
            cnt = jnp.minimum(pos + 1, w).astype(F32)
            d = acc[HALO:] / cnt - u[:, gs]
            t = _dot(d.astype(BF16), wp_ref[g]) * ps_ref[:, gs]
            out_ref[0, rs, gs] = t.astype(out_ref.dtype)
        _mem_attend(qm, mk_ref, mv_ref, out_ref, rs)
    carry_ref[...] = halo
    tail_ref[0] = halo


def _mix0(x, g_attn, w_in, hist16, mem_k, mem_v, w_pool, pool_scale, layer, tm, pos0, cast=()):
    B, T, D = x.shape
    sub = min(tm, MIX0_SUB_ROWS)
    n_i = T // tm
    assert T % tm == 0 and tm % sub == 0 and sub >= HALO

    def chunk(a):
        assert a.shape[1] % (B * n_i) == 0
        return pl.BlockSpec((a.shape[0], a.shape[1] // (B * n_i), a.shape[2]), lambda b, i: (0, b * n_i + i, 0))

    return pl.pallas_call(
        functools.partial(_mix0_kernel, tm=tm, sub=sub, pos0=pos0, n_cast=len(cast)),
        grid=(B, n_i),
        in_specs=[pl.BlockSpec((1, tm, D), lambda b, i: (b, i, 0)),
                  _layer(_rows(g_attn), layer), _layer(w_in, layer),
                  pl.BlockSpec((1, HALO, MIX_W), lambda b, i: (b, 0, 0)),
                  _layer(w_pool, layer), _layer(_rows(pool_scale), layer),
                  _batch_spec(mem_k, layer), _batch_spec(mem_v, layer)] + [chunk(a) for a in cast],
        out_specs=[pl.BlockSpec((1, tm, 2 * MIX_W), lambda b, i: (b, i, 0)),
                   pl.BlockSpec((1, HALO, MIX_W), lambda b, i: (b, 0, 0))] + [chunk(a) for a in cast],
        out_shape=[jax.ShapeDtypeStruct((B, T, 2 * MIX_W), BF16),
                   jax.ShapeDtypeStruct((B, HALO, MIX_W), F32)]
                  + [jax.ShapeDtypeStruct(a.shape, BF16) for a in cast],
        scratch_shapes=[pltpu.VMEM((HALO, MIX_W), F32)],
        compiler_params=_cparams("arbitrary", "arbitrary"),
        name="mix0",
    )(x, _rows(g_attn), w_in, hist16, w_pool, _rows(pool_scale), mem_k, mem_v, *cast)


def _lane_window(lane, lo):
    return (lane >= lo) & (lane < lo + N_SPLIT)


def _head_copies(buf, slot, hbm_refs, row0, tm, sem):
    out = []
    for t, hbm in enumerate(hbm_refs):
        for h in range(N_HEADS):
            src = buf.at[slot, :, pl.ds(t * MIX_W + h * HEAD_W, HEAD_W)]
            out.append(pltpu.make_async_copy(src, hbm.at[pl.ds(row0, tm), h, :], sem.at[slot]))
    return out


def _project_layer1(x, ga_ref, wi_ref, gk_ref, wk_ref, z_ref, bf_refs, k_hbm, v_hbm, kv_buf, sem, *, seq_len):
    tm = x.shape[0]
    i = pl.program_id(0)
    n_steps = pl.num_programs(0)
    slot = i % 2
    copies = functools.partial(_head_copies, kv_buf, hbm_refs=(k_hbm, v_hbm), tm=tm, sem=sem)

    xn = x * _inv_rms(x)
    z = _dot((xn * ga_ref[...]).astype(BF16), wi_ref[...])
    z_ref[:, :MIX_W] = (z[:, :MIX_W] * Q_SCALE).astype(BF16)
    z_ref[:, MIX_W:] = (z[:, MIX_W:] * QM_SCALE).astype(BF16)
    kv = _dot((xn * gk_ref[...]).astype(BF16), wk_ref[...])

    @pl.when(i >= 2)
    def _():
        for c in copies(slot, row0=(i - 2) * tm):
            c.wait()

    kv_buf[slot] = kv
    for c in copies(slot, row0=i * tm):
        c.start()

    @pl.when(i == n_steps - 1)
    def _():
        @pl.when(i >= 1)
        def _():
            for c in copies(1 - slot, row0=(i - 1) * tm):
                c.wait()
        for c in copies(slot, row0=i * tm):
            c.wait()

    if bf_refs:
        k = kv[:, :MIX_W]
        v = kv[:, MIX_W:]
        k1a_ref, k2a_ref, vb_ref = bf_refs
        lane = lax.broadcasted_iota(jnp.int32, (tm, HEAD_W), 1)
        pos = (i * tm) % seq_len + lax.broadcasted_iota(jnp.int32, (tm, HEAD_W), 0)
        off = pos % CHUNK
        base = (pos - off).astype(F32)
        off = off.astype(F32)

        def pos_lanes(lo):
            return jnp.where(_lane_window(lane, lo), base, jnp.where(_lane_window(lane, lo + N_SPLIT), off, 0.0))

        pos_hi = pos_lanes(DIFF_D)
        pos_lo = pos_lanes(0)
        for h in range(N_HEADS):
            hs = slice(h * HEAD_W, (h + 1) * HEAD_W)
            k1a_ref[:, hs] = jnp.where(lane < DIFF_D, k[:, hs], pos_hi).astype(BF16)
            k2a_ref[:, hs] = jnp.where(lane >= DIFF_D, k[:, hs], pos_lo).astype(BF16)
        vb_ref[...] = v.astype(BF16)


def _outmlp_kernel(x_ref, cat_ref, wo_ref, g_ref, w1_ref, w2_ref, gf_ref, *rest, final_norm, ff_chunk, n_bf, seq_len):
    if final_norm:
        out_ref, a_ref = rest
    else:
        ga_ref, wi_ref, gk_ref, wk_ref, out_ref, z_ref = rest[:6]
        bf_refs = rest[6:6 + n_bf]
        k_hbm, v_hbm, a_ref, kv_buf, sem = rest[6 + n_bf:]
    x1 = x_ref[...] + _dot(cat_ref[...], wo_ref[...])
    h = (x1 * _inv_rms(x1) * g_ref[...]).astype(BF16)
    for c in range(w1_ref.shape[1] // ff_chunk):
        cs = slice(c * ff_chunk, (c + 1) * ff_chunk)
        a = jnp.maximum(_dot(h, w1_ref[:, cs]), 0.0)
        a_ref[:, cs] = (a * a).astype(BF16)
    x2 = x1 + _dot(a_ref[...], w2_ref[...])
    if final_norm:
        out_ref[...] = x2 * _inv_rms(x2) * gf_ref[...]
    else:
        out_ref[...] = x2
        _project_layer1(x2, ga_ref, wi_ref, gk_ref, wk_ref, z_ref, bf_refs, k_hbm, v_hbm, kv_buf, sem,
                        seq_len=seq_len)


def _outmlp(x, cat, w, layer, tm, final_norm, seq_len=None, attn_operands=False):
    R, D = x.shape
    FF = w["w_ff1"].shape[-1]
    row = lambda i: (i, 0)
    fix = lambda i: (0, 0)
    n_bf = 3 if attn_operands else 0
    in_specs = [pl.BlockSpec((tm, D), row), pl.BlockSpec((tm, D), row),
                _layer(w["w_out"], layer), _layer(_rows(w["g_ffn"]), layer),
                _layer(w["w_ff1"], layer), _layer(w["w_ff2"], layer), _resident((1, D), fix)]
    args = [x, cat, w["w_out"], _rows(w["g_ffn"]), w["w_ff1"], w["w_ff2"], w["g_final"].reshape(1, D)]
    out_specs = [pl.BlockSpec((tm, D), row)]
    out_shape = [jax.ShapeDtypeStruct((R, D), F32)]
    scratch = [pltpu.VMEM((tm, FF), BF16)]
    if not final_norm:
        in_specs += [_layer(_rows(w["g_attn"]), layer + 1), _layer(w["w_in"], layer + 1),
                     _resident((1, D), fix), _layer(w["w_kv"], 0)]
        args += [_rows(w["g_attn"]), w["w_in"], w["g_kv"].reshape(1, D), w["w_kv"]]
        out_specs += ([pl.BlockSpec((tm, 2 * MIX_W), row)] + [pl.BlockSpec((tm, MIX_W), row)] * n_bf
                      + [pl.BlockSpec(memory_space=pl.ANY)] * 2)
        out_shape += ([jax.ShapeDtypeStruct((R, 2 * MIX_W), BF16)] + [jax.ShapeDtypeStruct((R, MIX_W), BF16)] * n_bf
                      + [jax.ShapeDtypeStruct((R, N_HEADS, HEAD_W), F32)] * 2)
        scratch += [pltpu.VMEM((2, tm, 2 * MIX_W), F32), pltpu.SemaphoreType.DMA((2,))]
    return pl.pallas_call(
        functools.partial(_outmlp_kernel, final_norm=final_norm, ff_chunk=1024, n_bf=n_bf, seq_len=seq_len),
        grid=(R // tm,),
        in_specs=in_specs,
        out_specs=out_specs,
        out_shape=out_shape,
        scratch_shapes=scratch,
        compiler_params=_cparams("arbitrary"),
        name="outmlp",
    )(*args)


def _lambda(lq, lam_init):
    a = jnp.sum(lq[0:1] * lq[1:2], axis=-1, keepdims=True)
    b = jnp.sum(lq[2:3] * lq[3:4], axis=-1, keepdims=True)
    return jnp.exp(a) - jnp.exp(b) + lam_init


def _slope2(h):
    return 2.0 ** (-8.0 * (h + 1) / N_HEADS) * LOG2E


def _bf16_pieces(c):
    out = []
    for _ in range(N_SPLIT):
        p = float(np.asarray(c, dtype=ml_dtypes.bfloat16))
        out.append(p)
        c -= p
    return out


def _slope_lanes(h, lo, shape):
    lane = lax.broadcasted_iota(jnp.int32, shape, 1)
    out = jnp.zeros(shape, F32)
    for p, piece in enumerate(_bf16_pieces(_slope2(h))):
        out = jnp.where((lane == lo + p) | (lane == lo + N_SPLIT + p), piece, out)
    return out.astype(BF16)


def _finish_head(o1, o2, lam, gs, lam_init):
    o = o1 - lam * o2
    return o * _inv_rms(o) * gs * (1.0 - lam_init)


def _diff_kernel(lq_ref, gs_ref, q_ref, qn_ref, qm_ref, k10_ref, k20_ref, v0_ref, k1n_ref, k2n_ref, vn_ref,
                 mk_ref, mv_ref, out_ref, kv_ref, qa_ref, acc_ref, m_ref, s_ref, *, tq, lam_init):
    kv0_refs = (k10_ref, k20_ref, v0_ref)
    kvn_refs = (k1n_ref, k2n_ref, vn_ref)
    i = pl.program_id(1)
    n_tiles = pl.num_programs(1)
    nblk = tq // HEAD_W
    n_chain = 2 * N_HEADS
    slot = i % 2
    row = lax.broadcasted_iota(jnp.int32, (tq, tq), 0)
    col = lax.broadcasted_iota(jnp.int32, (tq, tq), 1)
    ahead = jnp.where((col // CHUNK) <= (row // CHUNK), jnp.maximum(col - row, 0).astype(F32), -NEG_BIG)
    ones = jnp.ones((tq, HEAD_W), BF16)
    lane = lax.broadcasted_iota(jnp.int32, (tq, HEAD_W), 1)
    q0 = pl.multiple_of(i * tq, tq)
    qn0 = pl.multiple_of(jnp.minimum(i + 1, n_tiles - 1) * tq, tq)

    def hslice(n):
        return slice((n // 2) * HEAD_W, (n // 2 + 1) * HEAD_W)

    def scores(n, k0, q_slot=slot):
        return _dot_nt(qa_ref[q_slot * n_chain + n], kv_ref[n % 2, pl.ds(k0, tq), hslice(n)])

    def v_aug(n, k0):
        return jnp.concatenate([kv_ref[2, pl.ds(k0, tq), hslice(n)], ones], axis=1)

    def diag_fix(h):
        return (-2.0 * _slope2(h)) * ahead

    def build_queries(src_ref, q_slot):
        for h in range(N_HEADS):
            qh = src_ref[0, :, h * HEAD_W:(h + 1) * HEAD_W]
            base = q_slot * n_chain + 2 * h
            qa_ref[base] = jnp.where(lane < DIFF_D, qh, _slope_lanes(h, DIFF_D, qh.shape))
            qa_ref[base + 1] = jnp.where(lane >= DIFF_D, qh, _slope_lanes(h, 0, qh.shape))

    def consume(n, k0, first):
        vt = v_aug(n, k0)
        blocks = [s_ref[n, :, b * HEAD_W:(b + 1) * HEAD_W] for b in range(nblk)]
        m_row = jnp.max(functools.reduce(jnp.maximum, blocks), axis=-1, keepdims=True)
        if first:
            m_new = jnp.broadcast_to(m_row, (tq, HEAD_W))
        else:
            m_old = m_ref[n]
            m_new = jnp.maximum(m_old, m_row)
        p = jnp.concatenate([jnp.exp2(b - m_new) for b in blocks], axis=1).astype(BF16)
        pv = _dot(p, vt)
        if first:
            acc_ref[n] = pv
        else:
            alpha = jnp.exp2(m_old - m_new)
            acc_ref[n] = jnp.concatenate([alpha, alpha], axis=1) * acc_ref[n] + pv
        m_ref[n] = m_new

    def stage_next_diag(n):
        s_ref[n] = scores(n, qn0, 1 - slot) + diag_fix(n // 2)

    @pl.when(i == 0)
    def _():
        for c in range(3):
            kv_ref[c, pl.ds(0, tq), :] = kv0_refs[c][0]
        build_queries(q_ref, slot)
        for n in range(n_chain):
            s_ref[n] = scores(n, q0) + diag_fix(n // 2)

    for c in range(3):
        kv_ref[c, pl.ds(qn0, tq), :] = kvn_refs[c][0]
    build_queries(qn_ref, 1 - slot)
    _mem_attend(qm_ref[0], mk_ref, mv_ref, out_ref)

    @pl.when(i == 0)
    def _():
        for n in range(n_chain):
            consume(n, q0, True)
            stage_next_diag(n)

    @pl.when(i > 0)
    def _():
        for n in range(n_chain):
            consume(n, q0, True)
            s_ref[n] = scores(n, 0)

        def body(j, carry):
            k_cur = pl.multiple_of((j - 1) * tq, tq)
            k_next = pl.multiple_of(j * tq, tq)
            for n in range(n_chain):
                consume(n, k_cur, False)
                s_ref[n] = scores(n, k_next)
            return carry

        lax.fori_loop(1, i, body, 0)
        k_last = pl.multiple_of((i - 1) * tq, tq)
        for n in range(n_chain):
            consume(n, k_last, False)
            stage_next_diag(n)

    lam = _lambda(lq_ref[...], lam_init)
    for h in range(N_HEADS):
        o = [acc_ref[n, :, :HEAD_W] / acc_ref[n, :, HEAD_W:] for n in (2 * h, 2 * h + 1)]
        out_ref[0, :, h * HEAD_W:(h + 1) * HEAD_W] = _finish_head(
            o[0], o[1], lam, gs_ref[...], lam_init).astype(out_ref.dtype)


def _diff_prompt(z, k1a, k2a, vb, mem_k, mem_v, lq, g_sub, layer, lam_init, tq):
    B, T, _ = z.shape
    n_chain = 2 * N_HEADS
    n_tiles = T // tq
    diff_layer = layer - (mem_k.shape[0] - lq.shape[0])
    nxt = lambda i: jnp.minimum(i + 1, n_tiles - 1)
    return pl.pallas_call(
        functools.partial(_diff_kernel, tq=tq, lam_init=lam_init),
        grid=(B, n_tiles),
        in_specs=[_layer(lq, diff_layer), _layer(_rows(g_sub), diff_layer),
                  pl.BlockSpec((1, tq, MIX_W), lambda b, i: (b, i, 0)),
                  pl.BlockSpec((1, tq, MIX_W), lambda b, i: (b, nxt(i), 0)),
                  pl.BlockSpec((1, tq, MIX_W), lambda b, i: (b, i, 1))]
                 + [_resident((1, tq, MIX_W), lambda b, i: (b, 0, 0))] * 3
                 + [pl.BlockSpec((1, tq, MIX_W), lambda b, i: (b, nxt(i), 0))] * 3
                 + [_batch_spec(mem_k, layer), _batch_spec(mem_v, layer)],
        out_specs=pl.BlockSpec((1, tq, 2 * MIX_W), lambda b, i: (b, i, 0)),
        out_shape=jax.ShapeDtypeStruct((B, T, 2 * MIX_W), BF16),
        scratch_shapes=[pltpu.VMEM((3, T, MIX_W), BF16),
                        pltpu.VMEM((2 * n_chain, tq, HEAD_W), BF16),
                        pltpu.VMEM((n_chain, tq, 2 * HEAD_W), F32),
                        pltpu.VMEM((n_chain, tq, HEAD_W), F32),
                        pltpu.VMEM((n_chain, tq, tq), F32)],
        compiler_params=_cparams("arbitrary", "arbitrary"),
        name="diff_prompt",
    )(lq, _rows(g_sub), z, z, z, k1a, k2a, vb, k1a, k2a, vb, mem_k, mem_v)


def _cache_copies(hbm_refs, b, buf, slot, sem):
    return [pltpu.make_async_copy(hbm.at[b, :, h, :], buf.at[slot, t, h], sem.at[slot])
            for t, hbm in enumerate(hbm_refs) for h in range(N_HEADS)]


def _diff_sample_kernel(lq_ref, gs_ref, q_ref, qm_ref, ck_hbm, cv_hbm, nk_ref, nv_ref, mk_ref, mv_ref, out_ref,
                        cache_buf, sem, *, p0, lam_init):
    b = pl.program_id(0)
    slot = b % 2
    copies = functools.partial(_cache_copies, (ck_hbm, cv_hbm), buf=cache_buf, sem=sem)

    @pl.when(b == 0)
    def _():
        for c in copies(b, slot=slot):
            c.start()

    @pl.when(b + 1 < pl.num_programs(0))
    def _():
        for c in copies(b + 1, slot=1 - slot):
            c.start()

    for c in copies(b, slot=slot):
        c.wait()

    ts = q_ref.shape[1]
    lam = _lambda(lq_ref[...], lam_init)
    qpos = p0 + lax.broadcasted_iota(jnp.int32, (2 * ts, p0), 0) % ts
    dist_p = (qpos - lax.broadcasted_iota(jnp.int32, (2 * ts, p0), 1)).astype(F32)
    row = p0 + lax.broadcasted_iota(jnp.int32, (2 * ts, ts), 0) % ts
    col = p0 + lax.broadcasted_iota(jnp.int32, (2 * ts, ts), 1)
    dist_n = jnp.abs(row - col).astype(F32)
    vis_n = (col // CHUNK) <= (row // CHUNK)
    lane = lax.broadcasted_iota(jnp.int32, (ts, HEAD_W), 1)
    for h in range(N_HEADS):
        hs = slice(h * HEAD_W, (h + 1) * HEAD_W)
        qh = q_ref[0, :, hs]
        zero = jnp.zeros_like(qh)
        q2 = jnp.concatenate([jnp.where(lane < DIFF_D, qh, zero), jnp.where(lane >= DIFF_D, qh, zero)], axis=0)
        kp = cache_buf[slot, 0, h].astype(BF16)
        vp = cache_buf[slot, 1, h].astype(BF16)
        kn = _head(nk_ref, h).astype(BF16)
        vn = _head(nv_ref, h).astype(BF16)
        sp = _dot_nt(q2, kp) - _slope2(h) * dist_p
        sn = _dot_nt(q2, kn) + jnp.where(vis_n, -_slope2(h) * dist_n, NEG_BIG)
        m = jnp.maximum(jnp.max(sp, axis=-1, keepdims=True), jnp.max(sn, axis=-1, keepdims=True))
        ep = jnp.exp2(sp - m)
        en = jnp.exp2(sn - m)
        l = jnp.sum(ep, axis=-1, keepdims=True) + jnp.sum(en, axis=-1, keepdims=True)
        o = (_dot(ep.astype(BF16), vp) + _dot(en.astype(BF16), vn)) / l
        out_ref[0, :, hs] = _finish_head(o[:ts], o[ts:], lam, gs_ref[...], lam_init).astype(out_ref.dtype)
    _mem_attend(qm_ref[0], mk_ref, mv_ref, out_ref)


def _diff_sample(z, cache_k, cache_v, k_new, v_new, mem_k, mem_v, lq, g_sub, layer, lam_init):
    B, Ts, _ = z.shape
    P0 = cache_k.shape[1]
    diff_layer = layer - (mem_k.shape[0] - lq.shape[0])
    return pl.pallas_call(
        functools.partial(_diff_sample_kernel, p0=P0, lam_init=lam_init),
        grid=(B,),
        in_specs=[_layer(lq, diff_layer), _layer(_rows(g_sub), diff_layer),
                  pl.BlockSpec((1, Ts, MIX_W), lambda b: (b, 0, 0)),
                  pl.BlockSpec((1, Ts, MIX_W), lambda b: (b, 0, 1)),
                  pl.BlockSpec(memory_space=pl.ANY), pl.BlockSpec(memory_space=pl.ANY),
                  _batch_spec(k_new), _batch_spec(v_new), _batch_spec(mem_k, layer), _batch_spec(mem_v, layer)],
        out_specs=pl.BlockSpec((1, Ts, 2 * MIX_W), lambda b: (b, 0, 0)),
        out_shape=jax.ShapeDtypeStruct((B, Ts, 2 * MIX_W), BF16),
        scratch_shapes=[pltpu.VMEM((2, 2, N_HEADS, P0, HEAD_W), F32), pltpu.SemaphoreType.DMA((2,))],
        compiler_params=_cparams("arbitrary"),
        name="diff_sample",
    )(lq, _rows(g_sub), z, z, cache_k, cache_v, k_new, v_new, mem_k, mem_v)


def _lambda_init(layer_idx):
    return 0.8 - 0.6 * math.exp(-0.3 * layer_idx)


def _trunk(x, pos0, pool_hist, mem_k, mem_v, past_k, past_v, w, tm_rows, tm_seq, tq):
    B, T, D = x.shape
    R = B * T
    xr = x.reshape(R, D)
    hist16 = jnp.pad(pool_hist, ((0, 0), (HALO - POOL_HIST, 0), (0, 0)))

    to_cast = [name for name in ("w_ff1", "w_ff2") if w[name].dtype != BF16]
    cat0, tail, *cast = _mix0(x, w["g_attn"], w["w_in"], hist16, mem_k, mem_v, w["w_pool"], w["pool_scale"], 0,
                              tm_seq, pos0, cast=[w[name] for name in to_cast])
    w = dict(w, **dict(zip(to_cast, cast)))
    new_pool = tail[:, HALO - POOL_HIST:]
    prompt = past_k is None
    x1, z, *bf, k, v = _outmlp(xr, cat0.reshape(R, D), w, 0, tm_rows, final_norm=False, seq_len=T,
                               attn_operands=prompt)
    z3 = z.reshape(B, T, D)
    lam_init = _lambda_init(1)
    if prompt:
        k1a, k2a, vb = (a.reshape(B, T, MIX_W) for a in bf)
        cat1 = _diff_prompt(z3, k1a, k2a, vb, mem_k, mem_v, w["lambda_qk"], w["g_subln"], 1, lam_init, tq)
    else:
        shp = (B, T, N_HEADS, HEAD_W)
        cat1 = _diff_sample(z3, past_k, past_v, k.reshape(shp), v.reshape(shp), mem_k, mem_v,
                            w["lambda_qk"], w["g_subln"], 1, lam_init)
    y, = _outmlp(x1, cat1.reshape(R, D), w, 1, min(R, 2 * tm_rows), final_norm=True)
    shp = (B, T, N_HEADS, HEAD_W)
    return y.reshape(B, T, D), new_pool[None], k.reshape(shp), v.reshape(shp), w


def kernel(x_prompt, x_sample, mem_prompt, cache_k, cache_v, cache_mem_k, cache_mem_v, state_pool, g_attn, w_in, w_out, g_mem, w_mem_kv, g_ffn, w_ff1, w_ff2, w_pool, pool_scale, lambda_qk, g_subln, g_kv, w_kv, g_final):
    B, T, D = x_prompt.shape
    Bs, Ts, _ = x_sample.shape
    P0 = cache_k.shape[1]
    w = dict(g_attn=g_attn, w_in=w_in.astype(BF16), w_out=w_out.astype(BF16), g_ffn=g_ffn,
             w_ff1=w_ff1, w_ff2=w_ff2, w_pool=w_pool.astype(BF16),
             pool_scale=pool_scale, lambda_qk=lambda_qk, g_subln=g_subln, g_kv=g_kv,
             w_kv=w_kv.astype(BF16)[None], g_final=g_final)

    mem_k_p, mem_v_p, mem_k_d, mem_v_d = _mem_kv(mem_prompt, g_mem, w_mem_kv.astype(BF16))
    hist0 = jnp.zeros((B, POOL_HIST, MIX_W), F32)
    y_p, pool_p, k_p, v_p, w = _trunk(x_prompt, 0, hist0, mem_k_d, mem_v_d, None, None, w,
                                      tm_rows=512, tm_seq=2048, tq=512)
    y_s, pool_s, k_s, v_s, _ = _trunk(x_sample, P0, state_pool[0], cache_mem_k, cache_mem_v, cache_k, cache_v, w,
                                      tm_rows=Bs * Ts, tm_seq=Ts, tq=None)
    return (y_p, y_s, mem_k_p, mem_v_p, pool_p, k_p, v_p, pool_s, k_s, v_s)
```

```python
import functools
import math

import jax
import jax.numpy as jnp
import ml_dtypes
import numpy as np
from jax import lax
from jax.experimental import pallas as pl
from jax.experimental.pallas import tpu as pltpu

F32 = jnp.float32
BF16 = jnp.bfloat16

CHUNK = 64
POOL_WINDOWS = (2, 4, 8, 16)
POOL_HIST = 15
HALO = 16
N_HEADS = 4
HEAD_W = 128
DIFF_D = 64
MIX_W = N_HEADS * HEAD_W
EPS = 1e-6
NEG_BIG = -1e30
LOG2E = math.log2(math.e)
Q_SCALE = DIFF_D ** -0.5 * LOG2E
QM_SCALE = HEAD_W ** -0.5 * LOG2E
N_SPLIT = 4
MIX0_SUB_ROWS = 512
VMEM_LIMIT = 56 * 1024 * 1024


def _cparams(*sem):
    return pltpu.CompilerParams(dimension_semantics=sem, vmem_limit_bytes=VMEM_LIMIT)


def _resident(shape, index_map):
    return pl.BlockSpec(shape, index_map, pipeline_mode=pl.Buffered(1))


def _layer(arr, layer):
    zeros = (0,) * (arr.ndim - 1)
    return pl.BlockSpec((None,) + arr.shape[1:], lambda *_: (layer,) + zeros, pipeline_mode=pl.Buffered(1))


def _rows(g):
    return g.reshape(g.shape[0], 1, g.shape[1])


def _inv_rms(x):
    return lax.rsqrt(jnp.mean(x * x, axis=-1, keepdims=True) + EPS)


def _dot(a, b):
    return jnp.dot(a, b, preferred_element_type=F32)


def _dot_nt(a, b):
    return lax.dot_general(a, b, (((1,), (1,)), ((), ())), preferred_element_type=F32)


def _memkv_kernel(x_ref, g_ref, w_ref, k_ref, v_ref, kd_ref, vd_ref):
    nb, m, d = x_ref.shape
    x = x_ref[...].reshape(nb * m, d)
    h = (x * _inv_rms(x) * g_ref[0]).astype(BF16)
    kv = _dot(h, w_ref[0])
    for b in range(nb):
        rows = kv[b * m:(b + 1) * m]
        kd_ref[0, b] = rows[:, :MIX_W]
        vd_ref[0, b] = rows[:, MIX_W:]
        for hd in range(N_HEADS):
            k_ref[0, b, :, hd, :] = rows[:, hd * HEAD_W:(hd + 1) * HEAD_W]
            v_ref[0, b, :, hd, :] = rows[:, MIX_W + hd * HEAD_W:MIX_W + (hd + 1) * HEAD_W]


def _mem_kv(mem, g_mem, w_mem_kv):
    B, M, D = mem.shape
    L = g_mem.shape[0]
    out = jax.ShapeDtypeStruct((L, B, M, N_HEADS, HEAD_W), F32)
    dense = jax.ShapeDtypeStruct((L, B, M, MIX_W), F32)
    return pl.pallas_call(
        _memkv_kernel,
        grid=(L,),
        in_specs=[_resident((B, M, D), lambda l: (0, 0, 0)),
                  pl.BlockSpec((1, 1, D), lambda l: (l, 0, 0)),
                  pl.BlockSpec((1, D, 2 * MIX_W), lambda l: (l, 0, 0))],
        out_specs=[pl.BlockSpec((1, B, M, N_HEADS, HEAD_W), lambda l: (l, 0, 0, 0, 0))] * 2
                  + [pl.BlockSpec((1, B, M, MIX_W), lambda l: (l, 0, 0, 0))] * 2,
        out_shape=[out, out, dense, dense],
        compiler_params=_cparams("arbitrary"),
        name="mem_kv",
    )(mem, g_mem.reshape(L, 1, D), w_mem_kv)


def _batch_spec(arr, layer=None):
    if layer is None:
        zeros = (0,) * (arr.ndim - 1)
        return pl.BlockSpec((1,) + arr.shape[1:], lambda b, *_: (b,) + zeros)
    zeros = (0,) * (arr.ndim - 2)
    return pl.BlockSpec((None, 1) + arr.shape[2:], lambda b, *_: (layer, b) + zeros)


def _head(ref, h):
    if len(ref.shape) == 3:
        return ref[0, :, h * HEAD_W:(h + 1) * HEAD_W]
    return ref[0, :, h, :]


def _mem_attend(qm, mk_ref, mv_ref, out_ref, rs=slice(None)):
    ones = jnp.ones((mk_ref.shape[1], HEAD_W), BF16)
    for h in range(N_HEADS):
        hs = slice(h * HEAD_W, (h + 1) * HEAD_W)
        kh = _head(mk_ref, h).astype(BF16)
        vh = jnp.concatenate([_head(mv_ref, h).astype(BF16), ones], axis=1)
        s = _dot_nt(qm[:, hs], kh)
        e = jnp.exp2(s - jnp.max(s, axis=-1, keepdims=True))
        pv = _dot(e.astype(BF16), vh)
        o = pv[:, :HEAD_W] / pv[:, HEAD_W:]
        out_ref[0, rs, MIX_W + h * HEAD_W:MIX_W + (h + 1) * HEAD_W] = o.astype(out_ref.dtype)


def _mix0_kernel(x_ref, g_ref, wi_ref, hist_ref, wp_ref, ps_ref, mk_ref, mv_ref, *rest, tm, sub, pos0, n_cast):
    cast_in = rest[:n_cast]
    out_ref, tail_ref = rest[n_cast:n_cast + 2]
    cast_out = rest[n_cast + 2:2 * n_cast + 2]
    carry_ref = rest[2 * n_cast + 2]
    for src, dst in zip(cast_in, cast_out):
        dst[...] = src[...].astype(dst.dtype)
    i = pl.program_id(1)

    @pl.when(i == 0)
    def _():
        carry_ref[...] = hist_ref[0]

    halo = carry_ref[...]
    for r in range(tm // sub):
        rs = slice(r * sub, (r + 1) * sub)
        x = x_ref[0, rs, :]
        z = _dot((x * _inv_rms(x) * g_ref[...]).astype(BF16), wi_ref[...])
        u = z[:, :MIX_W]
        qm = (z[:, MIX_W:] * QM_SCALE).astype(BF16)
        ext = jnp.concatenate([halo, u], axis=0)
        halo = u[sub - HALO:]
        pos = pos0 + i * tm + r * sub + lax.broadcasted_iota(jnp.int32, (sub, HEAD_W), 0)
        for g, w in enumerate(POOL_WINDOWS):
            gs = slice(g * HEAD_W, (g + 1) * HEAD_W)
            acc = ext[:, gs]
            span = 1
            while span < w:
                acc = acc + pltpu.roll(acc, span, 0)
                span *= 2
            cnt = jnp.minimum(pos + 1, w).astype(F32)
            d = acc[HALO:] / cnt - u[:, gs]
            t = _dot(d.astype(BF16), wp_ref[g]) * ps_ref[:, gs]
            out_ref[0, rs, gs] = t.astype(out_ref.dtype)
        _mem_attend(qm, mk_ref, mv_ref, out_ref, rs)
    carry_ref[...] = halo
    tail_ref[0] = halo


def _mix0(x, g_attn, w_in, hist16, mem_k, mem_v, w_pool, pool_scale, layer, tm, pos0, cast=()):
    B, T, D = x.shape
    sub = min(tm, MIX0_SUB_ROWS)
    n_i = T // tm
    assert T % tm == 0 and tm % sub == 0 and sub >= HALO

    def chunk(a):
        assert a.shape[1] % (B * n_i) == 0
        return pl.BlockSpec((a.shape[0], a.shape[1] // (B * n_i), a.shape[2]), lambda b, i: (0, b * n_i + i, 0))

    return pl.pallas_call(
        functools.partial(_mix0_kernel, tm=tm, sub=sub, pos0=pos0, n_cast=len(cast)),
        grid=(B, n_i),
        in_specs=[pl.BlockSpec((1, tm, D), lambda b, i: (b, i, 0)),
                  _layer(_rows(g_attn), layer), _layer(w_in, layer),
                  pl.BlockSpec((1, HALO, MIX_W), lambda b, i: (b, 0, 0)),
                  _layer(w_pool, layer), _layer(_rows(pool_scale), layer),
                  _batch_spec(mem_k, layer), _batch_spec(mem_v, layer)] + [chunk(a) for a in cast],
        out_specs=[pl.BlockSpec((1, tm, 2 * MIX_W), lambda b, i: (b, i, 0)),
                   pl.BlockSpec((1, HALO, MIX_W), lambda b, i: (b, 0, 0))] + [chunk(a) for a in cast],
        out_shape=[jax.ShapeDtypeStruct((B, T, 2 * MIX_W), BF16),
                   jax.ShapeDtypeStruct((B, HALO, MIX_W), F32)]
                  + [jax.ShapeDtypeStruct(a.shape, BF16) for a in cast],
        scratch_shapes=[pltpu.VMEM((HALO, MIX_W), F32)],
        compiler_params=_cparams("arbitrary", "arbitrary"),
        name="mix0",
    )(x, _rows(g_attn), w_in, hist16, w_pool, _rows(pool_scale), mem_k, mem_v, *cast)


def _lane_window(lane, lo):
    return (lane >= lo) & (lane < lo + N_SPLIT)


def _head_copies(buf, slot, hbm_refs, row0, tm, sem):
    out = []
    for t, hbm in enumerate(hbm_refs):
        for h in range(N_HEADS):
            src = buf.at[slot, :, pl.ds(t * MIX_W + h * HEAD_W, HEAD_W)]
            out.append(pltpu.make_async_copy(src, hbm.at[pl.ds(row0, tm), h, :], sem.at[slot]))
    return out


def _project_layer1(x, ga_ref, wi_ref, gk_ref, wk_ref, z_ref, bf_refs, k_hbm, v_hbm, kv_buf, sem, *, seq_len):
    tm = x.shape[0]
    i = pl.program_id(0)
    n_steps = pl.num_programs(0)
    slot = i % 2
    copies = functools.partial(_head_copies, kv_buf, hbm_refs=(k_hbm, v_hbm), tm=tm, sem=sem)

    xn = x * _inv_rms(x)
    z = _dot((xn * ga_ref[...]).astype(BF16), wi_ref[...])
    z_ref[:, :MIX_W] = (z[:, :MIX_W] * Q_SCALE).astype(BF16)
    z_ref[:, MIX_W:] = (z[:, MIX_W:] * QM_SCALE).astype(BF16)
    kv = _dot((xn * gk_ref[...]).astype(BF16), wk_ref[...])

    @pl.when(i >= 2)
    def _():
        for c in copies(slot, row0=(i - 2) * tm):
            c.wait()

    kv_buf[slot] = kv
    for c in copies(slot, row0=i * tm):
        c.start()

    @pl.when(i == n_steps - 1)
    def _():
        @pl.when(i >= 1)
        def _():
            for c in copies(1 - slot, row0=(i - 1) * tm):
                c.wait()
        for c in copies(slot, row0=i * tm):
            c.wait()

    if bf_refs:
        k = kv[:, :MIX_W]
        v = kv[:, MIX_W:]
        k1a_ref, k2a_ref, vb_ref = bf_refs
        lane = lax.broadcasted_iota(jnp.int32, (tm, HEAD_W), 1)
        pos = (i * tm) % seq_len + lax.broadcasted_iota(jnp.int32, (tm, HEAD_W), 0)
        off = pos % CHUNK
        base = (pos - off).astype(F32)
        off = off.astype(F32)

        def pos_lanes(lo):
            return jnp.where(_lane_window(lane, lo), base, jnp.where(_lane_window(lane, lo + N_SPLIT), off, 0.0))

        pos_hi = pos_lanes(DIFF_D)
        pos_lo = pos_lanes(0)
        for h in range(N_HEADS):
            hs = slice(h * HEAD_W, (h + 1) * HEAD_W)
            k1a_ref[:, hs] = jnp.where(lane < DIFF_D, k[:, hs], pos_hi).astype(BF16)
            k2a_ref[:, hs] = jnp.where(lane >= DIFF_D, k[:, hs], pos_lo).astype(BF16)
        vb_ref[...] = v.astype(BF16)


def _outmlp_kernel(x_ref, cat_ref, wo_ref, g_ref, w1_ref, w2_ref, gf_ref, *rest, final_norm, ff_chunk, n_bf, seq_len):
    if final_norm:
        out_ref, a_ref = rest
    else:
        ga_ref, wi_ref, gk_ref, wk_ref, out_ref, z_ref = rest[:6]
        bf_refs = rest[6:6 + n_bf]
        k_hbm, v_hbm, a_ref, kv_buf, sem = rest[6 + n_bf:]
    x1 = x_ref[...] + _dot(cat_ref[...], wo_ref[...])
    h = (x1 * _inv_rms(x1) * g_ref[...]).astype(BF16)
    for c in range(w1_ref.shape[1] // ff_chunk):
        cs = slice(c * ff_chunk, (c + 1) * ff_chunk)
        a = jnp.maximum(_dot(h, w1_ref[:, cs]), 0.0)
        a_ref[:, cs] = (a * a).astype(BF16)
    x2 = x1 + _dot(a_ref[...], w2_ref[...])
    if final_norm:
        out_ref[...] = x2 * _inv_rms(x2) * gf_ref[...]
    else:
        out_ref[...] = x2
        _project_layer1(x2, ga_ref, wi_ref, gk_ref, wk_ref, z_ref, bf_refs, k_hbm, v_hbm, kv_buf, sem,
                        seq_len=seq_len)


def _outmlp(x, cat, w, layer, tm, final_norm, seq_len=None, attn_operands=False):
    R, D = x.shape
    FF = w["w_ff1"].shape[-1]
    row = lambda i: (i, 0)
    fix = lambda i: (0, 0)
    n_bf = 3 if attn_operands else 0
    assert not attn_operands or seq_len <= 256 * CHUNK
    in_specs = [pl.BlockSpec((tm, D), row), pl.BlockSpec((tm, D), row),
                _layer(w["w_out"], layer), _layer(_rows(w["g_ffn"]), layer),
                _layer(w["w_ff1"], layer), _layer(w["w_ff2"], layer), _resident((1, D), fix)]
    args = [x, cat, w["w_out"], _rows(w["g_ffn"]), w["w_ff1"], w["w_ff2"], w["g_final"].reshape(1, D)]
    out_specs = [pl.BlockSpec((tm, D), row)]
    out_shape = [jax.ShapeDtypeStruct((R, D), F32)]
    scratch = [pltpu.VMEM((tm, FF), BF16)]
    if not final_norm:
        in_specs += [_layer(_rows(w["g_attn"]), layer + 1), _layer(w["w_in"], layer + 1),
                     _resident((1, D), fix), _layer(w["w_kv"], 0)]
        args += [_rows(w["g_attn"]), w["w_in"], w["g_kv"].reshape(1, D), w["w_kv"]]
        out_specs += ([pl.BlockSpec((tm, 2 * MIX_W), row)] + [pl.BlockSpec((tm, MIX_W), row)] * n_bf
                      + [pl.BlockSpec(memory_space=pl.ANY)] * 2)
        out_shape += ([jax.ShapeDtypeStruct((R, 2 * MIX_W), BF16)] + [jax.ShapeDtypeStruct((R, MIX_W), BF16)] * n_bf
                      + [jax.ShapeDtypeStruct((R, N_HEADS, HEAD_W), F32)] * 2)
        scratch += [pltpu.VMEM((2, tm, 2 * MIX_W), F32), pltpu.SemaphoreType.DMA((2,))]
    return pl.pallas_call(
        functools.partial(_outmlp_kernel, final_norm=final_norm, ff_chunk=1024, n_bf=n_bf, seq_len=seq_len),
        grid=(R // tm,),
        in_specs=in_specs,
        out_specs=out_specs,
        out_shape=out_shape,
        scratch_shapes=scratch,
        compiler_params=_cparams("arbitrary"),
        name="outmlp",
    )(*args)


def _lambda(lq, lam_init):
    a = jnp.sum(lq[0:1] * lq[1:2], axis=-1, keepdims=True)
    b = jnp.sum(lq[2:3] * lq[3:4], axis=-1, keepdims=True)
    return jnp.exp(a) - jnp.exp(b) + lam_init


def _slope2(h):
    return 2.0 ** (-8.0 * (h + 1) / N_HEADS) * LOG2E


def _bf16_pieces(c):
    out = []
    for _ in range(N_SPLIT):
        p = float(np.asarray(c, dtype=ml_dtypes.bfloat16))
        out.append(p)
        c -= p
    return out


def _slope_lanes(h, lo, shape):
    lane = lax.broadcasted_iota(jnp.int32, shape, 1)
    out = jnp.zeros(shape, F32)
    for p, piece in enumerate(_bf16_pieces(_slope2(h))):
        out = jnp.where((lane == lo + p) | (lane == lo + N_SPLIT + p), piece, out)
    return out.astype(BF16)


def _finish_head(o1, o2, lam, gs, lam_init):
    o = o1 - lam * o2
    return o * _inv_rms(o) * gs * (1.0 - lam_init)


def _diff_kernel(lq_ref, gs_ref, q_ref, qn_ref, qm_ref, k10_ref, k20_ref, v0_ref, k1n_ref, k2n_ref, vn_ref,
                 mk_ref, mv_ref, out_ref, kv_ref, qa_ref, acc_ref, m_ref, s_ref, *, tq, lam_init):
    kv0_refs = (k10_ref, k20_ref, v0_ref)
    kvn_refs = (k1n_ref, k2n_ref, vn_ref)
    i = pl.program_id(1)
    n_tiles = pl.num_programs(1)
    nblk = tq // HEAD_W
    n_chain = 2 * N_HEADS
    slot = i % 2
    row = lax.broadcasted_iota(jnp.int32, (tq, tq), 0)
    col = lax.broadcasted_iota(jnp.int32, (tq, tq), 1)
    ahead = jnp.where((col // CHUNK) <= (row // CHUNK), jnp.maximum(col - row, 0).astype(F32), -NEG_BIG)
    ones = jnp.ones((tq, HEAD_W), BF16)
    lane = lax.broadcasted_iota(jnp.int32, (tq, HEAD_W), 1)
    q0 = pl.multiple_of(i * tq, tq)
    qn0 = pl.multiple_of(jnp.minimum(i + 1, n_tiles - 1) * tq, tq)

    def hslice(n):
        return slice((n // 2) * HEAD_W, (n // 2 + 1) * HEAD_W)

    def scores(n, k0, q_slot=slot):
        return _dot_nt(qa_ref[q_slot * n_chain + n], kv_ref[n % 2, pl.ds(k0, tq), hslice(n)])

    def v_aug(n, k0):
        return jnp.concatenate([kv_ref[2, pl.ds(k0, tq), hslice(n)], ones], axis=1)

    def diag_fix(h):
        return (-2.0 * _slope2(h)) * ahead

    def build_queries(src_ref, q_slot):
        for h in range(N_HEADS):
            qh = src_ref[0, :, h * HEAD_W:(h + 1) * HEAD_W]
            base = q_slot * n_chain + 2 * h
            qa_ref[base] = jnp.where(lane < DIFF_D, qh, _slope_lanes(h, DIFF_D, qh.shape))
            qa_ref[base + 1] = jnp.where(lane >= DIFF_D, qh, _slope_lanes(h, 0, qh.shape))

    def consume(n, k0, first):
        vt = v_aug(n, k0)
        blocks = [s_ref[n, :, b * HEAD_W:(b + 1) * HEAD_W] for b in range(nblk)]
        m_row = jnp.max(functools.reduce(jnp.maximum, blocks), axis=-1, keepdims=True)
        if first:
            m_new = jnp.broadcast_to(m_row, (tq, HEAD_W))
        else:
            m_old = m_ref[n]
            m_new = jnp.maximum(m_old, m_row)
        p = jnp.concatenate([jnp.exp2(b - m_new) for b in blocks], axis=1).astype(BF16)
        pv = _dot(p, vt)
        if first:
            acc_ref[n] = pv
        else:
            alpha = jnp.exp2(m_old - m_new)
            acc_ref[n] = jnp.concatenate([alpha, alpha], axis=1) * acc_ref[n] + pv
        m_ref[n] = m_new

    def stage_next_diag(n):
        s_ref[n] = scores(n, qn0, 1 - slot) + diag_fix(n // 2)

    @pl.when(i == 0)
    def _():
        for c in range(3):
            kv_ref[c, pl.ds(0, tq), :] = kv0_refs[c][0]
        build_queries(q_ref, slot)
        for n in range(n_chain):
            s_ref[n] = scores(n, q0) + diag_fix(n // 2)

    for c in range(3):
        kv_ref[c, pl.ds(qn0, tq), :] = kvn_refs[c][0]
    build_queries(qn_ref, 1 - slot)
    _mem_attend(qm_ref[0], mk_ref, mv_ref, out_ref)

    @pl.when(i == 0)
    def _():
        for n in range(n_chain):
            consume(n, q0, True)
            stage_next_diag(n)

    @pl.when(i > 0)
    def _():
        for n in range(n_chain):
            consume(n, q0, True)
            s_ref[n] = scores(n, 0)

        def body(j, carry):
            k_cur = pl.multiple_of((j - 1) * tq, tq)
            k_next = pl.multiple_of(j * tq, tq)
            for n in range(n_chain):
                consume(n, k_cur, False)
                s_ref[n] = scores(n, k_next)
            return carry

        lax.fori_loop(1, i, body, 0)
        k_last = pl.multiple_of((i - 1) * tq, tq)
        for n in range(n_chain):
            consume(n, k_last, False)
            stage_next_diag(n)

    lam = _lambda(lq_ref[...], lam_init)
    for h in range(N_HEADS):
        o = [acc_ref[n, :, :HEAD_W] / acc_ref[n, :, HEAD_W:] for n in (2 * h, 2 * h + 1)]
        out_ref[0, :, h * HEAD_W:(h + 1) * HEAD_W] = _finish_head(
            o[0], o[1], lam, gs_ref[...], lam_init).astype(out_ref.dtype)


def _diff_prompt(z, k1a, k2a, vb, mem_k, mem_v, lq, g_sub, layer, lam_init, tq):
    B, T, _ = z.shape
    n_chain = 2 * N_HEADS
    n_tiles = T // tq
    diff_layer = layer - (mem_k.shape[0] - lq.shape[0])
    nxt = lambda i: jnp.minimum(i + 1, n_tiles - 1)
    return pl.pallas_call(
        functools.partial(_diff_kernel, tq=tq, lam_init=lam_init),
        grid=(B, n_tiles),
        in_specs=[_layer(lq, diff_layer), _layer(_rows(g_sub), diff_layer),
                  pl.BlockSpec((1, tq, MIX_W), lambda b, i: (b, i, 0)),
                  pl.BlockSpec((1, tq, MIX_W), lambda b, i: (b, nxt(i), 0)),
                  pl.BlockSpec((1, tq, MIX_W), lambda b, i: (b, i, 1))]
                 + [_resident((1, tq, MIX_W), lambda b, i: (b, 0, 0))] * 3
                 + [pl.BlockSpec((1, tq, MIX_W), lambda b, i: (b, nxt(i), 0))] * 3
                 + [_batch_spec(mem_k, layer), _batch_spec(mem_v, layer)],
        out_specs=pl.BlockSpec((1, tq, 2 * MIX_W), lambda b, i: (b, i, 0)),
        out_shape=jax.ShapeDtypeStruct((B, T, 2 * MIX_W), BF16),
        scratch_shapes=[pltpu.VMEM((3, T, MIX_W), BF16),
                        pltpu.VMEM((2 * n_chain, tq, HEAD_W), BF16),
                        pltpu.VMEM((n_chain, tq, 2 * HEAD_W), F32),
                        pltpu.VMEM((n_chain, tq, HEAD_W), F32),
                        pltpu.VMEM((n_chain, tq, tq), F32)],
        compiler_params=_cparams("arbitrary", "arbitrary"),
        name="diff_prompt",
    )(lq, _rows(g_sub), z, z, z, k1a, k2a, vb, k1a, k2a, vb, mem_k, mem_v)


def _cache_copies(hbm_refs, b, buf, slot, sem):
    return [pltpu.make_async_copy(hbm.at[b, :, h, :], buf.at[slot, t, h], sem.at[slot])
            for t, hbm in enumerate(hbm_refs) for h in range(N_HEADS)]


def _diff_sample_kernel(lq_ref, gs_ref, q_ref, qm_ref, ck_hbm, cv_hbm, nk_ref, nv_ref, mk_ref, mv_ref, out_ref,
                        cache_buf, sem, *, p0, lam_init):
    b = pl.program_id(0)
    slot = b % 2
    copies = functools.partial(_cache_copies, (ck_hbm, cv_hbm), buf=cache_buf, sem=sem)

    @pl.when(b == 0)
    def _():
        for c in copies(b, slot=slot):
            c.start()

    @pl.when(b + 1 < pl.num_programs(0))
    def _():
        for c in copies(b + 1, slot=1 - slot):
            c.start()

    for c in copies(b, slot=slot):
        c.wait()

    ts = q_ref.shape[1]
    lam = _lambda(lq_ref[...], lam_init)
    qpos = p0 + lax.broadcasted_iota(jnp.int32, (2 * ts, p0), 0) % ts
    dist_p = (qpos - lax.broadcasted_iota(jnp.int32, (2 * ts, p0), 1)).astype(F32)
    row = p0 + lax.broadcasted_iota(jnp.int32, (2 * ts, ts), 0) % ts
    col = p0 + lax.broadcasted_iota(jnp.int32, (2 * ts, ts), 1)
    dist_n = jnp.abs(row - col).astype(F32)
    vis_n = (col // CHUNK) <= (row // CHUNK)
    lane = lax.broadcasted_iota(jnp.int32, (ts, HEAD_W), 1)
    for h in range(N_HEADS):
        hs = slice(h * HEAD_W, (h + 1) * HEAD_W)
        qh = q_ref[0, :, hs]
        zero = jnp.zeros_like(qh)
        q2 = jnp.concatenate([jnp.where(lane < DIFF_D, qh, zero), jnp.where(lane >= DIFF_D, qh, zero)], axis=0)
        kp = cache_buf[slot, 0, h].astype(BF16)
        vp = cache_buf[slot, 1, h].astype(BF16)
        kn = _head(nk_ref, h).astype(BF16)
        vn = _head(nv_ref, h).astype(BF16)
        sp = _dot_nt(q2, kp) - _slope2(h) * dist_p
        sn = _dot_nt(q2, kn) + jnp.where(vis_n, -_slope2(h) * dist_n, NEG_BIG)
        m = jnp.maximum(jnp.max(sp, axis=-1, keepdims=True), jnp.max(sn, axis=-1, keepdims=True))
        ep = jnp.exp2(sp - m)
        en = jnp.exp2(sn - m)
        l = jnp.sum(ep, axis=-1, keepdims=True) + jnp.sum(en, axis=-1, keepdims=True)
        o = (_dot(ep.astype(BF16), vp) + _dot(en.astype(BF16), vn)) / l
        out_ref[0, :, hs] = _finish_head(o[:ts], o[ts:], lam, gs_ref[...], lam_init).astype(out_ref.dtype)
    _mem_attend(qm_ref[0], mk_ref, mv_ref, out_ref)


def _diff_sample(z, cache_k, cache_v, k_new, v_new, mem_k, mem_v, lq, g_sub, layer, lam_init):
    B, Ts, _ = z.shape
    P0 = cache_k.shape[1]
    diff_layer = layer - (mem_k.shape[0] - lq.shape[0])
    return pl.pallas_call(
        functools.partial(_diff_sample_kernel, p0=P0, lam_init=lam_init),
        grid=(B,),
        in_specs=[_layer(lq, diff_layer), _layer(_rows(g_sub), diff_layer),
                  pl.BlockSpec((1, Ts, MIX_W), lambda b: (b, 0, 0)),
                  pl.BlockSpec((1, Ts, MIX_W), lambda b: (b, 0, 1)),
                  pl.BlockSpec(memory_space=pl.ANY), pl.BlockSpec(memory_space=pl.ANY),
                  _batch_spec(k_new), _batch_spec(v_new), _batch_spec(mem_k, layer), _batch_spec(mem_v, layer)],
        out_specs=pl.BlockSpec((1, Ts, 2 * MIX_W), lambda b: (b, 0, 0)),
        out_shape=jax.ShapeDtypeStruct((B, Ts, 2 * MIX_W), BF16),
        scratch_shapes=[pltpu.VMEM((2, 2, N_HEADS, P0, HEAD_W), F32), pltpu.SemaphoreType.DMA((2,))],
        compiler_params=_cparams("arbitrary"),
        name="diff_sample",
    )(lq, _rows(g_sub), z, z, cache_k, cache_v, k_new, v_new, mem_k, mem_v)


def _lambda_init(layer_idx):
    return 0.8 - 0.6 * math.exp(-0.3 * layer_idx)


def _trunk(x, pos0, pool_hist, mem_k, mem_v, past_k, past_v, w, tm_rows, tm_seq, tq):
    B, T, D = x.shape
    R = B * T
    xr = x.reshape(R, D)
    hist16 = jnp.pad(pool_hist, ((0, 0), (HALO - POOL_HIST, 0), (0, 0)))

    to_cast = [name for name in ("w_ff1", "w_ff2") if w[name].dtype != BF16]
    cat0, tail, *cast = _mix0(x, w["g_attn"], w["w_in"], hist16, mem_k, mem_v, w["w_pool"], w["pool_scale"], 0,
                              tm_seq, pos0, cast=[w[name] for name in to_cast])
    w = dict(w, **dict(zip(to_cast, cast)))
    new_pool = tail[:, HALO - POOL_HIST:]
    prompt = past_k is None
    x1, z, *bf, k, v = _outmlp(xr, cat0.reshape(R, D), w, 0, tm_rows, final_norm=False, seq_len=T,
                               attn_operands=prompt)
    z3 = z.reshape(B, T, D)
    lam_init = _lambda_init(1)
    if prompt:
        k1a, k2a, vb = (a.reshape(B, T, MIX_W) for a in bf)
        cat1 = _diff_prompt(z3, k1a, k2a, vb, mem_k, mem_v, w["lambda_qk"], w["g_subln"], 1, lam_init, tq)
    else:
        shp = (B, T, N_HEADS, HEAD_W)
        cat1 = _diff_sample(z3, past_k, past_v, k.reshape(shp), v.reshape(shp), mem_k, mem_v,
                            w["lambda_qk"], w["g_subln"], 1, lam_init)
    y, = _outmlp(x1, cat1.reshape(R, D), w, 1, min(R, 2 * tm_rows), final_norm=True)
    shp = (B, T, N_HEADS, HEAD_W)
    return y.reshape(B, T, D), new_pool[None], k.reshape(shp), v.reshape(shp), w


def kernel(x_prompt, x_sample, mem_prompt, cache_k, cache_v, cache_mem_k, cache_mem_v, state_pool, g_attn, w_in, w_out, g_mem, w_mem_kv, g_ffn, w_ff1, w_ff2, w_pool, pool_scale, lambda_qk, g_subln, g_kv, w_kv, g_final):
    B, T, D = x_prompt.shape
    Bs, Ts, _ = x_sample.shape
    P0 = cache_k.shape[1]
    w = dict(g_attn=g_attn, w_in=w_in.astype(BF16), w_out=w_out.astype(BF16), g_ffn=g_ffn,
             w_ff1=w_ff1, w_ff2=w_ff2, w_pool=w_pool.astype(BF16),
             pool_scale=pool_scale, lambda_qk=lambda_qk, g_subln=g_subln, g_kv=g_kv,
             w_kv=w_kv.astype(BF16)[None], g_final=g_final)

    mem_k_p, mem_v_p, mem_k_d, mem_v_d = _mem_kv(mem_prompt, g_mem, w_mem_kv.astype(BF16))
    hist0 = jnp.zeros((B, POOL_HIST, MIX_W), F32)
    y_p, pool_p, k_p, v_p, w = _trunk(x_prompt, 0, hist0, mem_k_d, mem_v_d, None, None, w,
                                      tm_rows=512, tm_seq=2048, tq=512)
    y_s, pool_s, k_s, v_s, _ = _trunk(x_sample, P0, state_pool[0], cache_mem_k, cache_mem_v, cache_k, cache_v, w,
                                      tm_rows=Bs * Ts, tm_seq=Ts, tq=None)
    return (y_p, y_s, mem_k_p, mem_v_p, pool_p, k_p, v_p, pool_s, k_s, v_s)
```

```python
import functools
import math

import jax
import jax.numpy as jnp
import ml_dtypes
import numpy as np
from jax import lax
from jax.experimental import pallas as pl
from jax.experimental.pallas import tpu as pltpu

F32 = jnp.float32
BF16 = jnp.bfloat16

CHUNK = 64
POOL_WINDOWS = (2, 4, 8, 16)
POOL_HIST = 15
HALO = 16
N_HEADS = 4
HEAD_W = 128
DIFF_D = 64
MIX_W = N_HEADS * HEAD_W
EPS = 1e-6
NEG_BIG = -1e30
LOG2E = math.log2(math.e)
Q_SCALE = DIFF_D ** -0.5 * LOG2E
QM_SCALE = HEAD_W ** -0.5 * LOG2E
N_SPLIT = 4
MIX0_SUB_ROWS = 512
VMEM_LIMIT = 56 * 1024 * 1024


def _cparams(*sem):
    return pltpu.CompilerParams(dimension_semantics=sem, vmem_limit_bytes=VMEM_LIMIT)


def _resident(shape, index_map):
    return pl.BlockSpec(shape, index_map, pipeline_mode=pl.Buffered(1))


def _layer(arr, layer):
    zeros = (0,) * (arr.ndim - 1)
    return pl.BlockSpec((None,) + arr.shape[1:], lambda *_: (layer,) + zeros, pipeline_mode=pl.Buffered(1))


def _rows(g):
    return g.reshape(g.shape[0], 1, g.shape[1])


def _inv_rms(x):
    return lax.rsqrt(jnp.mean(x * x, axis=-1, keepdims=True) + EPS)


def _dot(a, b):
    return jnp.dot(a, b, preferred_element_type=F32)


def _dot_nt(a, b):
    return lax.dot_general(a, b, (((1,), (1,)), ((), ())), preferred_element_type=F32)


def _memkv_kernel(x_ref, g_ref, w_ref, *rest, n_cast):
    cast_in = rest[:n_cast]
    k_ref, v_ref, kd_ref, vd_ref = rest[n_cast:n_cast + 4]
    for src, dst in zip(cast_in, rest[n_cast + 4:]):
        dst[...] = src[...].astype(dst.dtype)
    nb, m, d = x_ref.shape
    x = x_ref[...].reshape(nb * m, d)
    h = (x * _inv_rms(x) * g_ref[0]).astype(BF16)
    kv = _dot(h, w_ref[0].astype(BF16))
    for b in range(nb):
        rows = kv[b * m:(b + 1) * m]
        kd_ref[0, b] = rows[:, :MIX_W]
        vd_ref[0, b] = rows[:, MIX_W:]
        for hd in range(N_HEADS):
            k_ref[0, b, :, hd, :] = rows[:, hd * HEAD_W:(hd + 1) * HEAD_W]
            v_ref[0, b, :, hd, :] = rows[:, MIX_W + hd * HEAD_W:MIX_W + (hd + 1) * HEAD_W]


def _mem_kv(mem, g_mem, w_mem_kv, cast=()):
    B, M, D = mem.shape
    L = g_mem.shape[0]
    out = jax.ShapeDtypeStruct((L, B, M, N_HEADS, HEAD_W), F32)
    dense = jax.ShapeDtypeStruct((L, B, M, MIX_W), F32)

    def chunk(a):
        assert a.shape[1] % L == 0
        return pl.BlockSpec((a.shape[0], a.shape[1] // L, a.shape[2]), lambda l: (0, l, 0))

    return pl.pallas_call(
        functools.partial(_memkv_kernel, n_cast=len(cast)),
        grid=(L,),
        in_specs=[_resident((B, M, D), lambda l: (0, 0, 0)),
                  pl.BlockSpec((1, 1, D), lambda l: (l, 0, 0)),
                  pl.BlockSpec((1, D, 2 * MIX_W), lambda l: (l, 0, 0))] + [chunk(a) for a in cast],
        out_specs=[pl.BlockSpec((1, B, M, N_HEADS, HEAD_W), lambda l: (l, 0, 0, 0, 0))] * 2
                  + [pl.BlockSpec((1, B, M, MIX_W), lambda l: (l, 0, 0, 0))] * 2 + [chunk(a) for a in cast],
        out_shape=[out, out, dense, dense] + [jax.ShapeDtypeStruct(a.shape, BF16) for a in cast],
        compiler_params=_cparams("arbitrary"),
        name="mem_kv",
    )(mem, g_mem.reshape(L, 1, D), w_mem_kv, *cast)


def _batch_spec(arr, layer=None):
    if layer is None:
        zeros = (0,) * (arr.ndim - 1)
        return pl.BlockSpec((1,) + arr.shape[1:], lambda b, *_: (b,) + zeros)
    zeros = (0,) * (arr.ndim - 2)
    return pl.BlockSpec((None, 1) + arr.shape[2:], lambda b, *_: (layer, b) + zeros)


def _head(ref, h):
    if len(ref.shape) == 3:
        return ref[0, :, h * HEAD_W:(h + 1) * HEAD_W]
    return ref[0, :, h, :]


def _mem_attend(qm, mk_ref, mv_ref, out_ref, rs=slice(None)):
    ones = jnp.ones((mk_ref.shape[1], HEAD_W), BF16)
    for h in range(N_HEADS):
        hs = slice(h * HEAD_W, (h + 1) * HEAD_W)
        kh = _head(mk_ref, h).astype(BF16)
        vh = jnp.concatenate([_head(mv_ref, h).astype(BF16), ones], axis=1)
        s = _dot_nt(qm[:, hs], kh)
        e = jnp.exp2(s - jnp.max(s, axis=-1, keepdims=True))
        pv = _dot(e.astype(BF16), vh)
        o = pv[:, :HEAD_W] / pv[:, HEAD_W:]
        out_ref[0, rs, MIX_W + h * HEAD_W:MIX_W + (h + 1) * HEAD_W] = o.astype(out_ref.dtype)


def _mix0_kernel(x_ref, g_ref, wi_ref, hist_ref, wp_ref, ps_ref, mk_ref, mv_ref, *rest, tm, sub, pos0, n_cast):
    cast_in = rest[:n_cast]
    out_ref, tail_ref = rest[n_cast:n_cast + 2]
    cast_out = rest[n_cast + 2:2 * n_cast + 2]
    carry_ref = rest[2 * n_cast + 2]
    for src, dst in zip(cast_in, cast_out):
        dst[...] = src[...].astype(dst.dtype)
    i = pl.program_id(1)

    @pl.when(i == 0)
    def _():
        carry_ref[...] = hist_ref[0]

    halo = carry_ref[...]
    for r in range(tm // sub):
        rs = slice(r * sub, (r + 1) * sub)
        x = x_ref[0, rs, :]
        z = _dot((x * _inv_rms(x) * g_ref[...]).astype(BF16), wi_ref[...])
        u = z[:, :MIX_W]
        qm = (z[:, MIX_W:] * QM_SCALE).astype(BF16)
        ext = jnp.concatenate([halo, u], axis=0)
        halo = u[sub - HALO:]
        pos = pos0 + i * tm + r * sub + lax.broadcasted_iota(jnp.int32, (sub, HEAD_W), 0)
        for g, w in enumerate(POOL_WINDOWS):
            gs = slice(g * HEAD_W, (g + 1) * HEAD_W)
            acc = ext[:, gs]
            span = 1
            while span < w:
                acc = acc + pltpu.roll(acc, span, 0)
                span *= 2
            cnt = jnp.minimum(pos + 1, w).astype(F32)
            d = acc[HALO:] / cnt - u[:, gs]
            t = _dot(d.astype(BF16), wp_ref[g]) * ps_ref[:, gs]
            out_ref[0, rs, gs] = t.astype(out_ref.dtype)
        _mem_attend(qm, mk_ref, mv_ref, out_ref, rs)
    carry_ref[...] = halo
    tail_ref[0] = halo


def _mix0(x, g_attn, w_in, hist16, mem_k, mem_v, w_pool, pool_scale, layer, tm, pos0, cast=()):
    B, T, D = x.shape
    sub = min(tm, MIX0_SUB_ROWS)
    n_i = T // tm
    assert T % tm == 0 and tm % sub == 0 and sub >= HALO

    def chunk(a):
        assert a.shape[1] % (B * n_i) == 0
        return pl.BlockSpec((a.shape[0], a.shape[1] // (B * n_i), a.shape[2]), lambda b, i: (0, b * n_i + i, 0))

    return pl.pallas_call(
        functools.partial(_mix0_kernel, tm=tm, sub=sub, pos0=pos0, n_cast=len(cast)),
        grid=(B, n_i),
        in_specs=[pl.BlockSpec((1, tm, D), lambda b, i: (b, i, 0)),
                  _layer(_rows(g_attn), layer), _layer(w_in, layer),
                  pl.BlockSpec((1, HALO, MIX_W), lambda b, i: (b, 0, 0)),
                  _layer(w_pool, layer), _layer(_rows(pool_scale), layer),
                  _batch_spec(mem_k, layer), _batch_spec(mem_v, layer)] + [chunk(a) for a in cast],
        out_specs=[pl.BlockSpec((1, tm, 2 * MIX_W), lambda b, i: (b, i, 0)),
                   pl.BlockSpec((1, HALO, MIX_W), lambda b, i: (b, 0, 0))] + [chunk(a) for a in cast],
        out_shape=[jax.ShapeDtypeStruct((B, T, 2 * MIX_W), BF16),
                   jax.ShapeDtypeStruct((B, HALO, MIX_W), F32)]
                  + [jax.ShapeDtypeStruct(a.shape, BF16) for a in cast],
        scratch_shapes=[pltpu.VMEM((HALO, MIX_W), F32)],
        compiler_params=_cparams("arbitrary", "arbitrary"),
        name="mix0",
    )(x, _rows(g_attn), w_in, hist16, w_pool, _rows(pool_scale), mem_k, mem_v, *cast)


def _lane_window(lane, lo):
    return (lane >= lo) & (lane < lo + N_SPLIT)


def _head_copies(buf, slot, hbm_refs, row0, tm, sem):
    out = []
    for t, hbm in enumerate(hbm_refs):
        for h in range(N_HEADS):
            src = buf.at[slot, :, pl.ds(t * MIX_W + h * HEAD_W, HEAD_W)]
            out.append(pltpu.make_async_copy(src, hbm.at[pl.ds(row0, tm), h, :], sem.at[slot]))
    return out


def _project_layer1(x, ga_ref, wi_ref, gk_ref, wk_ref, z_ref, bf_refs, k_hbm, v_hbm, kv_buf, sem, *, seq_len):
    tm = x.shape[0]
    i = pl.program_id(0)
    n_steps = pl.num_programs(0)
    slot = i % 2
    copies = functools.partial(_head_copies, kv_buf, hbm_refs=(k_hbm, v_hbm), tm=tm, sem=sem)

    xn = x * _inv_rms(x)
    z = _dot((xn * ga_ref[...]).astype(BF16), wi_ref[...])
    z_ref[:, :MIX_W] = (z[:, :MIX_W] * Q_SCALE).astype(BF16)
    z_ref[:, MIX_W:] = (z[:, MIX_W:] * QM_SCALE).astype(BF16)
    kv = _dot((xn * gk_ref[...]).astype(BF16), wk_ref[...])

    @pl.when(i >= 2)
    def _():
        for c in copies(slot, row0=(i - 2) * tm):
            c.wait()

    kv_buf[slot] = kv
    for c in copies(slot, row0=i * tm):
        c.start()

    @pl.when(i == n_steps - 1)
    def _():
        @pl.when(i >= 1)
        def _():
            for c in copies(1 - slot, row0=(i - 1) * tm):
                c.wait()
        for c in copies(slot, row0=i * tm):
            c.wait()

    if bf_refs:
        k = kv[:, :MIX_W]
        v = kv[:, MIX_W:]
        k1a_ref, k2a_ref, vb_ref = bf_refs
        lane = lax.broadcasted_iota(jnp.int32, (tm, HEAD_W), 1)
        pos = (i * tm) % seq_len + lax.broadcasted_iota(jnp.int32, (tm, HEAD_W), 0)
        off = pos % CHUNK
        base = (pos - off).astype(F32)
        off = off.astype(F32)

        def pos_lanes(lo):
            return jnp.where(_lane_window(lane, lo), base, jnp.where(_lane_window(lane, lo + N_SPLIT), off, 0.0))

        pos_hi = pos_lanes(DIFF_D)
        pos_lo = pos_lanes(0)
        for h in range(N_HEADS):
            hs = slice(h * HEAD_W, (h + 1) * HEAD_W)
            k1a_ref[:, hs] = jnp.where(lane < DIFF_D, k[:, hs], pos_hi).astype(BF16)
            k2a_ref[:, hs] = jnp.where(lane >= DIFF_D, k[:, hs], pos_lo).astype(BF16)
        vb_ref[...] = v.astype(BF16)


def _outmlp_kernel(x_ref, cat_ref, wo_ref, g_ref, w1_ref, w2_ref, gf_ref, *rest, final_norm, ff_chunk, n_bf, seq_len):
    if final_norm:
        out_ref, a_ref = rest
    else:
        ga_ref, wi_ref, gk_ref, wk_ref, out_ref, z_ref = rest[:6]
        bf_refs = rest[6:6 + n_bf]
        k_hbm, v_hbm, a_ref, kv_buf, sem = rest[6 + n_bf:]
    x1 = x_ref[...] + _dot(cat_ref[...], wo_ref[...])
    h = (x1 * _inv_rms(x1) * g_ref[...]).astype(BF16)
    for c in range(w1_ref.shape[1] // ff_chunk):
        cs = slice(c * ff_chunk, (c + 1) * ff_chunk)
        a = jnp.maximum(_dot(h, w1_ref[:, cs]), 0.0)
        a_ref[:, cs] = (a * a).astype(BF16)
    x2 = x1 + _dot(a_ref[...], w2_ref[...])
    if final_norm:
        out_ref[...] = x2 * _inv_rms(x2) * gf_ref[...]
    else:
        out_ref[...] = x2
        _project_layer1(x2, ga_ref, wi_ref, gk_ref, wk_ref, z_ref, bf_refs, k_hbm, v_hbm, kv_buf, sem,
                        seq_len=seq_len)


def _outmlp(x, cat, w, layer, tm, final_norm, seq_len=None, attn_operands=False):
    R, D = x.shape
    FF = w["w_ff1"].shape[-1]
    row = lambda i: (i, 0)
    fix = lambda i: (0, 0)
    n_bf = 3 if attn_operands else 0
    assert not attn_operands or seq_len <= 256 * CHUNK
    in_specs = [pl.BlockSpec((tm, D), row), pl.BlockSpec((tm, D), row),
                _layer(w["w_out"], layer), _layer(_rows(w["g_ffn"]), layer),
                _layer(w["w_ff1"], layer), _layer(w["w_ff2"], layer), _resident((1, D), fix)]
    args = [x, cat, w["w_out"], _rows(w["g_ffn"]), w["w_ff1"], w["w_ff2"], w["g_final"].reshape(1, D)]
    out_specs = [pl.BlockSpec((tm, D), row)]
    out_shape = [jax.ShapeDtypeStruct((R, D), F32)]
    scratch = [pltpu.VMEM((tm, FF), BF16)]
    if not final_norm:
        in_specs += [_layer(_rows(w["g_attn"]), layer + 1), _layer(w["w_in"], layer + 1),
                     _resident((1, D), fix), _layer(w["w_kv"], 0)]
        args += [_rows(w["g_attn"]), w["w_in"], w["g_kv"].reshape(1, D), w["w_kv"]]
        out_specs += ([pl.BlockSpec((tm, 2 * MIX_W), row)] + [pl.BlockSpec((tm, MIX_W), row)] * n_bf
                      + [pl.BlockSpec(memory_space=pl.ANY)] * 2)
        out_shape += ([jax.ShapeDtypeStruct((R, 2 * MIX_W), BF16)] + [jax.ShapeDtypeStruct((R, MIX_W), BF16)] * n_bf
                      + [jax.ShapeDtypeStruct((R, N_HEADS, HEAD_W), F32)] * 2)
        scratch += [pltpu.VMEM((2, tm, 2 * MIX_W), F32), pltpu.SemaphoreType.DMA((2,))]
    return pl.pallas_call(
        functools.partial(_outmlp_kernel, final_norm=final_norm, ff_chunk=1024, n_bf=n_bf, seq_len=seq_len),
        grid=(R // tm,),
        in_specs=in_specs,
        out_specs=out_specs,
        out_shape=out_shape,
        scratch_shapes=scratch,
        compiler_params=_cparams("arbitrary"),
        name="outmlp",
    )(*args)


def _lambda(lq, lam_init):
    a = jnp.sum(lq[0:1] * lq[1:2], axis=-1, keepdims=True)
    b = jnp.sum(lq[2:3] * lq[3:4], axis=-1, keepdims=True)
    return jnp.exp(a) - jnp.exp(b) + lam_init


def _slope2(h):
    return 2.0 ** (-8.0 * (h + 1) / N_HEADS) * LOG2E


def _bf16_pieces(c):
    out = []
    for _ in range(N_SPLIT):
        p = float(np.asarray(c, dtype=ml_dtypes.bfloat16))
        out.append(p)
        c -= p
    return out


def _slope_lanes(h, lo, shape):
    lane = lax.broadcasted_iota(jnp.int32, shape, 1)
    out = jnp.zeros(shape, F32)
    for p, piece in enumerate(_bf16_pieces(_slope2(h))):
        out = jnp.where((lane == lo + p) | (lane == lo + N_SPLIT + p), piece, out)
    return out.astype(BF16)


def _finish_head(o1, o2, lam, gs, lam_init):
    o = o1 - lam * o2
    return o * _inv_rms(o) * gs * (1.0 - lam_init)


def _diff_kernel(lq_ref, gs_ref, q_ref, qn_ref, qm_ref, k10_ref, k20_ref, v0_ref, k1n_ref, k2n_ref, vn_ref,
                 mk_ref, mv_ref, out_ref, kv_ref, qa_ref, acc_ref, m_ref, s_ref, *, tq, lam_init):
    kv0_refs = (k10_ref, k20_ref, v0_ref)
    kvn_refs = (k1n_ref, k2n_ref, vn_ref)
    i = pl.program_id(1)
    n_tiles = pl.num_programs(1)
    nblk = tq // HEAD_W
    n_chain = 2 * N_HEADS
    slot = i % 2
    row = lax.broadcasted_iota(jnp.int32, (tq, tq), 0)
    col = lax.broadcasted_iota(jnp.int32, (tq, tq), 1)
    ahead = jnp.where((col // CHUNK) <= (row // CHUNK), jnp.maximum(col - row, 0).astype(F32), -NEG_BIG)
    ones = jnp.ones((tq, HEAD_W), BF16)
    lane = lax.broadcasted_iota(jnp.int32, (tq, HEAD_W), 1)
    q0 = pl.multiple_of(i * tq, tq)
    qn0 = pl.multiple_of(jnp.minimum(i + 1, n_tiles - 1) * tq, tq)

    def hslice(n):
        return slice((n // 2) * HEAD_W, (n // 2 + 1) * HEAD_W)

    def scores(n, k0, q_slot=slot):
        return _dot_nt(qa_ref[q_slot * n_chain + n], kv_ref[n % 2, pl.ds(k0, tq), hslice(n)])

    def v_aug(n, k0):
        return jnp.concatenate([kv_ref[2, pl.ds(k0, tq), hslice(n)], ones], axis=1)

    def diag_fix(h):
        return (-2.0 * _slope2(h)) * ahead

    def build_queries(src_ref, q_slot):
        for h in range(N_HEADS):
            qh = src_ref[0, :, h * HEAD_W:(h + 1) * HEAD_W]
            base = q_slot * n_chain + 2 * h
            qa_ref[base] = jnp.where(lane < DIFF_D, qh, _slope_lanes(h, DIFF_D, qh.shape))
            qa_ref[base + 1] = jnp.where(lane >= DIFF_D, qh, _slope_lanes(h, 0, qh.shape))

    def consume(n, k0, first):
        vt = v_aug(n, k0)
        blocks = [s_ref[n, :, b * HEAD_W:(b + 1) * HEAD_W] for b in range(nblk)]
        m_row = jnp.max(functools.reduce(jnp.maximum, blocks), axis=-1, keepdims=True)
        if first:
            m_new = jnp.broadcast_to(m_row, (tq, HEAD_W))
        else:
            m_old = m_ref[n]
            m_new = jnp.maximum(m_old, m_row)
        p = jnp.concatenate([jnp.exp2(b - m_new) for b in blocks], axis=1).astype(BF16)
        pv = _dot(p, vt)
        if first:
            acc_ref[n] = pv
        else:
            alpha = jnp.exp2(m_old - m_new)
            acc_ref[n] = jnp.concatenate([alpha, alpha], axis=1) * acc_ref[n] + pv
        m_ref[n] = m_new

    def stage_next_diag(n):
        s_ref[n] = scores(n, qn0, 1 - slot) + diag_fix(n // 2)

    @pl.when(i == 0)
    def _():
        for c in range(3):
            kv_ref[c, pl.ds(0, tq), :] = kv0_refs[c][0]
        build_queries(q_ref, slot)
        for n in range(n_chain):
            s_ref[n] = scores(n, q0) + diag_fix(n // 2)

    for c in range(3):
        kv_ref[c, pl.ds(qn0, tq), :] = kvn_refs[c][0]
    build_queries(qn_ref, 1 - slot)
    _mem_attend(qm_ref[0], mk_ref, mv_ref, out_ref)

    @pl.when(i == 0)
    def _():
        for n in range(n_chain):
            consume(n, q0, True)
            stage_next_diag(n)

    @pl.when(i > 0)
    def _():
        for n in range(n_chain):
            consume(n, q0, True)
            s_ref[n] = scores(n, 0)

        def body(j, carry):
            k_cur = pl.multiple_of((j - 1) * tq, tq)
            k_next = pl.multiple_of(j * tq, tq)
            for n in range(n_chain):
                consume(n, k_cur, False)
                s_ref[n] = scores(n, k_next)
            return carry

        lax.fori_loop(1, i, body, 0)
        k_last = pl.multiple_of((i - 1) * tq, tq)
        for n in range(n_chain):
            consume(n, k_last, False)
            stage_next_diag(n)

    lam = _lambda(lq_ref[...], lam_init)
    for h in range(N_HEADS):
        o = [acc_ref[n, :, :HEAD_W] / acc_ref[n, :, HEAD_W:] for n in (2 * h, 2 * h + 1)]
        out_ref[0, :, h * HEAD_W:(h + 1) * HEAD_W] = _finish_head(
            o[0], o[1], lam, gs_ref[...], lam_init).astype(out_ref.dtype)


def _diff_prompt(z, k1a, k2a, vb, mem_k, mem_v, lq, g_sub, layer, lam_init, tq):
    B, T, _ = z.shape
    n_chain = 2 * N_HEADS
    n_tiles = T // tq
    diff_layer = layer - (mem_k.shape[0] - lq.shape[0])
    nxt = lambda i: jnp.minimum(i + 1, n_tiles - 1)
    return pl.pallas_call(
        functools.partial(_diff_kernel, tq=tq, lam_init=lam_init),
        grid=(B, n_tiles),
        in_specs=[_layer(lq, diff_layer), _layer(_rows(g_sub), diff_layer),
                  pl.BlockSpec((1, tq, MIX_W), lambda b, i: (b, i, 0)),
                  pl.BlockSpec((1, tq, MIX_W), lambda b, i: (b, nxt(i), 0)),
                  pl.BlockSpec((1, tq, MIX_W), lambda b, i: (b, i, 1))]
                 + [_resident((1, tq, MIX_W), lambda b, i: (b, 0, 0))] * 3
                 + [pl.BlockSpec((1, tq, MIX_W), lambda b, i: (b, nxt(i), 0))] * 3
                 + [_batch_spec(mem_k, layer), _batch_spec(mem_v, layer)],
        out_specs=pl.BlockSpec((1, tq, 2 * MIX_W), lambda b, i: (b, i, 0)),
        out_shape=jax.ShapeDtypeStruct((B, T, 2 * MIX_W), BF16),
        scratch_shapes=[pltpu.VMEM((3, T, MIX_W), BF16),
                        pltpu.VMEM((2 * n_chain, tq, HEAD_W), BF16),
                        pltpu.VMEM((n_chain, tq, 2 * HEAD_W), F32),
                        pltpu.VMEM((n_chain, tq, HEAD_W), F32),
                        pltpu.VMEM((n_chain, tq, tq), F32)],
        compiler_params=_cparams("arbitrary", "arbitrary"),
        name="diff_prompt",
    )(lq, _rows(g_sub), z, z, z, k1a, k2a, vb, k1a, k2a, vb, mem_k, mem_v)


def _cache_copies(hbm_refs, b, buf, slot, sem):
    return [pltpu.make_async_copy(hbm.at[b, :, h, :], buf.at[slot, t, h], sem.at[slot])
            for t, hbm in enumerate(hbm_refs) for h in range(N_HEADS)]


def _diff_sample_kernel(lq_ref, gs_ref, q_ref, qm_ref, ck_hbm, cv_hbm, nk_ref, nv_ref, mk_ref, mv_ref, out_ref,
                        cache_buf, sem, *, p0, lam_init):
    b = pl.program_id(0)
    slot = b % 2
    copies = functools.partial(_cache_copies, (ck_hbm, cv_hbm), buf=cache_buf, sem=sem)

    @pl.when(b == 0)
    def _():
        for c in copies(b, slot=slot):
            c.start()

    @pl.when(b + 1 < pl.num_programs(0))
    def _():
        for c in copies(b + 1, slot=1 - slot):
            c.start()

    for c in copies(b, slot=slot):
        c.wait()

    ts = q_ref.shape[1]
    lam = _lambda(lq_ref[...], lam_init)
    qpos = p0 + lax.broadcasted_iota(jnp.int32, (2 * ts, p0), 0) % ts
    dist_p = (qpos - lax.broadcasted_iota(jnp.int32, (2 * ts, p0), 1)).astype(F32)
    row = p0 + lax.broadcasted_iota(jnp.int32, (2 * ts, ts), 0) % ts
    col = p0 + lax.broadcasted_iota(jnp.int32, (2 * ts, ts), 1)
    dist_n = jnp.abs(row - col).astype(F32)
    vis_n = (col // CHUNK) <= (row // CHUNK)
    lane = lax.broadcasted_iota(jnp.int32, (ts, HEAD_W), 1)
    for h in range(N_HEADS):
        hs = slice(h * HEAD_W, (h + 1) * HEAD_W)
        qh = q_ref[0, :, hs]
        zero = jnp.zeros_like(qh)
        q2 = jnp.concatenate([jnp.where(lane < DIFF_D, qh, zero), jnp.where(lane >= DIFF_D, qh, zero)], axis=0)
        kp = cache_buf[slot, 0, h].astype(BF16)
        vp = cache_buf[slot, 1, h].astype(BF16)
        kn = _head(nk_ref, h).astype(BF16)
        vn = _head(nv_ref, h).astype(BF16)
        sp = _dot_nt(q2, kp) - _slope2(h) * dist_p
        sn = _dot_nt(q2, kn) + jnp.where(vis_n, -_slope2(h) * dist_n, NEG_BIG)
        m = jnp.maximum(jnp.max(sp, axis=-1, keepdims=True), jnp.max(sn, axis=-1, keepdims=True))
        ep = jnp.exp2(sp - m)
        en = jnp.exp2(sn - m)
        l = jnp.sum(ep, axis=-1, keepdims=True) + jnp.sum(en, axis=-1, keepdims=True)
        o = (_dot(ep.astype(BF16), vp) + _dot(en.astype(BF16), vn)) / l
        out_ref[0, :, hs] = _finish_head(o[:ts], o[ts:], lam, gs_ref[...], lam_init).astype(out_ref.dtype)
    _mem_attend(qm_ref[0], mk_ref, mv_ref, out_ref)


def _diff_sample(z, cache_k, cache_v, k_new, v_new, mem_k, mem_v, lq, g_sub, layer, lam_init):
    B, Ts, _ = z.shape
    P0 = cache_k.shape[1]
    diff_layer = layer - (mem_k.shape[0] - lq.shape[0])
    return pl.pallas_call(
        functools.partial(_diff_sample_kernel, p0=P0, lam_init=lam_init),
        grid=(B,),
        in_specs=[_layer(lq, diff_layer), _layer(_rows(g_sub), diff_layer),
                  pl.BlockSpec((1, Ts, MIX_W), lambda b: (b, 0, 0)),
                  pl.BlockSpec((1, Ts, MIX_W), lambda b: (b, 0, 1)),
                  pl.BlockSpec(memory_space=pl.ANY), pl.BlockSpec(memory_space=pl.ANY),
                  _batch_spec(k_new), _batch_spec(v_new), _batch_spec(mem_k, layer), _batch_spec(mem_v, layer)],
        out_specs=pl.BlockSpec((1, Ts, 2 * MIX_W), lambda b: (b, 0, 0)),
        out_shape=jax.ShapeDtypeStruct((B, Ts, 2 * MIX_W), BF16),
        scratch_shapes=[pltpu.VMEM((2, 2, N_HEADS, P0, HEAD_W), F32), pltpu.SemaphoreType.DMA((2,))],
        compiler_params=_cparams("arbitrary"),
        name="diff_sample",
    )(lq, _rows(g_sub), z, z, cache_k, cache_v, k_new, v_new, mem_k, mem_v)


def _lambda_init(layer_idx):
    return 0.8 - 0.6 * math.exp(-0.3 * layer_idx)


def _trunk(x, pos0, pool_hist, mem_k, mem_v, past_k, past_v, w, tm_rows, tm_seq, tq):
    B, T, D = x.shape
    R = B * T
    xr = x.reshape(R, D)
    hist16 = jnp.pad(pool_hist, ((0, 0), (HALO - POOL_HIST, 0), (0, 0)))

    to_cast = [name for name in ("w_ff1", "w_ff2") if w[name].dtype != BF16]
    cat0, tail, *cast = _mix0(x, w["g_attn"], w["w_in"], hist16, mem_k, mem_v, w["w_pool"], w["pool_scale"], 0,
                              tm_seq, pos0, cast=[w[name] for name in to_cast])
    w = dict(w, **dict(zip(to_cast, cast)))
    new_pool = tail[:, HALO - POOL_HIST:]
    prompt = past_k is None
    x1, z, *bf, k, v = _outmlp(xr, cat0.reshape(R, D), w, 0, tm_rows, final_norm=False, seq_len=T,
                               attn_operands=prompt)
    z3 = z.reshape(B, T, D)
    lam_init = _lambda_init(1)
    if prompt:
        k1a, k2a, vb = (a.reshape(B, T, MIX_W) for a in bf)
        cat1 = _diff_prompt(z3, k1a, k2a, vb, mem_k, mem_v, w["lambda_qk"], w["g_subln"], 1, lam_init, tq)
    else:
        shp = (B, T, N_HEADS, HEAD_W)
        cat1 = _diff_sample(z3, past_k, past_v, k.reshape(shp), v.reshape(shp), mem_k, mem_v,
                            w["lambda_qk"], w["g_subln"], 1, lam_init)
    y, = _outmlp(x1, cat1.reshape(R, D), w, 1, min(R, 2 * tm_rows), final_norm=True)
    shp = (B, T, N_HEADS, HEAD_W)
    return y.reshape(B, T, D), new_pool[None], k.reshape(shp), v.reshape(shp), w


def kernel(x_prompt, x_sample, mem_prompt, cache_k, cache_v, cache_mem_k, cache_mem_v, state_pool, g_attn, w_in, w_out, g_mem, w_mem_kv, g_ffn, w_ff1, w_ff2, w_pool, pool_scale, lambda_qk, g_subln, g_kv, w_kv, g_final):
    B, T, D = x_prompt.shape
    Bs, Ts, _ = x_sample.shape
    P0 = cache_k.shape[1]
    mem_k_p, mem_v_p, mem_k_d, mem_v_d, w_in_b, w_out_b, w_kv_b = _mem_kv(
        mem_prompt, g_mem, w_mem_kv, cast=(w_in, w_out, w_kv[None]))
    w = dict(g_attn=g_attn, w_in=w_in_b, w_out=w_out_b, g_ffn=g_ffn,
             w_ff1=w_ff1, w_ff2=w_ff2, w_pool=w_pool.astype(BF16),
             pool_scale=pool_scale, lambda_qk=lambda_qk, g_subln=g_subln, g_kv=g_kv,
             w_kv=w_kv_b, g_final=g_final)

    hist0 = jnp.zeros((B, POOL_HIST, MIX_W), F32)
    y_p, pool_p, k_p, v_p, w = _trunk(x_prompt, 0, hist0, mem_k_d, mem_v_d, None, None, w,
                                      tm_rows=512, tm_seq=2048, tq=512)
    y_s, pool_s, k_s, v_s, _ = _trunk(x_sample, P0, state_pool[0], cache_mem_k, cache_mem_v, cache_k, cache_v, w,
                                      tm_rows=Bs * Ts, tm_seq=Ts, tq=None)
    return (y_p, y_s, mem_k_p, mem_v_p, pool_p, k_p, v_p, pool_s, k_s, v_s)
```

```python
import functools
import math

import jax
import jax.numpy as jnp
import ml_dtypes
import numpy as np
from jax import lax
from jax.experimental import pallas as pl
from jax.experimental.pallas import tpu as pltpu

F32 = jnp.float32
BF16 = jnp.bfloat16

CHUNK = 64
POOL_WINDOWS = (2, 4, 8, 16)
POOL_HIST = 15
HALO = 16
N_HEADS = 4
HEAD_W = 128
DIFF_D = 64
MIX_W = N_HEADS * HEAD_W
EPS = 1e-6
NEG_BIG = -1e30
LOG2E = math.log2(math.e)
Q_SCALE = DIFF_D ** -0.5 * LOG2E
QM_SCALE = HEAD_W ** -0.5 * LOG2E
N_SPLIT = 4
MIX0_SUB_ROWS = 512
VMEM_LIMIT = 56 * 1024 * 1024


def _cparams(*sem):
    return pltpu.CompilerParams(dimension_semantics=sem, vmem_limit_bytes=VMEM_LIMIT)


def _resident(shape, index_map):
    return pl.BlockSpec(shape, index_map, pipeline_mode=pl.Buffered(1))


def _layer(arr, layer):
    zeros = (0,) * (arr.ndim - 1)
    return pl.BlockSpec((None,) + arr.shape[1:], lambda *_: (layer,) + zeros, pipeline_mode=pl.Buffered(1))


def _rows(g):
    return g.reshape(g.shape[0], 1, g.shape[1])


def _inv_rms(x):
    return lax.rsqrt(jnp.mean(x * x, axis=-1, keepdims=True) + EPS)


def _dot(a, b):
    return jnp.dot(a, b, preferred_element_type=F32)


def _dot_nt(a, b):
    return lax.dot_general(a, b, (((1,), (1,)), ((), ())), preferred_element_type=F32)


def _memkv_kernel(x_ref, g_ref, w_ref, *rest, n_cast):
    cast_in = rest[:n_cast]
    k_ref, v_ref, kd_ref, vd_ref = rest[n_cast:n_cast + 4]
    for src, dst in zip(cast_in, rest[n_cast + 4:]):
        dst[...] = src[...].astype(dst.dtype)
    nb, m, d = x_ref.shape
    x = x_ref[...].reshape(nb * m, d)
    h = (x * _inv_rms(x) * g_ref[0]).astype(BF16)
    kv = _dot(h, w_ref[0].astype(BF16))
    for b in range(nb):
        rows = kv[b * m:(b + 1) * m]
        kd_ref[0, b] = rows[:, :MIX_W]
        vd_ref[0, b] = rows[:, MIX_W:]
        for hd in range(N_HEADS):
            k_ref[0, b, :, hd, :] = rows[:, hd * HEAD_W:(hd + 1) * HEAD_W]
            v_ref[0, b, :, hd, :] = rows[:, MIX_W + hd * HEAD_W:MIX_W + (hd + 1) * HEAD_W]


def _mem_kv(mem, g_mem, w_mem_kv, cast=()):
    B, M, D = mem.shape
    L = g_mem.shape[0]
    out = jax.ShapeDtypeStruct((L, B, M, N_HEADS, HEAD_W), F32)
    dense = jax.ShapeDtypeStruct((L, B, M, MIX_W), F32)

    def chunk(a):
        assert a.shape[1] % L == 0
        return pl.BlockSpec((a.shape[0], a.shape[1] // L, a.shape[2]), lambda l: (0, l, 0))

    return pl.pallas_call(
        functools.partial(_memkv_kernel, n_cast=len(cast)),
        grid=(L,),
        in_specs=[_resident((B, M, D), lambda l: (0, 0, 0)),
                  pl.BlockSpec((1, 1, D), lambda l: (l, 0, 0)),
                  pl.BlockSpec((1, D, 2 * MIX_W), lambda l: (l, 0, 0))] + [chunk(a) for a in cast],
        out_specs=[pl.BlockSpec((1, B, M, N_HEADS, HEAD_W), lambda l: (l, 0, 0, 0, 0))] * 2
                  + [pl.BlockSpec((1, B, M, MIX_W), lambda l: (l, 0, 0, 0))] * 2 + [chunk(a) for a in cast],
        out_shape=[out, out, dense, dense] + [jax.ShapeDtypeStruct(a.shape, BF16) for a in cast],
        compiler_params=_cparams("arbitrary"),
        name="mem_kv",
    )(mem, g_mem.reshape(L, 1, D), w_mem_kv, *cast)


def _batch_spec(arr, layer=None):
    if layer is None:
        zeros = (0,) * (arr.ndim - 1)
        return pl.BlockSpec((1,) + arr.shape[1:], lambda b, *_: (b,) + zeros)
    zeros = (0,) * (arr.ndim - 2)
    return pl.BlockSpec((None, 1) + arr.shape[2:], lambda b, *_: (layer, b) + zeros)


def _head(ref, h):
    if len(ref.shape) == 3:
        return ref[0, :, h * HEAD_W:(h + 1) * HEAD_W]
    return ref[0, :, h, :]


def _mem_attend(qm, mk_ref, mv_ref, out_ref, rs=slice(None)):
    ones = jnp.ones((mk_ref.shape[1], HEAD_W), BF16)
    for h in range(N_HEADS):
        hs = slice(h * HEAD_W, (h + 1) * HEAD_W)
        kh = _head(mk_ref, h).astype(BF16)
        vh = jnp.concatenate([_head(mv_ref, h).astype(BF16), ones], axis=1)
        s = _dot_nt(qm[:, hs], kh)
        e = jnp.exp2(s - jnp.max(s, axis=-1, keepdims=True))
        pv = _dot(e.astype(BF16), vh)
        o = pv[:, :HEAD_W] / pv[:, HEAD_W:]
        out_ref[0, rs, MIX_W + h * HEAD_W:MIX_W + (h + 1) * HEAD_W] = o.astype(out_ref.dtype)


def _mix0_kernel(x_ref, g_ref, wi_ref, hist_ref, wp_ref, ps_ref, mk_ref, mv_ref, *rest, tm, sub, pos0, n_cast):
    cast_in = rest[:n_cast]
    out_ref, tail_ref = rest[n_cast:n_cast + 2]
    cast_out = rest[n_cast + 2:2 * n_cast + 2]
    carry_ref = rest[2 * n_cast + 2]
    for src, dst in zip(cast_in, cast_out):
        dst[...] = src[...].astype(dst.dtype)
    i = pl.program_id(1)

    @pl.when(i == 0)
    def _():
        carry_ref[...] = hist_ref[0]

    halo = carry_ref[...]
    for r in range(tm // sub):
        rs = slice(r * sub, (r + 1) * sub)
        x = x_ref[0, rs, :]
        z = _dot((x * _inv_rms(x) * g_ref[...]).astype(BF16), wi_ref[...])
        u = z[:, :MIX_W]
        qm = (z[:, MIX_W:] * QM_SCALE).astype(BF16)
        ext = jnp.concatenate([halo, u], axis=0)
        halo = u[sub - HALO:]
        pos = pos0 + i * tm + r * sub + lax.broadcasted_iota(jnp.int32, (sub, HEAD_W), 0)
        for g, w in enumerate(POOL_WINDOWS):
            gs = slice(g * HEAD_W, (g + 1) * HEAD_W)
            acc = ext[:, gs]
            span = 1
            while span < w:
                acc = acc + pltpu.roll(acc, span, 0)
                span *= 2
            cnt = jnp.minimum(pos + 1, w).astype(F32)
            d = acc[HALO:] / cnt - u[:, gs]
            t = _dot(d.astype(BF16), wp_ref[g]) * ps_ref[:, gs]
            out_ref[0, rs, gs] = t.astype(out_ref.dtype)
        _mem_attend(qm, mk_ref, mv_ref, out_ref, rs)
    carry_ref[...] = halo
    tail_ref[0] = halo


def _mix0(x, g_attn, w_in, hist16, mem_k, mem_v, w_pool, pool_scale, layer, tm, pos0, cast=()):
    B, T, D = x.shape
    sub = min(tm, MIX0_SUB_ROWS)
    n_i = T // tm
    assert T % tm == 0 and tm % sub == 0 and sub >= HALO

    def chunk(a):
        assert a.shape[1] % (B * n_i) == 0
        return pl.BlockSpec((a.shape[0], a.shape[1] // (B * n_i), a.shape[2]), lambda b, i: (0, b * n_i + i, 0))

    return pl.pallas_call(
        functools.partial(_mix0_kernel, tm=tm, sub=sub, pos0=pos0, n_cast=len(cast)),
        grid=(B, n_i),
        in_specs=[pl.BlockSpec((1, tm, D), lambda b, i: (b, i, 0)),
                  _layer(_rows(g_attn), layer), _layer(w_in, layer),
                  pl.BlockSpec((1, HALO, MIX_W), lambda b, i: (b, 0, 0)),
                  _layer(w_pool, layer), _layer(_rows(pool_scale), layer),
                  _batch_spec(mem_k, layer), _batch_spec(mem_v, layer)] + [chunk(a) for a in cast],
        out_specs=[pl.BlockSpec((1, tm, 2 * MIX_W), lambda b, i: (b, i, 0)),
                   pl.BlockSpec((1, HALO, MIX_W), lambda b, i: (b, 0, 0))] + [chunk(a) for a in cast],
        out_shape=[jax.ShapeDtypeStruct((B, T, 2 * MIX_W), BF16),
                   jax.ShapeDtypeStruct((B, HALO, MIX_W), F32)]
                  + [jax.ShapeDtypeStruct(a.shape, BF16) for a in cast],
        scratch_shapes=[pltpu.VMEM((HALO, MIX_W), F32)],
        compiler_params=_cparams("arbitrary", "arbitrary"),
        name="mix0",
    )(x, _rows(g_attn), w_in, hist16, w_pool, _rows(pool_scale), mem_k, mem_v, *cast)


def _lane_window(lane, lo):
    return (lane >= lo) & (lane < lo + N_SPLIT)


def _head_copies(buf, slot, hbm_refs, row0, tm, sem):
    out = []
    for t, hbm in enumerate(hbm_refs):
        for h in range(N_HEADS):
            src = buf.at[slot, :, pl.ds(t * MIX_W + h * HEAD_W, HEAD_W)]
            out.append(pltpu.make_async_copy(src, hbm.at[pl.ds(row0, tm), h, :], sem.at[slot]))
    return out


def _project_layer1(x, ga_ref, wi_ref, gk_ref, wk_ref, z_ref, bf_refs, k_hbm, v_hbm, kv_buf, sem, *, seq_len):
    tm = x.shape[0]
    i = pl.program_id(0)
    n_steps = pl.num_programs(0)
    slot = i % 2
    copies = functools.partial(_head_copies, kv_buf, hbm_refs=(k_hbm, v_hbm), tm=tm, sem=sem)

    xn = x * _inv_rms(x)
    z = _dot((xn * ga_ref[...]).astype(BF16), wi_ref[...])
    z_ref[:, :MIX_W] = (z[:, :MIX_W] * Q_SCALE).astype(BF16)
    z_ref[:, MIX_W:] = (z[:, MIX_W:] * QM_SCALE).astype(BF16)
    kv = _dot((xn * gk_ref[...]).astype(BF16), wk_ref[...])

    @pl.when(i >= 2)
    def _():
        for c in copies(slot, row0=(i - 2) * tm):
            c.wait()

    kv_buf[slot] = kv
    for c in copies(slot, row0=i * tm):
        c.start()

    @pl.when(i == n_steps - 1)
    def _():
        @pl.when(i >= 1)
        def _():
            for c in copies(1 - slot, row0=(i - 1) * tm):
                c.wait()
        for c in copies(slot, row0=i * tm):
            c.wait()

    if bf_refs:
        k = kv[:, :MIX_W]
        v = kv[:, MIX_W:]
        k1a_ref, k2a_ref, vb_ref = bf_refs
        lane = lax.broadcasted_iota(jnp.int32, (tm, HEAD_W), 1)
        pos = (i * tm) % seq_len + lax.broadcasted_iota(jnp.int32, (tm, HEAD_W), 0)
        off = pos % CHUNK
        base = (pos - off).astype(F32)
        off = off.astype(F32)

        def pos_lanes(lo):
            return jnp.where(_lane_window(lane, lo), base, jnp.where(_lane_window(lane, lo + N_SPLIT), off, 0.0))

        pos_hi = pos_lanes(DIFF_D)
        pos_lo = pos_lanes(0)
        for h in range(N_HEADS):
            hs = slice(h * HEAD_W, (h + 1) * HEAD_W)
            k1a_ref[:, hs] = jnp.where(lane < DIFF_D, k[:, hs], pos_hi).astype(BF16)
            k2a_ref[:, hs] = jnp.where(lane >= DIFF_D, k[:, hs], pos_lo).astype(BF16)
        vb_ref[...] = v.astype(BF16)


def _outmlp_kernel(x_ref, cat_ref, wo_ref, g_ref, w1_ref, w2_ref, gf_ref, *rest, final_norm, ff_chunk, n_bf, seq_len):
    if final_norm:
        out_ref, a_ref = rest
    else:
        ga_ref, wi_ref, gk_ref, wk_ref, out_ref, z_ref = rest[:6]
        bf_refs = rest[6:6 + n_bf]
        k_hbm, v_hbm, a_ref, kv_buf, sem = rest[6 + n_bf:]
    x1 = x_ref[...] + _dot(cat_ref[...], wo_ref[...])
    h = (x1 * _inv_rms(x1) * g_ref[...]).astype(BF16)
    for c in range(w1_ref.shape[1] // ff_chunk):
        cs = slice(c * ff_chunk, (c + 1) * ff_chunk)
        a = jnp.maximum(_dot(h, w1_ref[:, cs]), 0.0)
        a_ref[:, cs] = (a * a).astype(BF16)
    x2 = x1 + _dot(a_ref[...], w2_ref[...])
    if final_norm:
        out_ref[...] = x2 * _inv_rms(x2) * gf_ref[...]
    else:
        out_ref[...] = x2
        _project_layer1(x2, ga_ref, wi_ref, gk_ref, wk_ref, z_ref, bf_refs, k_hbm, v_hbm, kv_buf, sem,
                        seq_len=seq_len)


def _outmlp(x, cat, w, layer, tm, final_norm, seq_len=None, attn_operands=False):
    R, D = x.shape
    FF = w["w_ff1"].shape[-1]
    row = lambda i: (i, 0)
    fix = lambda i: (0, 0)
    n_bf = 3 if attn_operands else 0
    assert not attn_operands or seq_len <= 256 * CHUNK
    in_specs = [pl.BlockSpec((tm, D), row), pl.BlockSpec((tm, D), row),
                _layer(w["w_out"], layer), _layer(_rows(w["g_ffn"]), layer),
                _layer(w["w_ff1"], layer), _layer(w["w_ff2"], layer), _resident((1, D), fix)]
    args = [x, cat, w["w_out"], _rows(w["g_ffn"]), w["w_ff1"], w["w_ff2"], w["g_final"].reshape(1, D)]
    out_specs = [pl.BlockSpec((tm, D), row)]
    out_shape = [jax.ShapeDtypeStruct((R, D), F32)]
    scratch = [pltpu.VMEM((tm, FF), BF16)]
    if not final_norm:
        in_specs += [_layer(_rows(w["g_attn"]), layer + 1), _layer(w["w_in"], layer + 1),
                     _resident((1, D), fix), _layer(w["w_kv"], 0)]
        args += [_rows(w["g_attn"]), w["w_in"], w["g_kv"].reshape(1, D), w["w_kv"]]
        out_specs += ([pl.BlockSpec((tm, 2 * MIX_W), row)] + [pl.BlockSpec((tm, MIX_W), row)] * n_bf
                      + [pl.BlockSpec(memory_space=pl.ANY)] * 2)
        out_shape += ([jax.ShapeDtypeStruct((R, 2 * MIX_W), BF16)] + [jax.ShapeDtypeStruct((R, MIX_W), BF16)] * n_bf
                      + [jax.ShapeDtypeStruct((R, N_HEADS, HEAD_W), F32)] * 2)
        scratch += [pltpu.VMEM((2, tm, 2 * MIX_W), F32), pltpu.SemaphoreType.DMA((2,))]
    return pl.pallas_call(
        functools.partial(_outmlp_kernel, final_norm=final_norm, ff_chunk=1024, n_bf=n_bf, seq_len=seq_len),
        grid=(R // tm,),
        in_specs=in_specs,
        out_specs=out_specs,
        out_shape=out_shape,
        scratch_shapes=scratch,
        compiler_params=_cparams("arbitrary"),
        name="outmlp",
    )(*args)


def _lambda(lq, lam_init):
    a = jnp.sum(lq[0:1] * lq[1:2], axis=-1, keepdims=True)
    b = jnp.sum(lq[2:3] * lq[3:4], axis=-1, keepdims=True)
    return jnp.exp(a) - jnp.exp(b) + lam_init


def _slope2(h):
    return 2.0 ** (-8.0 * (h + 1) / N_HEADS) * LOG2E


def _bf16_pieces(c):
    out = []
    for _ in range(N_SPLIT):
        p = float(np.asarray(c, dtype=ml_dtypes.bfloat16))
        out.append(p)
        c -= p
    return out


def _slope_lanes(h, lo, shape):
    lane = lax.broadcasted_iota(jnp.int32, shape, 1)
    out = jnp.zeros(shape, F32)
    for p, piece in enumerate(_bf16_pieces(_slope2(h))):
        out = jnp.where((lane == lo + p) | (lane == lo + N_SPLIT + p), piece, out)
    return out.astype(BF16)


def _finish_head(o1, o2, lam, gs, lam_init):
    o = o1 - lam * o2
    return o * _inv_rms(o) * gs * (1.0 - lam_init)


def _diff_kernel(lq_ref, gs_ref, q_ref, qn_ref, qm_ref, k10_ref, k20_ref, v0_ref, k1n_ref, k2n_ref, vn_ref,
                 mk_ref, mv_ref, out_ref, kv_ref, qa_ref, acc_ref, m_ref, s_ref, *, tq, lam_init):
    kv0_refs = (k10_ref, k20_ref, v0_ref)
    kvn_refs = (k1n_ref, k2n_ref, vn_ref)
    i = pl.program_id(1)
    n_tiles = pl.num_programs(1)
    nblk = tq // HEAD_W
    n_chain = 2 * N_HEADS
    slot = i % 2
    row = lax.broadcasted_iota(jnp.int32, (tq, tq), 0)
    col = lax.broadcasted_iota(jnp.int32, (tq, tq), 1)
    ahead = jnp.where((col // CHUNK) <= (row // CHUNK), jnp.maximum(col - row, 0).astype(F32), -NEG_BIG)
    ones = jnp.ones((tq, HEAD_W), BF16)
    lane = lax.broadcasted_iota(jnp.int32, (tq, HEAD_W), 1)
    q0 = pl.multiple_of(i * tq, tq)
    qn0 = pl.multiple_of(jnp.minimum(i + 1, n_tiles - 1) * tq, tq)

    def hslice(n):
        return slice((n // 2) * HEAD_W, (n // 2 + 1) * HEAD_W)

    def scores(n, k0, q_slot=slot):
        return _dot_nt(qa_ref[q_slot * n_chain + n], kv_ref[n % 2, pl.ds(k0, tq), hslice(n)])

    def v_aug(n, k0):
        return jnp.concatenate([kv_ref[2, pl.ds(k0, tq), hslice(n)], ones], axis=1)

    def diag_fix(h):
        return (-2.0 * _slope2(h)) * ahead

    def build_queries(src_ref, q_slot):
        for h in range(N_HEADS):
            qh = src_ref[0, :, h * HEAD_W:(h + 1) * HEAD_W]
            base = q_slot * n_chain + 2 * h
            qa_ref[base] = jnp.where(lane < DIFF_D, qh, _slope_lanes(h, DIFF_D, qh.shape))
            qa_ref[base + 1] = jnp.where(lane >= DIFF_D, qh, _slope_lanes(h, 0, qh.shape))

    def consume(n, k0):
        vt = v_aug(n, k0)
        blocks = [s_ref[n, :, b * HEAD_W:(b + 1) * HEAD_W] for b in range(nblk)]
        m_row = jnp.max(functools.reduce(jnp.maximum, blocks), axis=-1, keepdims=True)
        m_old = m_ref[n]
        m_new = jnp.maximum(m_old, m_row)
        p = jnp.concatenate([jnp.exp2(b - m_new) for b in blocks], axis=1).astype(BF16)
        alpha = jnp.exp2(m_old - m_new)
        acc_ref[n] = jnp.concatenate([alpha, alpha], axis=1) * acc_ref[n] + _dot(p, vt)
        m_ref[n] = m_new

    def consume_diag(n):
        half = tq // 2
        vt = v_aug(n, q0)
        for rows, keys in ((slice(0, half), half), (slice(half, tq), tq)):
            blocks = [s_ref[n, rows, b * HEAD_W:(b + 1) * HEAD_W] for b in range(keys // HEAD_W)]
            m_row = jnp.max(functools.reduce(jnp.maximum, blocks), axis=-1, keepdims=True)
            m_new = jnp.broadcast_to(m_row, (half, HEAD_W))
            p = jnp.concatenate([jnp.exp2(b - m_new) for b in blocks], axis=1).astype(BF16)
            acc_ref[n, rows, :] = _dot(p, vt[:keys])
            m_ref[n, rows, :] = m_new

    def stage_next_diag(n):
        s_ref[n] = scores(n, qn0, 1 - slot) + diag_fix(n // 2)

    @pl.when(i == 0)
    def _():
        for c in range(3):
            kv_ref[c, pl.ds(0, tq), :] = kv0_refs[c][0]
        build_queries(q_ref, slot)
        for n in range(n_chain):
            s_ref[n] = scores(n, q0) + diag_fix(n // 2)

    for c in range(3):
        kv_ref[c, pl.ds(qn0, tq), :] = kvn_refs[c][0]
    build_queries(qn_ref, 1 - slot)

    @pl.when(i == 0)
    def _():
        _mem_attend(qm_ref[0], mk_ref, mv_ref, out_ref)
        for n in range(n_chain):
            consume_diag(n)
            stage_next_diag(n)

    @pl.when(i > 0)
    def _():
        _mem_attend(qm_ref[0], mk_ref, mv_ref, out_ref)
        for n in range(n_chain):
            consume_diag(n)
            s_ref[n] = scores(n, 0)

        def body(j, carry):
            k_cur = pl.multiple_of((j - 1) * tq, tq)
            k_next = pl.multiple_of(j * tq, tq)
            for n in range(n_chain):
                consume(n, k_cur)
                s_ref[n] = scores(n, k_next)
            return carry

        lax.fori_loop(1, i, body, 0)
        k_last = pl.multiple_of((i - 1) * tq, tq)
        for n in range(n_chain):
            consume(n, k_last)
            stage_next_diag(n)

    lam = _lambda(lq_ref[...], lam_init)
    for h in range(N_HEADS):
        o = [acc_ref[n, :, :HEAD_W] / acc_ref[n, :, HEAD_W:] for n in (2 * h, 2 * h + 1)]
        out_ref[0, :, h * HEAD_W:(h + 1) * HEAD_W] = _finish_head(
            o[0], o[1], lam, gs_ref[...], lam_init).astype(out_ref.dtype)


def _diff_prompt(z, k1a, k2a, vb, mem_k, mem_v, lq, g_sub, layer, lam_init, tq):
    B, T, _ = z.shape
    n_chain = 2 * N_HEADS
    n_tiles = T // tq
    diff_layer = layer - (mem_k.shape[0] - lq.shape[0])
    nxt = lambda i: jnp.minimum(i + 1, n_tiles - 1)
    return pl.pallas_call(
        functools.partial(_diff_kernel, tq=tq, lam_init=lam_init),
        grid=(B, n_tiles),
        in_specs=[_layer(lq, diff_layer), _layer(_rows(g_sub), diff_layer),
                  pl.BlockSpec((1, tq, MIX_W), lambda b, i: (b, i, 0)),
                  pl.BlockSpec((1, tq, MIX_W), lambda b, i: (b, nxt(i), 0)),
                  pl.BlockSpec((1, tq, MIX_W), lambda b, i: (b, i, 1))]
                 + [_resident((1, tq, MIX_W), lambda b, i: (b, 0, 0))] * 3
                 + [pl.BlockSpec((1, tq, MIX_W), lambda b, i: (b, nxt(i), 0))] * 3
                 + [_batch_spec(mem_k, layer), _batch_spec(mem_v, layer)],
        out_specs=pl.BlockSpec((1, tq, 2 * MIX_W), lambda b, i: (b, i, 0)),
        out_shape=jax.ShapeDtypeStruct((B, T, 2 * MIX_W), BF16),
        scratch_shapes=[pltpu.VMEM((3, T, MIX_W), BF16),
                        pltpu.VMEM((2 * n_chain, tq, HEAD_W), BF16),
                        pltpu.VMEM((n_chain, tq, 2 * HEAD_W), F32),
                        pltpu.VMEM((n_chain, tq, HEAD_W), F32),
                        pltpu.VMEM((n_chain, tq, tq), F32)],
        compiler_params=_cparams("arbitrary", "arbitrary"),
        name="diff_prompt",
    )(lq, _rows(g_sub), z, z, z, k1a, k2a, vb, k1a, k2a, vb, mem_k, mem_v)


def _cache_copies(hbm_refs, b, buf, slot, sem):
    return [pltpu.make_async_copy(hbm.at[b, :, h, :], buf.at[slot, t, h], sem.at[slot])
            for t, hbm in enumerate(hbm_refs) for h in range(N_HEADS)]


def _diff_sample_kernel(lq_ref, gs_ref, q_ref, qm_ref, ck_hbm, cv_hbm, nk_ref, nv_ref, mk_ref, mv_ref, out_ref,
                        cache_buf, sem, *, p0, lam_init):
    b = pl.program_id(0)
    slot = b % 2
    copies = functools.partial(_cache_copies, (ck_hbm, cv_hbm), buf=cache_buf, sem=sem)

    @pl.when(b == 0)
    def _():
        for c in copies(b, slot=slot):
            c.start()

    @pl.when(b + 1 < pl.num_programs(0))
    def _():
        for c in copies(b + 1, slot=1 - slot):
            c.start()

    for c in copies(b, slot=slot):
        c.wait()

    ts = q_ref.shape[1]
    lam = _lambda(lq_ref[...], lam_init)
    qpos = p0 + lax.broadcasted_iota(jnp.int32, (2 * ts, p0), 0) % ts
    dist_p = (qpos - lax.broadcasted_iota(jnp.int32, (2 * ts, p0), 1)).astype(F32)
    row = p0 + lax.broadcasted_iota(jnp.int32, (2 * ts, ts), 0) % ts
    col = p0 + lax.broadcasted_iota(jnp.int32, (2 * ts, ts), 1)
    dist_n = jnp.abs(row - col).astype(F32)
    vis_n = (col // CHUNK) <= (row // CHUNK)
    lane = lax.broadcasted_iota(jnp.int32, (ts, HEAD_W), 1)
    for h in range(N_HEADS):
        hs = slice(h * HEAD_W, (h + 1) * HEAD_W)
        qh = q_ref[0, :, hs]
        zero = jnp.zeros_like(qh)
        q2 = jnp.concatenate([jnp.where(lane < DIFF_D, qh, zero), jnp.where(lane >= DIFF_D, qh, zero)], axis=0)
        kp = cache_buf[slot, 0, h].astype(BF16)
        vp = cache_buf[slot, 1, h].astype(BF16)
        kn = _head(nk_ref, h).astype(BF16)
        vn = _head(nv_ref, h).astype(BF16)
        sp = _dot_nt(q2, kp) - _slope2(h) * dist_p
        sn = _dot_nt(q2, kn) + jnp.where(vis_n, -_slope2(h) * dist_n, NEG_BIG)
        m = jnp.maximum(jnp.max(sp, axis=-1, keepdims=True), jnp.max(sn, axis=-1, keepdims=True))
        ep = jnp.exp2(sp - m)
        en = jnp.exp2(sn - m)
        l = jnp.sum(ep, axis=-1, keepdims=True) + jnp.sum(en, axis=-1, keepdims=True)
        o = (_dot(ep.astype(BF16), vp) + _dot(en.astype(BF16), vn)) / l
        out_ref[0, :, hs] = _finish_head(o[:ts], o[ts:], lam, gs_ref[...], lam_init).astype(out_ref.dtype)
    _mem_attend(qm_ref[0], mk_ref, mv_ref, out_ref)


def _diff_sample(z, cache_k, cache_v, k_new, v_new, mem_k, mem_v, lq, g_sub, layer, lam_init):
    B, Ts, _ = z.shape
    P0 = cache_k.shape[1]
    diff_layer = layer - (mem_k.shape[0] - lq.shape[0])
    return pl.pallas_call(
        functools.partial(_diff_sample_kernel, p0=P0, lam_init=lam_init),
        grid=(B,),
        in_specs=[_layer(lq, diff_layer), _layer(_rows(g_sub), diff_layer),
                  pl.BlockSpec((1, Ts, MIX_W), lambda b: (b, 0, 0)),
                  pl.BlockSpec((1, Ts, MIX_W), lambda b: (b, 0, 1)),
                  pl.BlockSpec(memory_space=pl.ANY), pl.BlockSpec(memory_space=pl.ANY),
                  _batch_spec(k_new), _batch_spec(v_new), _batch_spec(mem_k, layer), _batch_spec(mem_v, layer)],
        out_specs=pl.BlockSpec((1, Ts, 2 * MIX_W), lambda b: (b, 0, 0)),
        out_shape=jax.ShapeDtypeStruct((B, Ts, 2 * MIX_W), BF16),
        scratch_shapes=[pltpu.VMEM((2, 2, N_HEADS, P0, HEAD_W), F32), pltpu.SemaphoreType.DMA((2,))],
        compiler_params=_cparams("arbitrary"),
        name="diff_sample",
    )(lq, _rows(g_sub), z, z, cache_k, cache_v, k_new, v_new, mem_k, mem_v)


def _lambda_init(layer_idx):
    return 0.8 - 0.6 * math.exp(-0.3 * layer_idx)


def _trunk(x, pos0, pool_hist, mem_k, mem_v, past_k, past_v, w, tm_rows, tm_seq, tq):
    B, T, D = x.shape
    R = B * T
    xr = x.reshape(R, D)
    hist16 = jnp.pad(pool_hist, ((0, 0), (HALO - POOL_HIST, 0), (0, 0)))

    to_cast = [name for name in ("w_ff1", "w_ff2") if w[name].dtype != BF16]
    cat0, tail, *cast = _mix0(x, w["g_attn"], w["w_in"], hist16, mem_k, mem_v, w["w_pool"], w["pool_scale"], 0,
                              tm_seq, pos0, cast=[w[name] for name in to_cast])
    w = dict(w, **dict(zip(to_cast, cast)))
    new_pool = tail[:, HALO - POOL_HIST:]
    prompt = past_k is None
    x1, z, *bf, k, v = _outmlp(xr, cat0.reshape(R, D), w, 0, tm_rows, final_norm=False, seq_len=T,
                               attn_operands=prompt)
    z3 = z.reshape(B, T, D)
    lam_init = _lambda_init(1)
    if prompt:
        k1a, k2a, vb = (a.reshape(B, T, MIX_W) for a in bf)
        cat1 = _diff_prompt(z3, k1a, k2a, vb, mem_k, mem_v, w["lambda_qk"], w["g_subln"], 1, lam_init, tq)
    else:
        shp = (B, T, N_HEADS, HEAD_W)
        cat1 = _diff_sample(z3, past_k, past_v, k.reshape(shp), v.reshape(shp), mem_k, mem_v,
                            w["lambda_qk"], w["g_subln"], 1, lam_init)
    y, = _outmlp(x1, cat1.reshape(R, D), w, 1, min(R, 2 * tm_rows), final_norm=True)
    shp = (B, T, N_HEADS, HEAD_W)
    return y.reshape(B, T, D), new_pool[None], k.reshape(shp), v.reshape(shp), w


def kernel(x_prompt, x_sample, mem_prompt, cache_k, cache_v, cache_mem_k, cache_mem_v, state_pool, g_attn, w_in, w_out, g_mem, w_mem_kv, g_ffn, w_ff1, w_ff2, w_pool, pool_scale, lambda_qk, g_subln, g_kv, w_kv, g_final):
    B, T, D = x_prompt.shape
    Bs, Ts, _ = x_sample.shape
    P0 = cache_k.shape[1]
    mem_k_p, mem_v_p, mem_k_d, mem_v_d, w_in_b, w_out_b, w_kv_b = _mem_kv(
        mem_prompt, g_mem, w_mem_kv, cast=(w_in, w_out, w_kv[None]))
    w = dict(g_attn=g_attn, w_in=w_in_b, w_out=w_out_b, g_ffn=g_ffn,
             w_ff1=w_ff1, w_ff2=w_ff2, w_pool=w_pool.astype(BF16),
             pool_scale=pool_scale, lambda_qk=lambda_qk, g_subln=g_subln, g_kv=g_kv,
             w_kv=w_kv_b, g_final=g_final)

    hist0 = jnp.zeros((B, POOL_HIST, MIX_W), F32)
    y_p, pool_p, k_p, v_p, w = _trunk(x_prompt, 0, hist0, mem_k_d, mem_v_d, None, None, w,
                                      tm_rows=512, tm_seq=2048, tq=512)
    y_s, pool_s, k_s, v_s, _ = _trunk(x_sample, P0, state_pool[0], cache_mem_k, cache_mem_v, cache_k, cache_v, w,
                                      tm_rows=Bs * Ts, tm_seq=Ts, tq=None)
    return (y_p, y_s, mem_k_p, mem_v_p, pool_p, k_p, v_p, pool_s, k_s, v_s)
```
